```python
import jax, jax.numpy as jnp
from jax import lax
import numpy as np

D_MODEL = 1024
BATCH = 4
SEQ = 4096
DEPTH = 2
DEC_BATCH = 128
DEC_SEQ = 1
PAST_LEN = 16384
PAGE_SIZE = 128

N_A_LAYERS = DEPTH // 2
N_B_LAYERS = DEPTH - N_A_LAYERS
POOL_WINDOWS = (2, 4, 8, 16)
N_POOL_GROUPS = len(POOL_WINDOWS)
POOL_GROUP = D_MODEL // N_POOL_GROUPS
POOL_BUF = max(POOL_WINDOWS) - 1
HEAD_DIM = 64
N_HEADS = D_MODEL // HEAD_DIM
N_KV_HEADS = 4
GROUP = N_HEADS // N_KV_HEADS
KV_DIM = N_KV_HEADS * HEAD_DIM
WINDOW = 128
BLOCK = WINDOW
ROT_DIM = HEAD_DIM // 4
ROPE_THETA = 500000.0
D_FF = 2816
EPS = 1e-5

kernel_name = 'yoco_pool_swa_sink_macaron_step'


def _rmsnorm(x, g):
    xf = x.astype(jnp.float32)
    y = xf * lax.rsqrt(jnp.mean(xf * xf, axis=-1, keepdims=True) + EPS)
    return (y * g.astype(jnp.float32)).astype(x.dtype)


def _swiglu(x, wg, wu, wd):
    return (jax.nn.silu(x @ wg) * (x @ wu)) @ wd


def _rope(x, pos):
    half = ROT_DIM // 2
    inv = jnp.power(ROPE_THETA, -jnp.arange(half, dtype=jnp.float32) * (2.0 / ROT_DIM))
    ang = pos.astype(jnp.float32)[:, None] * inv[None, :]
    shp = (1, ang.shape[0]) + (1,) * (x.ndim - 3) + (half,)
    cos = jnp.cos(ang).reshape(shp)
    sin = jnp.sin(ang).reshape(shp)
    xf = x.astype(jnp.float32)
    x1 = xf[..., :half]
    x2 = xf[..., half:ROT_DIM]
    out = jnp.concatenate([x1 * cos - x2 * sin, x2 * cos + x1 * sin, xf[..., ROT_DIM:]], axis=-1)
    return out.astype(x.dtype)


def _pool_mixer(u, buf, pos0, w_pool, scale):
    B, T, D = u.shape
    ext_raw = jnp.concatenate([buf.astype(u.dtype), u], axis=1)
    ext = ext_raw.astype(jnp.float32)
    cs = jnp.concatenate([jnp.zeros((B, 1, D), jnp.float32), jnp.cumsum(ext, axis=1)], axis=1)
    end = cs[:, POOL_BUF + 1:POOL_BUF + 1 + T]
    pos = pos0 + jnp.arange(T)
    means = []
    for gi, w in enumerate(POOL_WINDOWS):
        c0, c1 = gi * POOL_GROUP, (gi + 1) * POOL_GROUP
        start = cs[:, POOL_BUF + 1 - w:POOL_BUF + 1 - w + T, c0:c1]
        cnt = jnp.minimum(w, pos + 1).astype(jnp.float32)[None, :, None]
        means.append((end[..., c0:c1] - start) / cnt)
    pooled = jnp.concatenate(means, axis=-1)
    diff = (pooled - u.astype(jnp.float32)).reshape(B, T, N_POOL_GROUPS, POOL_GROUP)
    mixed = jnp.einsum('btgc,gcd->btgd', diff, w_pool.astype(jnp.float32)).reshape(B, T, D)
    out = (mixed * scale.astype(jnp.float32)).astype(u.dtype)
    return out, ext_raw[:, -POOL_BUF:]


def _sink_attention(q, k, v, mask, sinks):
    s = jnp.einsum('bnqkgd,bnskd->bnkgqs', q.astype(jnp.float32), k.astype(jnp.float32)) * (HEAD_DIM ** -0.5)
    s = jnp.where(mask[None, :, None, None], s, -jnp.inf)
    sink = sinks.astype(jnp.float32).reshape(1, 1, N_KV_HEADS, GROUP, 1, 1)
    m = jnp.maximum(jnp.max(s, axis=-1, keepdims=True), sink)
    p = jnp.exp(s - m)
    denom = jnp.sum(p, axis=-1, keepdims=True) + jnp.exp(sink - m)
    return jnp.einsum('bnkgqs,bnskd->bnqkgd', p / denom, v.astype(jnp.float32))


def _window_attention(q, kk, vv, wb, pos0, sinks):
    B, T = q.shape[0], q.shape[1]
    qpos = pos0 + jnp.arange(T)
    kpos = pos0 - wb + jnp.arange(wb + T)
    if T % BLOCK == 0 and wb == BLOCK:
        nb = T // BLOCK
        qb = q.reshape(B, nb, BLOCK, N_KV_HEADS, GROUP, HEAD_DIM)
        kb = jnp.concatenate([kk[:, :T].reshape(B, nb, BLOCK, N_KV_HEADS, HEAD_DIM),
                              kk[:, wb:].reshape(B, nb, BLOCK, N_KV_HEADS, HEAD_DIM)], axis=2)
        vb = jnp.concatenate([vv[:, :T].reshape(B, nb, BLOCK, N_KV_HEADS, HEAD_DIM),
                              vv[:, wb:].reshape(B, nb, BLOCK, N_KV_HEADS, HEAD_DIM)], axis=2)
        qp = qpos.reshape(nb, BLOCK)
        kp = jnp.concatenate([kpos[:T].reshape(nb, BLOCK), kpos[wb:].reshape(nb, BLOCK)], axis=1)
    else:
        qb, kb, vb = q[:, None], kk[:, None], vv[:, None]
        qp, kp = qpos[None], kpos[None]
    d = qp[:, :, None] - kp[:, None, :]
    mask = (d >= 0) & (d < WINDOW) & (kp[:, None, :] >= 0)
    o = _sink_attention(qb, kb, vb, mask, sinks)
    return o.reshape(B, T, N_HEADS * HEAD_DIM).astype(q.dtype)


def _trunk(x, pos0, pool_bufs, k_buf, v_buf, norm_g, ffn_w_gate, ffn_w_up, ffn_w_down, pool_w,
           pool_scale, kv_norm_g, w_kv, w_q, w_o, attn_sinks, final_norm_g):
    B, T, _ = x.shape
    pos = pos0 + jnp.arange(T)
    wb = k_buf.shape[1]
    new_pool = []
    kk = vv = None
    for l in range(DEPTH):
        if l == N_A_LAYERS:
            kv = _rmsnorm(x, kv_norm_g) @ w_kv
            k_new = _rope(kv[..., :KV_DIM].reshape(B, T, N_KV_HEADS, HEAD_DIM), pos)
            v_new = kv[..., KV_DIM:].reshape(B, T, N_KV_HEADS, HEAD_DIM)
            kk = jnp.concatenate([k_buf.astype(k_new.dtype), k_new], axis=1)
            vv = jnp.concatenate([v_buf.astype(v_new.dtype), v_new], axis=1)
        x = x + 0.5 * _swiglu(_rmsnorm(x, norm_g[l, 0]), ffn_w_gate[l, 0], ffn_w_up[l, 0], ffn_w_down[l, 0])
        h = _rmsnorm(x, norm_g[l, 1])
        if l < N_A_LAYERS:
            out, nbuf = _pool_mixer(h, pool_bufs[l], pos0, pool_w[l], pool_scale[l])
            new_pool.append(nbuf)
        else:
            j = l - N_A_LAYERS
            q = _rope((h @ w_q[j]).reshape(B, T, N_KV_HEADS, GROUP, HEAD_DIM), pos)
            out = _window_attention(q, kk, vv, wb, pos0, attn_sinks[j]) @ w_o[j]
        x = x + out
        x = x + 0.5 * _swiglu(_rmsnorm(x, norm_g[l, 2]), ffn_w_gate[l, 1], ffn_w_up[l, 1], ffn_w_down[l, 1])
    y = _rmsnorm(x, final_norm_g)
    return y, jnp.stack(new_pool, axis=0), kk[:, -wb:], vv[:, -wb:]


def setup_inputs(seed: int = 0) -> dict:
    key = jax.random.key(seed)
    ks = jax.random.split(key, 20)
    f32 = jnp.float32
    wb = min(WINDOW, PAST_LEN)
    nrm = lambda k, s, sc: jax.random.normal(k, s, f32) * sc
    return {
        'x_prompt': nrm(ks[0], (BATCH, SEQ, D_MODEL), 1.0),
        'x_sample': nrm(ks[1], (DEC_BATCH, DEC_SEQ, D_MODEL), 1.0),
        'state_pool': nrm(ks[2], (N_A_LAYERS, DEC_BATCH, POOL_BUF, D_MODEL), 1.0),
        'cache_k_win': nrm(ks[3], (DEC_BATCH, wb, N_KV_HEADS, HEAD_DIM), 1.0),
        'cache_v_win': nrm(ks[4], (DEC_BATCH, wb, N_KV_HEADS, HEAD_DIM), 1.0),
        'norm_g': 1.0 + nrm(ks[5], (DEPTH, 3, D_MODEL), 0.05),
        'ffn_w_gate': nrm(ks[6], (DEPTH, 2, D_MODEL, D_FF), D_MODEL ** -0.5),
        'ffn_w_up': nrm(ks[7], (DEPTH, 2, D_MODEL, D_FF), D_MODEL ** -0.5),
        'ffn_w_down': nrm(ks[8], (DEPTH, 2, D_FF, D_MODEL), D_FF ** -0.5),
        'pool_w': nrm(ks[9], (N_A_LAYERS, N_POOL_GROUPS, POOL_GROUP, POOL_GROUP), POOL_GROUP ** -0.5),
        'pool_scale': 1.0 + nrm(ks[10], (N_A_LAYERS, D_MODEL), 0.1),
        'kv_norm_g': 1.0 + nrm(ks[11], (D_MODEL,), 0.05),
        'w_kv': nrm(ks[12], (D_MODEL, 2 * KV_DIM), D_MODEL ** -0.5),
        'w_q': nrm(ks[13], (N_B_LAYERS, D_MODEL, N_HEADS * HEAD_DIM), D_MODEL ** -0.5),
        'w_o': nrm(ks[14], (N_B_LAYERS, N_HEADS * HEAD_DIM, D_MODEL), (N_HEADS * HEAD_DIM) ** -0.5),
        'attn_sinks': nrm(ks[15], (N_B_LAYERS, N_HEADS), 0.5),
        'final_norm_g': 1.0 + nrm(ks[16], (D_MODEL,), 0.05),
    }


def reference(x_prompt, x_sample, state_pool, cache_k_win, cache_v_win, norm_g, ffn_w_gate, ffn_w_up,
              ffn_w_down, pool_w, pool_scale, kv_norm_g, w_kv, w_q, w_o, attn_sinks, final_norm_g):
    Bp = x_prompt.shape[0]
    pool0 = jnp.zeros((N_A_LAYERS, Bp, POOL_BUF, D_MODEL), x_prompt.dtype)
    kbuf0 = jnp.zeros((Bp, WINDOW, N_KV_HEADS, HEAD_DIM), x_prompt.dtype)
    y_prompt, pool_prompt, k_win_prompt, v_win_prompt = _trunk(
        x_prompt, 0, pool0, kbuf0, kbuf0, norm_g, ffn_w_gate, ffn_w_up, ffn_w_down, pool_w,
        pool_scale, kv_norm_g, w_kv, w_q, w_o, attn_sinks, final_norm_g)
    y_sample, pool_sample, k_win_sample, v_win_sample = _trunk(
        x_sample, PAST_LEN, state_pool, cache_k_win, cache_v_win, norm_g, ffn_w_gate, ffn_w_up,
        ffn_w_down, pool_w, pool_scale, kv_norm_g, w_kv, w_q, w_o, attn_sinks, final_norm_g)
    return (y_prompt, y_sample, pool_prompt, pool_sample, k_win_prompt, v_win_prompt, k_win_sample, v_win_sample)
```

```python
import functools

import numpy as np
import jax
import jax.numpy as jnp
from jax import lax
from jax.experimental import pallas as pl
from jax.experimental.pallas import tpu as pltpu

F32 = jnp.float32
BF16 = jnp.bfloat16

D_MODEL = 1024
D_FF = 2816
HEAD_DIM = 64
N_HEADS = 16
N_KV = 4
GROUP = 4
KV_DIM = N_KV * HEAD_DIM
WINDOW = 128
ROT_DIM = 16
ROPE_THETA = 500000.0
EPS = 1e-5
POOL_WINDOWS = (2, 4, 8, 16)
POOL_GROUP = 256
POOL_BUF = 15
PAST_LEN = 16384

LANES = 128
FF_CHUNK = 256
N_FF_CHUNKS = D_FF // FF_CHUNK
OUT_CHUNK = 256
N_OUT_CHUNKS = D_MODEL // OUT_CHUNK
VMEM_LIMIT = 56 * 1024 * 1024


def _const_spec(shape):
    nd = len(shape)
    return pl.BlockSpec(shape, lambda *_: (0,) * nd, pipeline_mode=pl.Buffered(1))


def _rms(x, g):
    ms = jnp.mean(x * x, axis=-1, keepdims=True)
    return x * lax.rsqrt(ms + EPS) * g


def _rope(x, c, s1, s2):
    pieces = []
    for j in range(x.shape[1] // LANES):
        xc = x[:, j * LANES:(j + 1) * LANES]
        up = pltpu.roll(xc, LANES - ROT_DIM // 2, axis=1)
        dn = pltpu.roll(xc, ROT_DIM // 2, axis=1)
        pieces.append(xc * c + up * s1 + dn * s2)
    return jnp.concatenate(pieces, axis=1) if len(pieces) > 1 else pieces[0]


def _rope_tables(pos):
    half = ROT_DIM // 2
    inv = jnp.power(ROPE_THETA, -jnp.arange(half, dtype=F32) * (2.0 / ROT_DIM))
    ang = pos.astype(F32)[:, None] * inv[None, :]
    cos, sin = jnp.cos(ang), jnp.sin(ang)
    n = pos.shape[0]
    ones = jnp.ones((n, HEAD_DIM - ROT_DIM), F32)
    zeros8 = jnp.zeros((n, half), F32)
    zeros = jnp.zeros((n, HEAD_DIM - ROT_DIM), F32)
    c = jnp.concatenate([cos, cos, ones], axis=1)
    s1 = jnp.concatenate([-sin, zeros8, zeros], axis=1)
    s2 = jnp.concatenate([zeros8, sin, zeros], axis=1)
    tile = lambda t: jnp.concatenate([t, t], axis=1)
    return tile(c), tile(s1), tile(s2)


def _ffn_kernel(*refs, has_attn, has_final):
    it = iter(refs)
    x_ref = next(it)
    if has_attn:
        a_ref, wo_ref = next(it), next(it)
    g_ref, wgu_ref, wd_ref = next(it), next(it), next(it)
    if has_final:
        gf_ref = next(it)
    o_ref = next(it)
    xn_ref, h_ref = next(it), next(it)
    if has_attn:
        xr_ref = next(it)
        xr_ref[...] = x_ref[...] + jnp.dot(a_ref[...], wo_ref[...], preferred_element_type=F32)
        res_ref = xr_ref
    else:
        res_ref = x_ref

    xn_ref[...] = _rms(res_ref[...], g_ref[...]).astype(BF16)
    for j in range(N_FF_CHUNKS):
        r = jnp.dot(xn_ref[...], wgu_ref[j], preferred_element_type=F32)
        gate, up = r[:, :FF_CHUNK], r[:, FF_CHUNK:]
        h_ref[:, j * FF_CHUNK:(j + 1) * FF_CHUNK] = (gate * jax.nn.sigmoid(gate) * up).astype(BF16)
    for n in range(N_OUT_CHUNKS):
        sl = slice(n * OUT_CHUNK, (n + 1) * OUT_CHUNK)
        y = jnp.dot(h_ref[...], wd_ref[n], preferred_element_type=F32)
        o_ref[:, sl] = res_ref[:, sl] + 0.5 * y
    if has_final:
        o_ref[...] = _rms(o_ref[...], gf_ref[...])


def _ffn(x, g, wgu, wd, *, tm, attn=None, wo=None, final_g=None):
    m = x.shape[0]
    has_attn, has_final = attn is not None, final_g is not None
    row = pl.BlockSpec((tm, D_MODEL), lambda i: (i, 0))
    args, specs = [x], [row]
    if has_attn:
        args += [attn, wo]
        specs += [row, _const_spec(wo.shape)]
    args += [g, wgu, wd]
    specs += [_const_spec(g.shape), _const_spec(wgu.shape), _const_spec(wd.shape)]
    if has_final:
        args.append(final_g)
        specs.append(_const_spec(final_g.shape))
    scratch = [pltpu.VMEM((tm, D_MODEL), BF16), pltpu.VMEM((tm, D_FF), BF16)]
    if has_attn:
        scratch.append(pltpu.VMEM((tm, D_MODEL), F32))
    return pl.pallas_call(
        functools.partial(_ffn_kernel, has_attn=has_attn, has_final=has_final),
        grid=(m // tm,),
        in_specs=specs,
        out_specs=row,
        out_shape=jax.ShapeDtypeStruct((m, D_MODEL), F32),
        scratch_shapes=scratch,
        compiler_params=pltpu.CompilerParams(
            dimension_semantics=("arbitrary",), vmem_limit_bytes=VMEM_LIMIT),
        name="ffn",
    )(*args)


HALO = 16


def _pool_prompt_kernel(x_ref, g_ref, w_ref, sc_ref, o_ref, buf_ref, hext_ref, *, tp):
    t = pl.program_id(1)

    @pl.when(t == 0)
    def _():
        hext_ref[0:HALO, :] = jnp.zeros((HALO, D_MODEL), F32)

    x = x_ref[0]
    h = _rms(x, g_ref[...])
    hext_ref[HALO:HALO + tp, :] = h
    pos = (t * tp + lax.broadcasted_iota(jnp.int32, (tp, 1), 0)).astype(F32)
    mixed = []
    for gi, w in enumerate(POOL_WINDOWS):
        cs = slice(gi * POOL_GROUP, (gi + 1) * POOL_GROUP)
        s = h[:, cs]
        for k in range(1, w):
            s = s + hext_ref[HALO - k:HALO - k + tp, cs]
        cnt = jnp.minimum(float(w), pos + 1.0)
        diff = (s / cnt - h[:, cs]).astype(BF16)
        mixed.append(jnp.dot(diff, w_ref[gi], preferred_element_type=F32))
    out = jnp.concatenate(mixed, axis=1) * sc_ref[...]
    o_ref[0] = x + out
    @pl.when(t == pl.num_programs(1) - 1)
    def _():
        buf_ref[0] = hext_ref[HALO + tp - POOL_BUF:HALO + tp, :]

    hext_ref[0:HALO, :] = hext_ref[tp:tp + HALO, :]


def _pool_prompt(x, g, w, sc, *, tp):
    b, t, _ = x.shape
    blk = pl.BlockSpec((1, tp, D_MODEL), lambda i, j: (i, j, 0))
    return pl.pallas_call(
        functools.partial(_pool_prompt_kernel, tp=tp),
        grid=(b, t // tp),
        in_specs=[blk, _const_spec(g.shape), _const_spec(w.shape), _const_spec(sc.shape)],
        out_specs=[blk, pl.BlockSpec((1, POOL_BUF, D_MODEL), lambda i, j: (i, 0, 0))],
        out_shape=[jax.ShapeDtypeStruct(x.shape, F32),
                   jax.ShapeDtypeStruct((b, POOL_BUF, D_MODEL), F32)],
        scratch_shapes=[pltpu.VMEM((HALO + tp, D_MODEL), F32)],
        compiler_params=pltpu.CompilerParams(
            dimension_semantics=("arbitrary", "arbitrary"), vmem_limit_bytes=VMEM_LIMIT),
        name="pool_prompt",
    )(x, g, w, sc)


def _pool_sample_kernel(x_ref, buf_ref, g_ref, w_ref, sc_ref, o_ref, h_ref):
    x = x_ref[...]
    h = _rms(x, g_ref[...])
    h_ref[...] = h
    mixed = []
    for gi, w in enumerate(POOL_WINDOWS):
        cs = slice(gi * POOL_GROUP, (gi + 1) * POOL_GROUP)
        s = h[:, cs]
        for k in range(1, w):
            s = s + buf_ref[POOL_BUF - k, :, cs]
        diff = (s / float(w) - h[:, cs]).astype(BF16)
        mixed.append(jnp.dot(diff, w_ref[gi], preferred_element_type=F32))
    o_ref[...] = x + jnp.concatenate(mixed, axis=1) * sc_ref[...]


def _pool_sample(x, buf_t, g, w, sc):
    m = x.shape[0]
    return pl.pallas_call(
        _pool_sample_kernel,
        out_shape=[jax.ShapeDtypeStruct((m, D_MODEL), F32), jax.ShapeDtypeStruct((m, D_MODEL), F32)],
        compiler_params=pltpu.CompilerParams(vmem_limit_bytes=VMEM_LIMIT),
        name="pool_sample",
    )(x, buf_t, g, w, sc)


def _proj_rope_kernel(x_ref, g_ref, w_ref, c_ref, s1_ref, s2_ref, o_ref, *, rope_cols, scale, bcast):
    n = _rms(x_ref[...], g_ref[...]).astype(BF16)
    y = jnp.dot(n, w_ref[...], preferred_element_type=F32)
    rows = y.shape[0]
    tabs = [r[...] for r in (c_ref, s1_ref, s2_ref)]
    if bcast:
        tabs = [jnp.broadcast_to(t, (rows, LANES)) for t in tabs]
    rot = _rope(y[:, :rope_cols], *tabs)
    if scale != 1.0:
        rot = rot * scale
    o_ref[:, :rope_cols] = rot.astype(o_ref.dtype)
    if rope_cols < y.shape[1]:
        o_ref[:, rope_cols:] = y[:, rope_cols:].astype(o_ref.dtype)


def _proj_rope(x, g, w, tabs, *, tm, rope_cols, scale=1.0, out_dtype=F32):
    m, n = x.shape[0], w.shape[1]
    bcast = tabs[0].shape[0] == 1
    period = tabs[0].shape[0] // tm
    tspec = _const_spec((1, LANES)) if bcast else pl.BlockSpec((tm, LANES), lambda i: (i % period, 0))
    return pl.pallas_call(
        functools.partial(_proj_rope_kernel, rope_cols=rope_cols, scale=scale, bcast=bcast),
        grid=(m // tm,),
        in_specs=[pl.BlockSpec((tm, D_MODEL), lambda i: (i, 0)), _const_spec(g.shape),
                  _const_spec(w.shape), tspec, tspec, tspec],
        out_specs=pl.BlockSpec((tm, n), lambda i: (i, 0)),
        out_shape=jax.ShapeDtypeStruct((m, n), out_dtype),
        compiler_params=pltpu.CompilerParams(
            dimension_semantics=("arbitrary",), vmem_limit_bytes=VMEM_LIMIT),
        name="proj_rope",
    )(x, g, w, *tabs)


def _seg_mask(rows):
    lane = lax.broadcasted_iota(jnp.int32, (rows, KV_DIM), 1)
    return [(lane >= kv * HEAD_DIM) & (lane < (kv + 1) * HEAD_DIM) for kv in range(N_KV)]


def _attn_prompt_kernel(x_ref, g_ref, wq_ref, c_ref, s1_ref, s2_ref, k_ref, v_ref, sink_ref,
                        o_ref, kbuf, vbuf, *, tq):
    n = pl.program_id(1)

    @pl.when(n == 0)
    def _():
        kbuf[0:WINDOW, :] = jnp.zeros((WINDOW, KV_DIM), F32)
        vbuf[0:WINDOW, :] = jnp.zeros((WINDOW, KV_DIM), F32)

    kbuf[WINDOW:WINDOW + tq, :] = k_ref[...]
    vbuf[WINDOW:WINDOW + tq, :] = v_ref[...]

    hn = _rms(x_ref[...], g_ref[...]).astype(BF16)
    q = jnp.dot(hn, wq_ref[...], preferred_element_type=F32)
    q = (_rope(q, c_ref[...], s1_ref[...], s2_ref[...]) * (HEAD_DIM ** -0.5)).astype(BF16)

    seg2 = _seg_mask(2 * WINDOW)
    seg1 = _seg_mask(WINDOW)
    qi = lax.broadcasted_iota(jnp.int32, (WINDOW, 2 * WINDOW), 0)
    kj = lax.broadcasted_iota(jnp.int32, (WINDOW, 2 * WINDOW), 1)
    in_window = ((kj < WINDOW) & (kj > qi)) | ((kj >= WINDOW) & (kj - WINDOW <= qi))
    first_key = jnp.where(n > 0, 0, WINDOW)

    for i in range(tq // WINDOW):
        kk = kbuf[i * WINDOW:(i + 2) * WINDOW, :]
        vv = vbuf[i * WINDOW:(i + 2) * WINDOW, :]
        kbd = jnp.concatenate([jnp.where(m, kk, 0.0).astype(BF16) for m in seg2], axis=0)
        vbd = jnp.concatenate([jnp.where(m, vv, 0.0).astype(BF16) for m in seg2], axis=0)
        mask = (in_window & (kj >= first_key)) if i == 0 else in_window
        for gm in range(GROUP):
            qg = q[i * WINDOW:(i + 1) * WINDOW, gm * KV_DIM:(gm + 1) * KV_DIM]
            s = lax.dot_general(qg, kbd, (((1,), (1,)), ((), ())), preferred_element_type=F32)
            ps, rinv = [], jnp.zeros((WINDOW, KV_DIM), F32)
            for kv in range(N_KV):
                skv = jnp.where(mask, s[:, kv * 2 * WINDOW:(kv + 1) * 2 * WINDOW], -jnp.inf)
                sink = sink_ref[gm, kv]
                mx = jnp.maximum(jnp.max(skv, axis=-1, keepdims=True), sink)
                p = jnp.exp(skv - mx)
                den = jnp.sum(p, axis=-1, keepdims=True) + jnp.exp(sink - mx)
                ps.append(p.astype(BF16))
                rinv = jnp.where(seg1[kv], 1.0 / den, rinv)
            pcat = jnp.concatenate(ps, axis=1)
            og = jnp.dot(pcat, vbd, preferred_element_type=F32) * rinv
            o_ref[i * WINDOW:(i + 1) * WINDOW, gm * KV_DIM:(gm + 1) * KV_DIM] = og.astype(o_ref.dtype)

    kbuf[0:WINDOW, :] = kbuf[tq:tq + WINDOW, :]
    vbuf[0:WINDOW, :] = vbuf[tq:tq + WINDOW, :]


def _attn_prompt(x, g, wq, tabs, k, v, sinks, *, batch, tq):
    m = x.shape[0]
    nb = m // batch // tq
    row = lambda w: pl.BlockSpec((tq, w), lambda b, j: (b * nb + j, 0))
    return pl.pallas_call(
        functools.partial(_attn_prompt_kernel, tq=tq),
        grid=(batch, nb),
        in_specs=[row(D_MODEL), _const_spec(g.shape), _const_spec(wq.shape),
                  pl.BlockSpec((tq, LANES), lambda b, j: (j, 0)),
                  pl.BlockSpec((tq, LANES), lambda b, j: (j, 0)),
                  pl.BlockSpec((tq, LANES), lambda b, j: (j, 0)),
                  row(KV_DIM), row(KV_DIM), pl.BlockSpec(memory_space=pltpu.SMEM)],
        out_specs=row(D_MODEL),
        out_shape=jax.ShapeDtypeStruct((m, D_MODEL), BF16),
        scratch_shapes=[pltpu.VMEM((WINDOW + tq, KV_DIM), F32), pltpu.VMEM((WINDOW + tq, KV_DIM), F32)],
        compiler_params=pltpu.CompilerParams(
            dimension_semantics=("arbitrary", "arbitrary"), vmem_limit_bytes=VMEM_LIMIT),
        name="attn_prompt",
    )(x, g, wq, *tabs, k, v, sinks)


def _attn_sample_kernel(qbd_ref, kc_ref, vc_ref, kn_ref, vn_ref, sink_ref, o_ref, *, bb):
    tpos = lax.broadcasted_iota(jnp.int32, (N_HEADS, WINDOW), 1)

    def body(b, carry):
        qb = qbd_ref[b]
        kb = kc_ref[b].astype(BF16)
        vb = vc_ref[b].astype(BF16)
        s = lax.dot_general(qb.astype(BF16), kb, (((1,), (1,)), ((), ())), preferred_element_type=F32)
        s = jnp.where(tpos >= 1, s, -jnp.inf)
        s_new = jnp.sum(qb * kn_ref[pl.ds(b, 1), :], axis=-1, keepdims=True)
        sink = sink_ref[...]
        mx = jnp.maximum(jnp.maximum(jnp.max(s, axis=-1, keepdims=True), s_new), sink)
        p = jnp.exp(s - mx)
        p_new = jnp.exp(s_new - mx)
        den = jnp.sum(p, axis=-1, keepdims=True) + p_new + jnp.exp(sink - mx)
        o = jnp.dot(p.astype(BF16), vb, preferred_element_type=F32)
        o = o + p_new * vn_ref[pl.ds(b, 1), :]
        o_ref[b] = o / den
        return carry

    lax.fori_loop(0, bb, body, 0)


def _attn_sample(qbd, kc, vc, kn, vn, sinks, *, bb):
    b = qbd.shape[0]
    return pl.pallas_call(
        functools.partial(_attn_sample_kernel, bb=bb),
        grid=(b // bb,),
        in_specs=[pl.BlockSpec((bb, N_HEADS, KV_DIM), lambda i: (i, 0, 0)),
                  pl.BlockSpec((bb, WINDOW, KV_DIM), lambda i: (i, 0, 0)),
                  pl.BlockSpec((bb, WINDOW, KV_DIM), lambda i: (i, 0, 0)),
                  pl.BlockSpec((bb, KV_DIM), lambda i: (i, 0)),
                  pl.BlockSpec((bb, KV_DIM), lambda i: (i, 0)),
                  _const_spec(sinks.shape)],
        out_specs=pl.BlockSpec((bb, N_HEADS, KV_DIM), lambda i: (i, 0, 0)),
        out_shape=jax.ShapeDtypeStruct((b, N_HEADS, KV_DIM), F32),
        compiler_params=pltpu.CompilerParams(
            dimension_semantics=("arbitrary",), vmem_limit_bytes=VMEM_LIMIT),
        name="attn_sample",
    )(qbd, kc, vc, kn, vn, sinks)


def _prep_ffn(wg, wu, wd):
    wg3 = wg.reshape(D_MODEL, N_FF_CHUNKS, FF_CHUNK)
    wu3 = wu.reshape(D_MODEL, N_FF_CHUNKS, FF_CHUNK)
    wgu = jnp.concatenate([wg3, wu3], axis=2).transpose(1, 0, 2).astype(BF16)
    wd3 = wd.reshape(D_FF, N_OUT_CHUNKS, OUT_CHUNK).transpose(1, 0, 2).astype(BF16)
    return wgu, wd3


def kernel(x_prompt, x_sample, state_pool, cache_k_win, cache_v_win, norm_g, ffn_w_gate, ffn_w_up,
           ffn_w_down, pool_w, pool_scale, kv_norm_g, w_kv, w_q, w_o, attn_sinks, final_norm_g):
    bp, seq, _ = x_prompt.shape
    bs = x_sample.shape[0]
    ffn_w = [[_prep_ffn(ffn_w_gate[l, i], ffn_w_up[l, i], ffn_w_down[l, i]) for i in range(2)]
             for l in range(2)]
    g = lambda l, i: norm_g[l, i].reshape(1, D_MODEL)
    pw = pool_w[0].astype(BF16)
    psc = pool_scale[0].reshape(1, D_MODEL)
    kvg = kv_norm_g.reshape(1, D_MODEL)
    wkv = w_kv.astype(BF16)
    wq = w_q[0].reshape(D_MODEL, N_KV, GROUP, HEAD_DIM).transpose(0, 2, 1, 3).reshape(D_MODEL, D_MODEL).astype(BF16)
    wo = w_o[0].reshape(N_KV, GROUP, HEAD_DIM, D_MODEL).transpose(1, 0, 2, 3).reshape(D_MODEL, D_MODEL).astype(BF16)
    sinks_gk = attn_sinks[0].reshape(N_KV, GROUP).T
    gfin = final_norm_g.reshape(1, D_MODEL)
    tabs_p = _rope_tables(jnp.arange(seq))
    tabs_s = _rope_tables(jnp.full((1,), PAST_LEN))

    tm = 512
    x = x_prompt.reshape(bp * seq, D_MODEL)
    x = _ffn(x, g(0, 0), *ffn_w[0][0], tm=tm)
    x, pool_p = _pool_prompt(x.reshape(bp, seq, D_MODEL), g(0, 1), pw, psc, tp=512)
    x = x.reshape(bp * seq, D_MODEL)
    x = _ffn(x, g(0, 2), *ffn_w[0][1], tm=tm)
    kv = _proj_rope(x, kvg, wkv, tabs_p, tm=tm, rope_cols=KV_DIM)
    k_p, v_p = kv[:, :KV_DIM], kv[:, KV_DIM:]
    x = _ffn(x, g(1, 0), *ffn_w[1][0], tm=tm)
    attn = _attn_prompt(x, g(1, 1), wq, tabs_p, k_p, v_p, sinks_gk, batch=bp, tq=512)
    y_p = _ffn(x, g(1, 2), *ffn_w[1][1], tm=tm, attn=attn, wo=wo, final_g=gfin)
    y_prompt = y_p.reshape(bp, seq, D_MODEL)
    pool_prompt = pool_p[None]
    k_win_prompt = k_p.reshape(bp, seq, N_KV, HEAD_DIM)[:, -WINDOW:]
    v_win_prompt = v_p.reshape(bp, seq, N_KV, HEAD_DIM)[:, -WINDOW:]

    ts = bs
    xs = x_sample.reshape(bs, D_MODEL)
    xs = _ffn(xs, g(0, 0), *ffn_w[0][0], tm=ts)
    buf_t = jnp.transpose(state_pool[0], (1, 0, 2))
    xs, h_s = _pool_sample(xs, buf_t, g(0, 1), pw, psc)
    pool_sample = jnp.concatenate([state_pool[:, :, 1:], h_s[None, :, None, :]], axis=2)
    xs = _ffn(xs, g(0, 2), *ffn_w[0][1], tm=ts)
    kv_s = _proj_rope(xs, kvg, wkv, tabs_s, tm=ts, rope_cols=KV_DIM)
    k_s, v_s = kv_s[:, :KV_DIM], kv_s[:, KV_DIM:]
    xs = _ffn(xs, g(1, 0), *ffn_w[1][0], tm=ts)
    q_s = _proj_rope(xs, g(1, 1), wq, tabs_s, tm=ts, rope_cols=D_MODEL, scale=HEAD_DIM ** -0.5)
    seg = (jnp.arange(KV_DIM)[None, :] // HEAD_DIM == jnp.arange(N_KV)[:, None]).astype(F32)
    q4 = q_s.reshape(bs, GROUP, KV_DIM)
    qbd = (q4[:, None, :, :] * seg[None, :, None, :]).reshape(bs, N_HEADS, KV_DIM)
    sinks_col = attn_sinks[0].reshape(N_HEADS, 1)
    kc = cache_k_win.reshape(bs, WINDOW, KV_DIM)
    vc = cache_v_win.reshape(bs, WINDOW, KV_DIM)
    obd = _attn_sample(qbd, kc, vc, k_s, v_s, sinks_col, bb=16)
    o5 = obd.reshape(bs, N_KV, GROUP, N_KV, HEAD_DIM)
    attn_s = jnp.stack([o5[:, kv, :, kv, :] for kv in range(N_KV)], axis=2).reshape(bs, D_MODEL).astype(BF16)
    y_s = _ffn(xs, g(1, 2), *ffn_w[1][1], tm=ts, attn=attn_s, wo=wo, final_g=gfin)
    y_sample = y_s.reshape(bs, 1, D_MODEL)
    k_win_sample = jnp.concatenate([cache_k_win[:, 1:], k_s.reshape(bs, 1, N_KV, HEAD_DIM)], axis=1)
    v_win_sample = jnp.concatenate([cache_v_win[:, 1:], v_s.reshape(bs, 1, N_KV, HEAD_DIM)], axis=1)

    return (y_prompt, y_sample, pool_prompt, pool_sample, k_win_prompt, v_win_prompt,
            k_win_sample, v_win_sample)
```

```python
import functools

import numpy as np
import jax
import jax.numpy as jnp
from jax import lax
from jax.experimental import pallas as pl
from jax.experimental.pallas import tpu as pltpu

F32 = jnp.float32
BF16 = jnp.bfloat16

D_MODEL = 1024
D_FF = 2816
HEAD_DIM = 64
N_HEADS = 16
N_KV = 4
GROUP = 4
KV_DIM = N_KV * HEAD_DIM
WINDOW = 128
ROT_DIM = 16
ROPE_THETA = 500000.0
EPS = 1e-5
POOL_WINDOWS = (2, 4, 8, 16)
POOL_GROUP = 256
POOL_BUF = 15
PAST_LEN = 16384

LANES = 128
FF_CHUNK = 256
N_FF_CHUNKS = D_FF // FF_CHUNK
OUT_CHUNK = 256
N_OUT_CHUNKS = D_MODEL // OUT_CHUNK
VMEM_LIMIT = 56 * 1024 * 1024

GU_SLABS = 8
GU_ROWS = D_MODEL // GU_SLABS
DN_SLABS = 8
DN_ROWS = D_FF // DN_SLABS


def _const_spec(shape):
    nd = len(shape)
    return pl.BlockSpec(shape, lambda *_: (0,) * nd, pipeline_mode=pl.Buffered(1))


def _rms(x, g):
    ms = jnp.mean(x * x, axis=-1, keepdims=True)
    return x * lax.rsqrt(ms + EPS) * g


def _rope(x, c, s1, s2):
    pieces = []
    for j in range(x.shape[1] // LANES):
        xc = x[:, j * LANES:(j + 1) * LANES]
        up = pltpu.roll(xc, LANES - ROT_DIM // 2, axis=1)
        dn = pltpu.roll(xc, ROT_DIM // 2, axis=1)
        pieces.append(xc * c + up * s1 + dn * s2)
    return jnp.concatenate(pieces, axis=1) if len(pieces) > 1 else pieces[0]


def _rope_tables(pos):
    half = ROT_DIM // 2
    inv = np.power(ROPE_THETA, -np.arange(half, dtype=np.float64) * (2.0 / ROT_DIM))
    ang = np.asarray(pos, np.float64)[:, None] * inv[None, :]
    cos, sin = np.cos(ang), np.sin(ang)
    n = ang.shape[0]
    ones = np.ones((n, HEAD_DIM - ROT_DIM))
    zeros8 = np.zeros((n, half))
    zeros = np.zeros((n, HEAD_DIM - ROT_DIM))
    c = np.concatenate([cos, cos, ones], axis=1)
    s1 = np.concatenate([-sin, zeros8, zeros], axis=1)
    s2 = np.concatenate([zeros8, sin, zeros], axis=1)
    return [jnp.asarray(np.concatenate([t, t], axis=1), F32) for t in (c, s1, s2)]


def _stage_ffn_weights(wg_hbm, wu_hbm, wd_hbm, wgu_scr, wd_scr, gu_stage, dn_stage, gu_sem, dn_sem):
    chunks = []
    for src, col0 in ((wg_hbm, 0), (wu_hbm, FF_CHUNK)):
        for r in range(GU_SLABS):
            chunks.append(("gu", src, r, col0))
    for r in range(DN_SLABS):
        chunks.append(("dn", wd_hbm, r, 0))

    def copy(c):
        kind, src, r, _ = chunks[c]
        slot = c % 2
        if kind == "gu":
            return pltpu.make_async_copy(src.at[pl.ds(r * GU_ROWS, GU_ROWS), :], gu_stage.at[slot], gu_sem.at[slot])
        return pltpu.make_async_copy(src.at[pl.ds(r * DN_ROWS, DN_ROWS), :], dn_stage.at[slot], dn_sem.at[slot])

    copy(0).start()
    copy(1).start()
    for c, (kind, _, r, col0) in enumerate(chunks):
        slot = c % 2
        copy(c).wait()
        if kind == "gu":
            for j in range(N_FF_CHUNKS):
                wgu_scr[j, r * GU_ROWS:(r + 1) * GU_ROWS, col0:col0 + FF_CHUNK] = (
                    gu_stage[slot, :, j * FF_CHUNK:(j + 1) * FF_CHUNK].astype(BF16))
        else:
            wd_scr[r * DN_ROWS:(r + 1) * DN_ROWS, :] = dn_stage[slot].astype(BF16)
        if c + 2 < len(chunks):
            copy(c + 2).start()


def _ffn_rows(rows, x_ref, a_ref, g_ref, gf_ref, kvg_ref, tabs, o_ref, k_ref, v_ref,
              wgu_scr, wd_scr, wo_scr, wkv_scr, xn_ref, h_ref, xr_ref):
    rs = slice(0, rows)
    if a_ref is not None:
        xr_ref[rs, :] = x_ref[...] + jnp.dot(a_ref[...], wo_scr[...], preferred_element_type=F32)
        res = lambda sl: xr_ref[rs, sl]
    else:
        res = lambda sl: x_ref[:, sl]
    xn_ref[rs, :] = _rms(res(slice(None)), g_ref[...]).astype(BF16)
    for j in range(N_FF_CHUNKS):
        r = jnp.dot(xn_ref[rs, :], wgu_scr[j], preferred_element_type=F32)
        gate, up = r[:, :FF_CHUNK], r[:, FF_CHUNK:]
        h_ref[rs, j * FF_CHUNK:(j + 1) * FF_CHUNK] = (gate * jax.nn.sigmoid(gate) * up).astype(BF16)
    for n in range(N_OUT_CHUNKS):
        sl = slice(n * OUT_CHUNK, (n + 1) * OUT_CHUNK)
        y = jnp.dot(h_ref[rs, :], wd_scr[:, sl], preferred_element_type=F32)
        o_ref[:, sl] = res(sl) + 0.5 * y
    if kvg_ref is not None:
        kvn = _rms(o_ref[...], kvg_ref[...]).astype(BF16)
        kv = jnp.dot(kvn, wkv_scr[...], preferred_element_type=F32)
        tb = [t[...] for t in tabs]
        if tb[0].shape[0] != rows:
            tb = [jnp.broadcast_to(t, (rows, LANES)) for t in tb]
        k_ref[...] = _rope(kv[:, :KV_DIM], *tb)
        v_ref[...] = kv[:, KV_DIM:]
    if gf_ref is not None:
        o_ref[...] = _rms(o_ref[...], gf_ref[...])


def _ffn_kernel(*refs, nt, tm, ms, has_attn, has_kv, has_final):
    it = iter(refs)
    x_ref, xs_ref = next(it), next(it)
    a_ref = as_ref = wo_ref = gf_ref = kvg_ref = wkv_ref = None
    tabs_p = tabs_s = None
    if has_attn:
        a_ref, as_ref, wo_ref = next(it), next(it), next(it)
    g_ref, wg_hbm, wu_hbm, wd_hbm = next(it), next(it), next(it), next(it)
    if has_kv:
        kvg_ref, wkv_ref = next(it), next(it)
        tabs_p = [next(it) for _ in range(3)]
        tabs_s = [next(it) for _ in range(3)]
    if has_final:
        gf_ref = next(it)
    o_ref, os_ref = next(it), next(it)
    k_ref = v_ref = ks_ref = vs_ref = None
    if has_kv:
        k_ref, v_ref, ks_ref, vs_ref = next(it), next(it), next(it), next(it)
    wgu_scr, wd_scr, gu_stage, dn_stage, gu_sem, dn_sem, xn_ref, h_ref = [next(it) for _ in range(8)]
    wo_scr = xr_ref = wkv_scr = None
    if has_attn:
        wo_scr, xr_ref = next(it), next(it)
    if has_kv:
        wkv_scr = next(it)

    i = pl.program_id(0)

    @pl.when(i == 0)
    def _():
        _stage_ffn_weights(wg_hbm, wu_hbm, wd_hbm, wgu_scr, wd_scr, gu_stage, dn_stage, gu_sem, dn_sem)
        if has_attn:
            wo_scr[...] = wo_ref[...].astype(BF16)
        if has_kv:
            wkv_scr[...] = wkv_ref[...].astype(BF16)

    common = (wgu_scr, wd_scr, wo_scr, wkv_scr, xn_ref, h_ref, xr_ref)

    @pl.when((i >= 1) & (i <= nt))
    def _():
        _ffn_rows(tm, x_ref, a_ref, g_ref, gf_ref, kvg_ref, tabs_p, o_ref, k_ref, v_ref, *common)

    @pl.when(i == nt + 1)
    def _():
        _ffn_rows(ms, xs_ref, as_ref, g_ref, gf_ref, kvg_ref, tabs_s, os_ref, ks_ref, vs_ref, *common)


def _ffn(x, xs, g, wg, wu, wd, *, tm, seq, attn=None, attn_s=None, wo=None,
         kv_g=None, wkv=None, tabs_p=None, tabs_s=None, final_g=None):
    mp, ms = x.shape[0], xs.shape[0]
    nt = mp // tm
    has_attn, has_kv, has_final = attn is not None, kv_g is not None, final_g is not None
    tile = lambda i: jnp.clip(i - 1, 0, nt - 1)
    row = lambda w: pl.BlockSpec((tm, w), lambda i: (tile(i), 0))
    hbm = pl.BlockSpec(memory_space=pl.ANY)

    args, specs = [x, xs], [row(D_MODEL), _const_spec(xs.shape)]
    if has_attn:
        args += [attn, attn_s, wo]
        specs += [row(D_MODEL), _const_spec(attn_s.shape), _const_spec(wo.shape)]
    args += [g, wg, wu, wd]
    specs += [_const_spec(g.shape), hbm, hbm, hbm]
    if has_kv:
        per_seq = seq // tm
        args += [kv_g, wkv, *tabs_p, *tabs_s]
        specs += [_const_spec(kv_g.shape), _const_spec(wkv.shape)]
        specs += [pl.BlockSpec((tm, LANES), lambda i: (tile(i) % per_seq, 0))] * 3
        specs += [_const_spec((1, LANES))] * 3
    if has_final:
        args.append(final_g)
        specs.append(_const_spec(final_g.shape))

    out_shape = [jax.ShapeDtypeStruct((mp, D_MODEL), F32), jax.ShapeDtypeStruct((ms, D_MODEL), F32)]
    out_specs = [row(D_MODEL), _const_spec((ms, D_MODEL))]
    if has_kv:
        out_shape += [jax.ShapeDtypeStruct((mp, KV_DIM), F32)] * 2 + [jax.ShapeDtypeStruct((ms, KV_DIM), F32)] * 2
        out_specs += [row(KV_DIM)] * 2 + [_const_spec((ms, KV_DIM))] * 2

    scratch = [pltpu.VMEM((N_FF_CHUNKS, D_MODEL, 2 * FF_CHUNK), BF16),
               pltpu.VMEM((D_FF, D_MODEL), BF16),
               pltpu.VMEM((2, GU_ROWS, D_FF), F32),
               pltpu.VMEM((2, DN_ROWS, D_MODEL), F32),
               pltpu.SemaphoreType.DMA((2,)),
               pltpu.SemaphoreType.DMA((2,)),
               pltpu.VMEM((tm, D_MODEL), BF16),
               pltpu.VMEM((tm, D_FF), BF16)]
    if has_attn:
        scratch += [pltpu.VMEM((D_MODEL, D_MODEL), BF16), pltpu.VMEM((tm, D_MODEL), F32)]
    if has_kv:
        scratch += [pltpu.VMEM((D_MODEL, 2 * KV_DIM), BF16)]

    return pl.pallas_call(
        functools.partial(_ffn_kernel, nt=nt, tm=tm, ms=ms, has_attn=has_attn, has_kv=has_kv,
                          has_final=has_final),
        grid=(nt + 2,),
        in_specs=specs,
        out_specs=out_specs,
        out_shape=out_shape,
        scratch_shapes=scratch,
        compiler_params=pltpu.CompilerParams(
            dimension_semantics=("arbitrary",), vmem_limit_bytes=VMEM_LIMIT),
        name="ffn",
    )(*args)


HALO = 16


def _pool_prompt_kernel(x_ref, g_ref, w_ref, sc_ref, o_ref, buf_ref, hext_ref, *, tp):
    t = pl.program_id(1)

    @pl.when(t == 0)
    def _():
        hext_ref[0:HALO, :] = jnp.zeros((HALO, D_MODEL), F32)

    x = x_ref[...]
    h = _rms(x, g_ref[...])
    hext_ref[HALO:HALO + tp, :] = h
    pos = (t * tp + lax.broadcasted_iota(jnp.int32, (tp, 1), 0)).astype(F32)
    mixed = []
    for gi, w in enumerate(POOL_WINDOWS):
        cs = slice(gi * POOL_GROUP, (gi + 1) * POOL_GROUP)
        s = h[:, cs]
        for k in range(1, w):
            s = s + hext_ref[HALO - k:HALO - k + tp, cs]
        cnt = jnp.minimum(float(w), pos + 1.0)
        diff = (s / cnt - h[:, cs]).astype(BF16)
        mixed.append(jnp.dot(diff, w_ref[gi].astype(BF16), preferred_element_type=F32))
    out = jnp.concatenate(mixed, axis=1) * sc_ref[...]
    o_ref[...] = x + out

    @pl.when(t == pl.num_programs(1) - 1)
    def _():
        buf_ref[0] = hext_ref[HALO + tp - POOL_BUF:HALO + tp, :]

    hext_ref[0:HALO, :] = hext_ref[tp:tp + HALO, :]


def _pool_prompt(x, g, w, sc, *, batch, tp):
    m = x.shape[0]
    nb = m // batch // tp
    blk = pl.BlockSpec((tp, D_MODEL), lambda i, j: (i * nb + j, 0))
    return pl.pallas_call(
        functools.partial(_pool_prompt_kernel, tp=tp),
        grid=(batch, nb),
        in_specs=[blk, _const_spec(g.shape), _const_spec(w.shape), _const_spec(sc.shape)],
        out_specs=[blk, pl.BlockSpec((1, POOL_BUF, D_MODEL), lambda i, j: (i, 0, 0))],
        out_shape=[jax.ShapeDtypeStruct(x.shape, F32),
                   jax.ShapeDtypeStruct((batch, POOL_BUF, D_MODEL), F32)],
        scratch_shapes=[pltpu.VMEM((HALO + tp, D_MODEL), F32)],
        compiler_params=pltpu.CompilerParams(
            dimension_semantics=("arbitrary", "arbitrary"), vmem_limit_bytes=VMEM_LIMIT),
        name="pool_prompt",
    )(x, g, w, sc)


def _pool_sample_kernel(x_ref, buf_ref, g_ref, w_ref, sc_ref, o_ref, h_ref):
    x = x_ref[...]
    h = _rms(x, g_ref[...])
    h_ref[...] = h
    mixed = []
    for gi, w in enumerate(POOL_WINDOWS):
        cs = slice(gi * POOL_GROUP, (gi + 1) * POOL_GROUP)
        s = h[:, cs]
        for k in range(1, w):
            s = s + buf_ref[POOL_BUF - k, :, cs]
        diff = (s / float(w) - h[:, cs]).astype(BF16)
        mixed.append(jnp.dot(diff, w_ref[gi].astype(BF16), preferred_element_type=F32))
    o_ref[...] = x + jnp.concatenate(mixed, axis=1) * sc_ref[...]


def _pool_sample(x, buf_t, g, w, sc):
    m = x.shape[0]
    return pl.pallas_call(
        _pool_sample_kernel,
        out_shape=[jax.ShapeDtypeStruct((m, D_MODEL), F32), jax.ShapeDtypeStruct((m, D_MODEL), F32)],
        compiler_params=pltpu.CompilerParams(vmem_limit_bytes=VMEM_LIMIT),
        name="pool_sample",
    )(x, buf_t, g, w, sc)


def _q_sample_kernel(x_ref, g_ref, w_ref, c_ref, s1_ref, s2_ref, o_ref):
    n = _rms(x_ref[...], g_ref[...]).astype(BF16)
    y = jnp.dot(n, w_ref[...].astype(BF16), preferred_element_type=F32)
    tabs = [jnp.broadcast_to(r[...], (y.shape[0], LANES)) for r in (c_ref, s1_ref, s2_ref)]
    o_ref[...] = _rope(y, *tabs) * (HEAD_DIM ** -0.5)


def _q_sample(x, g, w, tabs):
    return pl.pallas_call(
        _q_sample_kernel,
        out_shape=jax.ShapeDtypeStruct(x.shape, F32),
        compiler_params=pltpu.CompilerParams(vmem_limit_bytes=VMEM_LIMIT),
        name="q_sample",
    )(x, g, w, *tabs)


def _seg_mask(rows):
    lane = lax.broadcasted_iota(jnp.int32, (rows, KV_DIM), 1)
    return [(lane >= kv * HEAD_DIM) & (lane < (kv + 1) * HEAD_DIM) for kv in range(N_KV)]


def _attn_prompt_kernel(x_ref, g_ref, wq_ref, c_ref, s1_ref, s2_ref, k_ref, v_ref, sink_ref,
                        o_ref, kbuf, vbuf, *, tq):
    n = pl.program_id(1)

    @pl.when(n == 0)
    def _():
        kbuf[0:WINDOW, :] = jnp.zeros((WINDOW, KV_DIM), F32)
        vbuf[0:WINDOW, :] = jnp.zeros((WINDOW, KV_DIM), F32)

    kbuf[WINDOW:WINDOW + tq, :] = k_ref[...]
    vbuf[WINDOW:WINDOW + tq, :] = v_ref[...]

    hn = _rms(x_ref[...], g_ref[...]).astype(BF16)
    q = jnp.dot(hn, wq_ref[...], preferred_element_type=F32)
    q = (_rope(q, c_ref[...], s1_ref[...], s2_ref[...]) * (HEAD_DIM ** -0.5)).astype(BF16)

    seg2 = _seg_mask(2 * WINDOW)
    seg1 = _seg_mask(WINDOW)
    qi = lax.broadcasted_iota(jnp.int32, (WINDOW, 2 * WINDOW), 0)
    kj = lax.broadcasted_iota(jnp.int32, (WINDOW, 2 * WINDOW), 1)
    in_window = ((kj < WINDOW) & (kj > qi)) | ((kj >= WINDOW) & (kj - WINDOW <= qi))
    first_key = jnp.where(n > 0, 0, WINDOW)

    for i in range(tq // WINDOW):
        kk = kbuf[i * WINDOW:(i + 2) * WINDOW, :]
        vv = vbuf[i * WINDOW:(i + 2) * WINDOW, :]
        kbd = jnp.concatenate([jnp.where(m, kk, 0.0).astype(BF16) for m in seg2], axis=0)
        vbd = jnp.concatenate([jnp.where(m, vv, 0.0).astype(BF16) for m in seg2], axis=0)
        mask = (in_window & (kj >= first_key)) if i == 0 else in_window
        for gm in range(GROUP):
            qg = q[i * WINDOW:(i + 1) * WINDOW, gm * KV_DIM:(gm + 1) * KV_DIM]
            s = lax.dot_general(qg, kbd, (((1,), (1,)), ((), ())), preferred_element_type=F32)
            ps, rinv = [], jnp.zeros((WINDOW, KV_DIM), F32)
            for kv in range(N_KV):
                skv = jnp.where(mask, s[:, kv * 2 * WINDOW:(kv + 1) * 2 * WINDOW], -jnp.inf)
                sink = sink_ref[gm, kv]
                mx = jnp.maximum(jnp.max(skv, axis=-1, keepdims=True), sink)
                p = jnp.exp(skv - mx)
                den = jnp.sum(p, axis=-1, keepdims=True) + jnp.exp(sink - mx)
                ps.append(p.astype(BF16))
                rinv = jnp.where(seg1[kv], 1.0 / den, rinv)
            pcat = jnp.concatenate(ps, axis=1)
            og = jnp.dot(pcat, vbd, preferred_element_type=F32) * rinv
            o_ref[i * WINDOW:(i + 1) * WINDOW, gm * KV_DIM:(gm + 1) * KV_DIM] = og.astype(o_ref.dtype)

    kbuf[0:WINDOW, :] = kbuf[tq:tq + WINDOW, :]
    vbuf[0:WINDOW, :] = vbuf[tq:tq + WINDOW, :]


def _attn_prompt(x, g, wq, tabs, k, v, sinks, *, batch, tq):
    m = x.shape[0]
    nb = m // batch // tq
    row = lambda w: pl.BlockSpec((tq, w), lambda b, j: (b * nb + j, 0))
    return pl.pallas_call(
        functools.partial(_attn_prompt_kernel, tq=tq),
        grid=(batch, nb),
        in_specs=[row(D_MODEL), _const_spec(g.shape), _const_spec(wq.shape),
                  pl.BlockSpec((tq, LANES), lambda b, j: (j, 0)),
                  pl.BlockSpec((tq, LANES), lambda b, j: (j, 0)),
                  pl.BlockSpec((tq, LANES), lambda b, j: (j, 0)),
                  row(KV_DIM), row(KV_DIM), pl.BlockSpec(memory_space=pltpu.SMEM)],
        out_specs=row(D_MODEL),
        out_shape=jax.ShapeDtypeStruct((m, D_MODEL), BF16),
        scratch_shapes=[pltpu.VMEM((WINDOW + tq, KV_DIM), F32), pltpu.VMEM((WINDOW + tq, KV_DIM), F32)],
        compiler_params=pltpu.CompilerParams(
            dimension_semantics=("arbitrary", "arbitrary"), vmem_limit_bytes=VMEM_LIMIT),
        name="attn_prompt",
    )(x, g, wq, *tabs, k, v, sinks)


def _attn_sample_kernel(qbd_ref, kc_ref, vc_ref, kn_ref, vn_ref, sink_ref, o_ref, *, bb):
    tpos = lax.broadcasted_iota(jnp.int32, (N_HEADS, WINDOW), 1)

    def body(b, carry):
        qb = qbd_ref[b]
        kb = kc_ref[b].astype(BF16)
        vb = vc_ref[b].astype(BF16)
        s = lax.dot_general(qb.astype(BF16), kb, (((1,), (1,)), ((), ())), preferred_element_type=F32)
        s = jnp.where(tpos >= 1, s, -jnp.inf)
        s_new = jnp.sum(qb * kn_ref[pl.ds(b, 1), :], axis=-1, keepdims=True)
        sink = sink_ref[...]
        mx = jnp.maximum(jnp.maximum(jnp.max(s, axis=-1, keepdims=True), s_new), sink)
        p = jnp.exp(s - mx)
        p_new = jnp.exp(s_new - mx)
        den = jnp.sum(p, axis=-1, keepdims=True) + p_new + jnp.exp(sink - mx)
        o = jnp.dot(p.astype(BF16), vb, preferred_element_type=F32)
        o = o + p_new * vn_ref[pl.ds(b, 1), :]
        o_ref[b] = o / den
        return carry

    lax.fori_loop(0, bb, body, 0)


def _attn_sample(qbd, kc, vc, kn, vn, sinks, *, bb):
    b = qbd.shape[0]
    return pl.pallas_call(
        functools.partial(_attn_sample_kernel, bb=bb),
        grid=(b // bb,),
        in_specs=[pl.BlockSpec((bb, N_HEADS, KV_DIM), lambda i: (i, 0, 0)),
                  pl.BlockSpec((bb, WINDOW, KV_DIM), lambda i: (i, 0, 0)),
                  pl.BlockSpec((bb, WINDOW, KV_DIM), lambda i: (i, 0, 0)),
                  pl.BlockSpec((bb, KV_DIM), lambda i: (i, 0)),
                  pl.BlockSpec((bb, KV_DIM), lambda i: (i, 0)),
                  _const_spec(sinks.shape)],
        out_specs=pl.BlockSpec((bb, N_HEADS, KV_DIM), lambda i: (i, 0, 0)),
        out_shape=jax.ShapeDtypeStruct((b, N_HEADS, KV_DIM), F32),
        compiler_params=pltpu.CompilerParams(
            dimension_semantics=("arbitrary",), vmem_limit_bytes=VMEM_LIMIT),
        name="attn_sample",
    )(qbd, kc, vc, kn, vn, sinks)


def kernel(x_prompt, x_sample, state_pool, cache_k_win, cache_v_win, norm_g, ffn_w_gate, ffn_w_up,
           ffn_w_down, pool_w, pool_scale, kv_norm_g, w_kv, w_q, w_o, attn_sinks, final_norm_g):
    bp, seq, _ = x_prompt.shape
    bs = x_sample.shape[0]
    tm = 512
    g = lambda l, i: norm_g[l, i].reshape(1, D_MODEL)
    ffn_w = lambda l, i: (ffn_w_gate[l, i], ffn_w_up[l, i], ffn_w_down[l, i])
    psc = pool_scale[0].reshape(1, D_MODEL)
    kvg = kv_norm_g.reshape(1, D_MODEL)
    wq = w_q[0].reshape(D_MODEL, N_KV, GROUP, HEAD_DIM).transpose(0, 2, 1, 3).reshape(D_MODEL, D_MODEL)
    wo = w_o[0].reshape(N_KV, GROUP, HEAD_DIM, D_MODEL).transpose(1, 0, 2, 3).reshape(D_MODEL, D_MODEL)
    sinks_gk = attn_sinks[0].reshape(N_KV, GROUP).T
    gfin = final_norm_g.reshape(1, D_MODEL)
    tabs_p = _rope_tables(np.arange(seq))
    tabs_s = _rope_tables(np.full((1,), PAST_LEN))

    x = x_prompt.reshape(bp * seq, D_MODEL)
    xs = x_sample.reshape(bs, D_MODEL)

    x, xs = _ffn(x, xs, g(0, 0), *ffn_w(0, 0), tm=tm, seq=seq)
    x, pool_p = _pool_prompt(x, g(0, 1), pool_w[0], psc, batch=bp, tp=512)
    buf_t = jnp.transpose(state_pool[0], (1, 0, 2))
    xs, h_s = _pool_sample(xs, buf_t, g(0, 1), pool_w[0], psc)
    x, xs, k_p, v_p, k_s, v_s = _ffn(x, xs, g(0, 2), *ffn_w(0, 1), tm=tm, seq=seq,
                                     kv_g=kvg, wkv=w_kv, tabs_p=tabs_p, tabs_s=tabs_s)
    x, xs = _ffn(x, xs, g(1, 0), *ffn_w(1, 0), tm=tm, seq=seq)
    attn = _attn_prompt(x, g(1, 1), wq.astype(BF16), tabs_p, k_p, v_p, sinks_gk, batch=bp, tq=512)
    q_s = _q_sample(xs, g(1, 1), wq, tabs_s)
    seg = (jnp.arange(KV_DIM)[None, :] // HEAD_DIM == jnp.arange(N_KV)[:, None]).astype(F32)
    q4 = q_s.reshape(bs, GROUP, KV_DIM)
    qbd = (q4[:, None, :, :] * seg[None, :, None, :]).reshape(bs, N_HEADS, KV_DIM)
    sinks_col = attn_sinks[0].reshape(N_HEADS, 1)
    kc = cache_k_win.reshape(bs, WINDOW, KV_DIM)
    vc = cache_v_win.reshape(bs, WINDOW, KV_DIM)
    obd = _attn_sample(qbd, kc, vc, k_s, v_s, sinks_col, bb=16)
    o5 = obd.reshape(bs, N_KV, GROUP, N_KV, HEAD_DIM)
    attn_s = jnp.stack([o5[:, kv, :, kv, :] for kv in range(N_KV)], axis=2).reshape(bs, D_MODEL).astype(BF16)
    y_p, y_s = _ffn(x, xs, g(1, 2), *ffn_w(1, 1), tm=tm, seq=seq, attn=attn, attn_s=attn_s, wo=wo,
                    final_g=gfin)

    y_prompt = y_p.reshape(bp, seq, D_MODEL)
    y_sample = y_s.reshape(bs, 1, D_MODEL)
    pool_prompt = pool_p[None]
    pool_sample = jnp.concatenate([state_pool[:, :, 1:], h_s[None, :, None, :]], axis=2)
    k_win_prompt = k_p.reshape(bp, seq, N_KV, HEAD_DIM)[:, -WINDOW:]
    v_win_prompt = v_p.reshape(bp, seq, N_KV, HEAD_DIM)[:, -WINDOW:]
    k_win_sample = jnp.concatenate([cache_k_win[:, 1:], k_s.reshape(bs, 1, N_KV, HEAD_DIM)], axis=1)
    v_win_sample = jnp.concatenate([cache_v_win[:, 1:], v_s.reshape(bs, 1, N_KV, HEAD_DIM)], axis=1)
    return (y_prompt, y_sample, pool_prompt, pool_sample, k_win_prompt, v_win_prompt,
            k_win_sample, v_win_sample)
```

```python
import functools

import numpy as np
import jax
import jax.numpy as jnp
from jax import lax
from jax.experimental import pallas as pl
from jax.experimental.pallas import tpu as pltpu

F32 = jnp.float32
BF16 = jnp.bfloat16

D_MODEL = 1024
D_FF = 2816
HEAD_DIM = 64
N_HEADS = 16
N_KV = 4
GROUP = 4
KV_DIM = N_KV * HEAD_DIM
WINDOW = 128
ROT_DIM = 16
ROPE_THETA = 500000.0
EPS = 1e-5
POOL_WINDOWS = (2, 4, 8, 16)
POOL_GROUP = 256
POOL_BUF = 15
PAST_LEN = 16384

LANES = 128
FF_CHUNK = 256
N_FF_CHUNKS = D_FF // FF_CHUNK
OUT_CHUNK = 256
N_OUT_CHUNKS = D_MODEL // OUT_CHUNK
VMEM_LIMIT = 56 * 1024 * 1024

GU_SLABS = 8
GU_ROWS = D_MODEL // GU_SLABS
DN_SLABS = 8
DN_ROWS = D_FF // DN_SLABS


def _const_spec(shape):
    nd = len(shape)
    return pl.BlockSpec(shape, lambda *_: (0,) * nd, pipeline_mode=pl.Buffered(1))


def _rms(x, g):
    ms = jnp.mean(x * x, axis=-1, keepdims=True)
    return x * lax.rsqrt(ms + EPS) * g


def _rope(x, c, s1, s2):
    pieces = []
    for j in range(x.shape[1] // LANES):
        xc = x[:, j * LANES:(j + 1) * LANES]
        up = pltpu.roll(xc, LANES - ROT_DIM // 2, axis=1)
        dn = pltpu.roll(xc, ROT_DIM // 2, axis=1)
        pieces.append(xc * c + up * s1 + dn * s2)
    return jnp.concatenate(pieces, axis=1) if len(pieces) > 1 else pieces[0]


def _rope_tables(pos):
    half = ROT_DIM // 2
    inv = np.power(ROPE_THETA, -np.arange(half, dtype=np.float64) * (2.0 / ROT_DIM))
    ang = np.asarray(pos, np.float64)[:, None] * inv[None, :]
    cos, sin = np.cos(ang), np.sin(ang)
    n = ang.shape[0]
    ones = np.ones((n, HEAD_DIM - ROT_DIM))
    zeros8 = np.zeros((n, half))
    zeros = np.zeros((n, HEAD_DIM - ROT_DIM))
    c = np.concatenate([cos, cos, ones], axis=1)
    s1 = np.concatenate([-sin, zeros8, zeros], axis=1)
    s2 = np.concatenate([zeros8, sin, zeros], axis=1)
    return [jnp.asarray(np.concatenate([t, t], axis=1), F32) for t in (c, s1, s2)]


def _stage_ffn_weights(wg_hbm, wu_hbm, wd_hbm, wgu_scr, wd_scr, gu_stage, dn_stage, gu_sem, dn_sem):
    chunks = []
    for src, col0 in ((wg_hbm, 0), (wu_hbm, FF_CHUNK)):
        for r in range(GU_SLABS):
            chunks.append(("gu", src, r, col0))
    for r in range(DN_SLABS):
        chunks.append(("dn", wd_hbm, r, 0))

    def copy(c):
        kind, src, r, _ = chunks[c]
        slot = c % 2
        if kind == "gu":
            return pltpu.make_async_copy(src.at[pl.ds(r * GU_ROWS, GU_ROWS), :], gu_stage.at[slot], gu_sem.at[slot])
        return pltpu.make_async_copy(src.at[pl.ds(r * DN_ROWS, DN_ROWS), :], dn_stage.at[slot], dn_sem.at[slot])

    copy(0).start()
    copy(1).start()
    for c, (kind, _, r, col0) in enumerate(chunks):
        slot = c % 2
        copy(c).wait()
        if kind == "gu":
            for j in range(N_FF_CHUNKS):
                wgu_scr[j, r * GU_ROWS:(r + 1) * GU_ROWS, col0:col0 + FF_CHUNK] = (
                    gu_stage[slot, :, j * FF_CHUNK:(j + 1) * FF_CHUNK].astype(BF16))
        else:
            wd_scr[r * DN_ROWS:(r + 1) * DN_ROWS, :] = dn_stage[slot].astype(BF16)
        if c + 2 < len(chunks):
            copy(c + 2).start()


def _ffn_rows(rows, x_ref, a_ref, g_ref, gf_ref, kvg_ref, tabs, o_ref, k_ref, v_ref,
              wgu_scr, wd_scr, wo_scr, wkv_scr, xn_ref, h_ref, xr_ref):
    rs = slice(0, rows)
    if a_ref is not None:
        xr_ref[rs, :] = x_ref[...] + jnp.dot(a_ref[...], wo_scr[...], preferred_element_type=F32)
        res = lambda sl: xr_ref[rs, sl]
    else:
        res = lambda sl: x_ref[:, sl]
    xn_ref[rs, :] = _rms(res(slice(None)), g_ref[...]).astype(BF16)
    for j in range(N_FF_CHUNKS):
        r = jnp.dot(xn_ref[rs, :], wgu_scr[j], preferred_element_type=F32)
        gate, up = r[:, :FF_CHUNK], r[:, FF_CHUNK:]
        h_ref[rs, j * FF_CHUNK:(j + 1) * FF_CHUNK] = (gate * jax.nn.sigmoid(gate) * up).astype(BF16)
    for n in range(N_OUT_CHUNKS):
        sl = slice(n * OUT_CHUNK, (n + 1) * OUT_CHUNK)
        y = jnp.dot(h_ref[rs, :], wd_scr[:, sl], preferred_element_type=F32)
        o_ref[:, sl] = res(sl) + 0.5 * y
    if kvg_ref is not None:
        kvn = _rms(o_ref[...], kvg_ref[...]).astype(BF16)
        kv = jnp.dot(kvn, wkv_scr[...], preferred_element_type=F32)
        tb = [t[...] for t in tabs]
        if tb[0].shape[0] != rows:
            tb = [jnp.broadcast_to(t, (rows, LANES)) for t in tb]
        k_ref[...] = _rope(kv[:, :KV_DIM], *tb)
        v_ref[...] = kv[:, KV_DIM:]
    if gf_ref is not None:
        o_ref[...] = _rms(o_ref[...], gf_ref[...])


def _ffn_kernel(*refs, nt, tm, ms, layer, has_attn, has_kv, has_final):
    it = iter(refs)
    x_ref, xs_ref = next(it), next(it)
    a_ref = as_ref = wo_ref = gf_ref = kvg_ref = wkv_ref = None
    tabs_p = tabs_s = None
    if has_attn:
        a_ref, as_ref, wo_ref = next(it), next(it), next(it)
    g_ref, wg_hbm, wu_hbm, wd_hbm = next(it), next(it), next(it), next(it)
    if has_kv:
        kvg_ref, wkv_ref = next(it), next(it)
        tabs_p = [next(it) for _ in range(3)]
        tabs_s = [next(it) for _ in range(3)]
    if has_final:
        gf_ref = next(it)
    o_ref, os_ref = next(it), next(it)
    k_ref = v_ref = ks_ref = vs_ref = None
    if has_kv:
        k_ref, v_ref, ks_ref, vs_ref = next(it), next(it), next(it), next(it)
    wgu_scr, wd_scr, gu_stage, dn_stage, gu_sem, dn_sem, xn_ref, h_ref = [next(it) for _ in range(8)]
    wo_scr = xr_ref = wkv_scr = None
    if has_attn:
        wo_scr, xr_ref = next(it), next(it)
    if has_kv:
        wkv_scr = next(it)

    i = pl.program_id(0)

    @pl.when(i == 0)
    def _():
        _stage_ffn_weights(wg_hbm.at[layer], wu_hbm.at[layer], wd_hbm.at[layer], wgu_scr, wd_scr,
                           gu_stage, dn_stage, gu_sem, dn_sem)
        if has_attn:
            wo_scr[...] = wo_ref[...].astype(BF16)
        if has_kv:
            wkv_scr[...] = wkv_ref[...].astype(BF16)

    common = (wgu_scr, wd_scr, wo_scr, wkv_scr, xn_ref, h_ref, xr_ref)

    @pl.when((i >= 1) & (i <= nt))
    def _():
        _ffn_rows(tm, x_ref, a_ref, g_ref, gf_ref, kvg_ref, tabs_p, o_ref, k_ref, v_ref, *common)

    @pl.when(i == nt + 1)
    def _():
        _ffn_rows(ms, xs_ref, as_ref, g_ref, gf_ref, kvg_ref, tabs_s, os_ref, ks_ref, vs_ref, *common)


def _ffn(x, xs, g, wg, wu, wd, *, layer, tm, seq, attn=None, attn_s=None, wo=None,
         kv_g=None, wkv=None, tabs_p=None, tabs_s=None, final_g=None):
    mp, ms = x.shape[0], xs.shape[0]
    nt = mp // tm
    has_attn, has_kv, has_final = attn is not None, kv_g is not None, final_g is not None
    tile = lambda i: jnp.clip(i - 1, 0, nt - 1)
    row = lambda w: pl.BlockSpec((tm, w), lambda i: (tile(i), 0))
    hbm = pl.BlockSpec(memory_space=pl.ANY)

    args, specs = [x, xs], [row(D_MODEL), _const_spec(xs.shape)]
    if has_attn:
        args += [attn, attn_s, wo]
        specs += [row(D_MODEL), _const_spec(attn_s.shape), _const_spec(wo.shape)]
    args += [g, wg, wu, wd]
    specs += [_const_spec(g.shape), hbm, hbm, hbm]
    if has_kv:
        per_seq = seq // tm
        args += [kv_g, wkv, *tabs_p, *tabs_s]
        specs += [_const_spec(kv_g.shape), _const_spec(wkv.shape)]
        specs += [pl.BlockSpec((tm, LANES), lambda i: (tile(i) % per_seq, 0))] * 3
        specs += [_const_spec((1, LANES))] * 3
    if has_final:
        args.append(final_g)
        specs.append(_const_spec(final_g.shape))

    out_shape = [jax.ShapeDtypeStruct((mp, D_MODEL), F32), jax.ShapeDtypeStruct((ms, D_MODEL), F32)]
    out_specs = [row(D_MODEL), _const_spec((ms, D_MODEL))]
    if has_kv:
        out_shape += [jax.ShapeDtypeStruct((mp, KV_DIM), F32)] * 2 + [jax.ShapeDtypeStruct((ms, KV_DIM), F32)] * 2
        out_specs += [row(KV_DIM)] * 2 + [_const_spec((ms, KV_DIM))] * 2

    scratch = [pltpu.VMEM((N_FF_CHUNKS, D_MODEL, 2 * FF_CHUNK), BF16),
               pltpu.VMEM((D_FF, D_MODEL), BF16),
               pltpu.VMEM((2, GU_ROWS, D_FF), F32),
               pltpu.VMEM((2, DN_ROWS, D_MODEL), F32),
               pltpu.SemaphoreType.DMA((2,)),
               pltpu.SemaphoreType.DMA((2,)),
               pltpu.VMEM((tm, D_MODEL), BF16),
               pltpu.VMEM((tm, D_FF), BF16)]
    if has_attn:
        scratch += [pltpu.VMEM((D_MODEL, D_MODEL), BF16), pltpu.VMEM((tm, D_MODEL), F32)]
    if has_kv:
        scratch += [pltpu.VMEM((D_MODEL, 2 * KV_DIM), BF16)]

    return pl.pallas_call(
        functools.partial(_ffn_kernel, nt=nt, tm=tm, ms=ms, layer=layer, has_attn=has_attn, has_kv=has_kv,
                          has_final=has_final),
        grid=(nt + 2,),
        in_specs=specs,
        out_specs=out_specs,
        out_shape=out_shape,
        scratch_shapes=scratch,
        compiler_params=pltpu.CompilerParams(
            dimension_semantics=("arbitrary",), vmem_limit_bytes=VMEM_LIMIT),
        name="ffn",
    )(*args)


HALO = 16


def _pool_prompt_kernel(x_ref, g_ref, w_ref, sc_ref, o_ref, buf_ref, hext_ref, *, tp):
    t = pl.program_id(1)

    @pl.when(t == 0)
    def _():
        hext_ref[0:HALO, :] = jnp.zeros((HALO, D_MODEL), F32)

    x = x_ref[...]
    h = _rms(x, g_ref[...])
    hext_ref[HALO:HALO + tp, :] = h
    pos = (t * tp + lax.broadcasted_iota(jnp.int32, (tp, 1), 0)).astype(F32)
    mixed = []
    for gi, w in enumerate(POOL_WINDOWS):
        cs = slice(gi * POOL_GROUP, (gi + 1) * POOL_GROUP)
        s = h[:, cs]
        for k in range(1, w):
            s = s + hext_ref[HALO - k:HALO - k + tp, cs]
        cnt = jnp.minimum(float(w), pos + 1.0)
        diff = (s / cnt - h[:, cs]).astype(BF16)
        mixed.append(jnp.dot(diff, w_ref[gi].astype(BF16), preferred_element_type=F32))
    out = jnp.concatenate(mixed, axis=1) * sc_ref[...]
    o_ref[...] = x + out

    @pl.when(t == pl.num_programs(1) - 1)
    def _():
        buf_ref[0] = hext_ref[HALO + tp - POOL_BUF:HALO + tp, :]

    hext_ref[0:HALO, :] = hext_ref[tp:tp + HALO, :]


def _pool_prompt(x, g, w, sc, *, batch, tp):
    m = x.shape[0]
    nb = m // batch // tp
    blk = pl.BlockSpec((tp, D_MODEL), lambda i, j: (i * nb + j, 0))
    return pl.pallas_call(
        functools.partial(_pool_prompt_kernel, tp=tp),
        grid=(batch, nb),
        in_specs=[blk, _const_spec(g.shape), _const_spec(w.shape), _const_spec(sc.shape)],
        out_specs=[blk, pl.BlockSpec((1, POOL_BUF, D_MODEL), lambda i, j: (i, 0, 0))],
        out_shape=[jax.ShapeDtypeStruct(x.shape, F32),
                   jax.ShapeDtypeStruct((batch, POOL_BUF, D_MODEL), F32)],
        scratch_shapes=[pltpu.VMEM((HALO + tp, D_MODEL), F32)],
        compiler_params=pltpu.CompilerParams(
            dimension_semantics=("arbitrary", "arbitrary"), vmem_limit_bytes=VMEM_LIMIT),
        name="pool_prompt",
    )(x, g, w, sc)


def _pool_sample_kernel(x_ref, buf_ref, g_ref, w_ref, sc_ref, o_ref, nbuf_ref):
    x = x_ref[...]
    h = _rms(x, g_ref[...])
    nbuf_ref[0:POOL_BUF - 1] = buf_ref[1:POOL_BUF]
    nbuf_ref[POOL_BUF - 1] = h
    mixed = []
    for gi, w in enumerate(POOL_WINDOWS):
        cs = slice(gi * POOL_GROUP, (gi + 1) * POOL_GROUP)
        s = h[:, cs]
        for k in range(1, w):
            s = s + buf_ref[POOL_BUF - k, :, cs]
        diff = (s / float(w) - h[:, cs]).astype(BF16)
        mixed.append(jnp.dot(diff, w_ref[gi].astype(BF16), preferred_element_type=F32))
    o_ref[...] = x + jnp.concatenate(mixed, axis=1) * sc_ref[...]


def _pool_sample(x, buf_t, g, w, sc):
    m = x.shape[0]
    return pl.pallas_call(
        _pool_sample_kernel,
        out_shape=[jax.ShapeDtypeStruct((m, D_MODEL), F32), jax.ShapeDtypeStruct(buf_t.shape, F32)],
        compiler_params=pltpu.CompilerParams(vmem_limit_bytes=VMEM_LIMIT),
        name="pool_sample",
    )(x, buf_t, g, w, sc)


def _q_sample_kernel(x_ref, g_ref, w_ref, c_ref, s1_ref, s2_ref, o_ref):
    n = _rms(x_ref[...], g_ref[...]).astype(BF16)
    y = jnp.dot(n, w_ref[...].astype(BF16), preferred_element_type=F32)
    tabs = [jnp.broadcast_to(r[...], (y.shape[0], LANES)) for r in (c_ref, s1_ref, s2_ref)]
    o_ref[...] = _rope(y, *tabs) * (HEAD_DIM ** -0.5)


def _q_sample(x, g, w, tabs):
    return pl.pallas_call(
        _q_sample_kernel,
        out_shape=jax.ShapeDtypeStruct(x.shape, F32),
        compiler_params=pltpu.CompilerParams(vmem_limit_bytes=VMEM_LIMIT),
        name="q_sample",
    )(x, g, w, *tabs)


def _seg_mask(rows):
    lane = lax.broadcasted_iota(jnp.int32, (rows, KV_DIM), 1)
    return [(lane >= kv * HEAD_DIM) & (lane < (kv + 1) * HEAD_DIM) for kv in range(N_KV)]


def _attn_prompt_kernel(x_ref, g_ref, wq_ref, c_ref, s1_ref, s2_ref, k_ref, v_ref, sink_ref,
                        o_ref, kbuf, vbuf, *, tq):
    n = pl.program_id(1)

    @pl.when(n == 0)
    def _():
        kbuf[0:WINDOW, :] = jnp.zeros((WINDOW, KV_DIM), F32)
        vbuf[0:WINDOW, :] = jnp.zeros((WINDOW, KV_DIM), F32)

    kbuf[WINDOW:WINDOW + tq, :] = k_ref[...]
    vbuf[WINDOW:WINDOW + tq, :] = v_ref[...]

    hn = _rms(x_ref[...], g_ref[...]).astype(BF16)
    q = jnp.dot(hn, wq_ref[...], preferred_element_type=F32)
    q = (_rope(q, c_ref[...], s1_ref[...], s2_ref[...]) * (HEAD_DIM ** -0.5)).astype(BF16)

    seg2 = _seg_mask(2 * WINDOW)
    seg1 = _seg_mask(WINDOW)
    qi = lax.broadcasted_iota(jnp.int32, (WINDOW, 2 * WINDOW), 0)
    kj = lax.broadcasted_iota(jnp.int32, (WINDOW, 2 * WINDOW), 1)
    in_window = ((kj < WINDOW) & (kj > qi)) | ((kj >= WINDOW) & (kj - WINDOW <= qi))
    first_key = jnp.where(n > 0, 0, WINDOW)

    for i in range(tq // WINDOW):
        kk = kbuf[i * WINDOW:(i + 2) * WINDOW, :]
        vv = vbuf[i * WINDOW:(i + 2) * WINDOW, :]
        kbd = jnp.concatenate([jnp.where(m, kk, 0.0).astype(BF16) for m in seg2], axis=0)
        vbd = jnp.concatenate([jnp.where(m, vv, 0.0).astype(BF16) for m in seg2], axis=0)
        mask = (in_window & (kj >= first_key)) if i == 0 else in_window
        for gm in range(GROUP):
            qg = q[i * WINDOW:(i + 1) * WINDOW, gm * KV_DIM:(gm + 1) * KV_DIM]
            s = lax.dot_general(qg, kbd, (((1,), (1,)), ((), ())), preferred_element_type=F32)
            ps, rinv = [], jnp.zeros((WINDOW, KV_DIM), F32)
            for kv in range(N_KV):
                skv = jnp.where(mask, s[:, kv * 2 * WINDOW:(kv + 1) * 2 * WINDOW], -jnp.inf)
                sink = sink_ref[gm, kv]
                mx = jnp.maximum(jnp.max(skv, axis=-1, keepdims=True), sink)
                p = jnp.exp(skv - mx)
                den = jnp.sum(p, axis=-1, keepdims=True) + jnp.exp(sink - mx)
                ps.append(p.astype(BF16))
                rinv = jnp.where(seg1[kv], 1.0 / den, rinv)
            pcat = jnp.concatenate(ps, axis=1)
            og = jnp.dot(pcat, vbd, preferred_element_type=F32) * rinv
            o_ref[i * WINDOW:(i + 1) * WINDOW, gm * KV_DIM:(gm + 1) * KV_DIM] = og.astype(o_ref.dtype)

    kbuf[0:WINDOW, :] = kbuf[tq:tq + WINDOW, :]
    vbuf[0:WINDOW, :] = vbuf[tq:tq + WINDOW, :]


def _attn_prompt(x, g, wq, tabs, k, v, sinks, *, batch, tq):
    m = x.shape[0]
    nb = m // batch // tq
    row = lambda w: pl.BlockSpec((tq, w), lambda b, j: (b * nb + j, 0))
    return pl.pallas_call(
        functools.partial(_attn_prompt_kernel, tq=tq),
        grid=(batch, nb),
        in_specs=[row(D_MODEL), _const_spec(g.shape), _const_spec(wq.shape),
                  pl.BlockSpec((tq, LANES), lambda b, j: (j, 0)),
                  pl.BlockSpec((tq, LANES), lambda b, j: (j, 0)),
                  pl.BlockSpec((tq, LANES), lambda b, j: (j, 0)),
                  row(KV_DIM), row(KV_DIM), pl.BlockSpec(memory_space=pltpu.SMEM)],
        out_specs=row(D_MODEL),
        out_shape=jax.ShapeDtypeStruct((m, D_MODEL), BF16),
        scratch_shapes=[pltpu.VMEM((WINDOW + tq, KV_DIM), F32), pltpu.VMEM((WINDOW + tq, KV_DIM), F32)],
        compiler_params=pltpu.CompilerParams(
            dimension_semantics=("arbitrary", "arbitrary"), vmem_limit_bytes=VMEM_LIMIT),
        name="attn_prompt",
    )(x, g, wq, *tabs, k, v, sinks)


def _attn_sample_kernel(q_ref, kt_ref, vt_ref, knt_ref, vnt_ref, sink_ref, o_ref, ko_ref, vo_ref, *, bb):
    i = pl.program_id(0)
    nb = knt_ref.shape[1]
    lane = lax.broadcasted_iota(jnp.int32, (HEAD_DIM, WINDOW), 1)
    bcol = lax.broadcasted_iota(jnp.int32, (KV_DIM, nb), 1)
    col = lax.broadcasted_iota(jnp.int32, (N_HEADS, N_KV * WINDOW), 1)
    row = lax.broadcasted_iota(jnp.int32, (N_HEADS, N_KV * WINDOW), 0)
    own = (col // WINDOW) == (row % N_KV)
    sink = sink_ref[...]
    for j in range(bb):
        pick = bcol == i * bb + j
        kn = jnp.sum(jnp.where(pick, knt_ref[...], 0.0), axis=1, keepdims=True)
        vn = jnp.sum(jnp.where(pick, vnt_ref[...], 0.0), axis=1, keepdims=True)
        ks, vs = [], []
        for kv in range(N_KV):
            hs = slice(kv * HEAD_DIM, (kv + 1) * HEAD_DIM)
            k1 = jnp.where(lane == WINDOW - 1, kn[hs], pltpu.roll(kt_ref[j, kv], WINDOW - 1, axis=1))
            v1 = jnp.where(lane == WINDOW - 1, vn[hs], pltpu.roll(vt_ref[j, kv], WINDOW - 1, axis=1))
            ko_ref[j, kv] = k1
            vo_ref[j, kv] = v1
            ks.append(k1.astype(BF16))
            vs.append(v1.astype(BF16))
        kcat = jnp.concatenate(ks, axis=1)
        vcat = jnp.concatenate(vs, axis=1)
        s = jnp.dot(q_ref[j].astype(BF16), kcat, preferred_element_type=F32)
        s = jnp.where(own, s, -jnp.inf)
        mx = jnp.maximum(jnp.max(s, axis=-1, keepdims=True), sink)
        p = jnp.exp(s - mx)
        den = jnp.sum(p, axis=-1, keepdims=True) + jnp.exp(sink - mx)
        o = lax.dot_general(p.astype(BF16), vcat, (((1,), (1,)), ((), ())), preferred_element_type=F32)
        o_ref[j] = o / den


def _attn_sample(q3, kt, vt, knt, vnt, sinks, *, bb):
    b = q3.shape[0]
    blk4 = pl.BlockSpec((bb, N_KV, HEAD_DIM, WINDOW), lambda i: (i, 0, 0, 0))
    blk3 = pl.BlockSpec((bb, N_HEADS, HEAD_DIM), lambda i: (i, 0, 0))
    return pl.pallas_call(
        functools.partial(_attn_sample_kernel, bb=bb),
        grid=(b // bb,),
        in_specs=[blk3, blk4, blk4, _const_spec(knt.shape), _const_spec(vnt.shape), _const_spec(sinks.shape)],
        out_specs=[blk3, blk4, blk4],
        out_shape=[jax.ShapeDtypeStruct(q3.shape, F32), jax.ShapeDtypeStruct(kt.shape, F32),
                   jax.ShapeDtypeStruct(vt.shape, F32)],
        compiler_params=pltpu.CompilerParams(
            dimension_semantics=("arbitrary",), vmem_limit_bytes=VMEM_LIMIT),
        name="attn_sample",
    )(q3, kt, vt, knt, vnt, sinks)


def kernel(x_prompt, x_sample, state_pool, cache_k_win, cache_v_win, norm_g, ffn_w_gate, ffn_w_up,
           ffn_w_down, pool_w, pool_scale, kv_norm_g, w_kv, w_q, w_o, attn_sinks, final_norm_g):
    bp, seq, _ = x_prompt.shape
    bs = x_sample.shape[0]
    tm = 512
    g = lambda l, i: norm_g[l, i].reshape(1, D_MODEL)
    ffn_w = (ffn_w_gate.reshape(4, D_MODEL, D_FF), ffn_w_up.reshape(4, D_MODEL, D_FF),
             ffn_w_down.reshape(4, D_FF, D_MODEL))
    psc = pool_scale[0].reshape(1, D_MODEL)
    kvg = kv_norm_g.reshape(1, D_MODEL)
    wq = w_q[0].reshape(D_MODEL, N_KV, GROUP, HEAD_DIM).transpose(0, 2, 1, 3).reshape(D_MODEL, D_MODEL)
    wo = w_o[0].reshape(N_KV, GROUP, HEAD_DIM, D_MODEL).transpose(1, 0, 2, 3).reshape(D_MODEL, D_MODEL)
    sinks_gk = attn_sinks[0].reshape(N_KV, GROUP).T
    gfin = final_norm_g.reshape(1, D_MODEL)
    tabs_p = _rope_tables(np.arange(seq))
    tabs_s = _rope_tables(np.full((1,), PAST_LEN))

    x = x_prompt.reshape(bp * seq, D_MODEL)
    xs = x_sample.reshape(bs, D_MODEL)

    x, xs = _ffn(x, xs, g(0, 0), *ffn_w, layer=0, tm=tm, seq=seq)
    x, pool_p = _pool_prompt(x, g(0, 1), pool_w[0], psc, batch=bp, tp=512)
    buf_t = jnp.transpose(state_pool[0], (1, 0, 2))
    xs, nbuf_t = _pool_sample(xs, buf_t, g(0, 1), pool_w[0], psc)
    x, xs, k_p, v_p, k_s, v_s = _ffn(x, xs, g(0, 2), *ffn_w, layer=1, tm=tm, seq=seq,
                                     kv_g=kvg, wkv=w_kv, tabs_p=tabs_p, tabs_s=tabs_s)
    x, xs = _ffn(x, xs, g(1, 0), *ffn_w, layer=2, tm=tm, seq=seq)
    attn = _attn_prompt(x, g(1, 1), wq.astype(BF16), tabs_p, k_p, v_p, sinks_gk, batch=bp, tq=512)
    q_s = _q_sample(xs, g(1, 1), wq, tabs_s)
    kt = jnp.transpose(cache_k_win, (0, 2, 3, 1))
    vt = jnp.transpose(cache_v_win, (0, 2, 3, 1))
    o_s, kt_new, vt_new = _attn_sample(q_s.reshape(bs, N_HEADS, HEAD_DIM), kt, vt, k_s.T, v_s.T,
                                       sinks_gk.reshape(N_HEADS, 1), bb=8)
    attn_s = o_s.reshape(bs, D_MODEL).astype(BF16)
    y_p, y_s = _ffn(x, xs, g(1, 2), *ffn_w, layer=3, tm=tm, seq=seq, attn=attn, attn_s=attn_s, wo=wo,
                    final_g=gfin)

    y_prompt = y_p.reshape(bp, seq, D_MODEL)
    y_sample = y_s.reshape(bs, 1, D_MODEL)
    pool_prompt = pool_p[None]
    pool_sample = jnp.transpose(nbuf_t, (1, 0, 2))[None]
    k_win_prompt = k_p.reshape(bp, seq, KV_DIM)[:, seq - WINDOW:].reshape(bp, WINDOW, N_KV, HEAD_DIM)
    v_win_prompt = v_p.reshape(bp, seq, KV_DIM)[:, seq - WINDOW:].reshape(bp, WINDOW, N_KV, HEAD_DIM)
    k_win_sample = jnp.transpose(kt_new, (0, 3, 1, 2))
    v_win_sample = jnp.transpose(vt_new, (0, 3, 1, 2))
    return (y_prompt, y_sample, pool_prompt, pool_sample, k_win_prompt, v_win_prompt,
            k_win_sample, v_win_sample)
```

```python
import functools

import numpy as np
import jax
import jax.numpy as jnp
from jax import lax
from jax.experimental import pallas as pl
from jax.experimental.pallas import tpu as pltpu

F32 = jnp.float32
BF16 = jnp.bfloat16

D_MODEL = 1024
D_FF = 2816
HEAD_DIM = 64
N_HEADS = 16
N_KV = 4
GROUP = 4
KV_DIM = N_KV * HEAD_DIM
WINDOW = 128
ROT_DIM = 16
ROPE_THETA = 500000.0
EPS = 1e-5
POOL_WINDOWS = (2, 4, 8, 16)
POOL_GROUP = 256
POOL_BUF = 15
PAST_LEN = 16384

LANES = 128
FF_CHUNK = 256
N_FF_CHUNKS = D_FF // FF_CHUNK
OUT_CHUNK = 256
N_OUT_CHUNKS = D_MODEL // OUT_CHUNK
VMEM_LIMIT = 56 * 1024 * 1024

GU_SLABS = 8
GU_ROWS = D_MODEL // GU_SLABS
DN_SLABS = 8
DN_ROWS = D_FF // DN_SLABS


def _const_spec(shape):
    nd = len(shape)
    return pl.BlockSpec(shape, lambda *_: (0,) * nd, pipeline_mode=pl.Buffered(1))


def _rms(x, g):
    ms = jnp.mean(x * x, axis=-1, keepdims=True)
    return x * lax.rsqrt(ms + EPS) * g


def _rope(x, c, s1, s2):
    pieces = []
    for j in range(x.shape[1] // LANES):
        xc = x[:, j * LANES:(j + 1) * LANES]
        up = pltpu.roll(xc, LANES - ROT_DIM // 2, axis=1)
        dn = pltpu.roll(xc, ROT_DIM // 2, axis=1)
        pieces.append(xc * c + up * s1 + dn * s2)
    return jnp.concatenate(pieces, axis=1) if len(pieces) > 1 else pieces[0]


def _rope_tables(pos):
    half = ROT_DIM // 2
    inv = np.power(ROPE_THETA, -np.arange(half, dtype=np.float64) * (2.0 / ROT_DIM))
    ang = np.asarray(pos, np.float64)[:, None] * inv[None, :]
    cos, sin = np.cos(ang), np.sin(ang)
    n = ang.shape[0]
    ones = np.ones((n, HEAD_DIM - ROT_DIM))
    zeros8 = np.zeros((n, half))
    zeros = np.zeros((n, HEAD_DIM - ROT_DIM))
    c = np.concatenate([cos, cos, ones], axis=1)
    s1 = np.concatenate([-sin, zeros8, zeros], axis=1)
    s2 = np.concatenate([zeros8, sin, zeros], axis=1)
    return [jnp.asarray(np.concatenate([t, t], axis=1), F32) for t in (c, s1, s2)]


def _stage_ffn_weights(wg_hbm, wu_hbm, wd_hbm, wgu_scr, wd_scr, gu_stage, dn_stage, gu_sem, dn_sem):
    chunks = []
    for src, col0 in ((wg_hbm, 0), (wu_hbm, FF_CHUNK)):
        for r in range(GU_SLABS):
            chunks.append(("gu", src, r, col0))
    for r in range(DN_SLABS):
        chunks.append(("dn", wd_hbm, r, 0))

    def copy(c):
        kind, src, r, _ = chunks[c]
        slot = c % 2
        if kind == "gu":
            return pltpu.make_async_copy(src.at[pl.ds(r * GU_ROWS, GU_ROWS), :], gu_stage.at[slot], gu_sem.at[slot])
        return pltpu.make_async_copy(src.at[pl.ds(r * DN_ROWS, DN_ROWS), :], dn_stage.at[slot], dn_sem.at[slot])

    copy(0).start()
    copy(1).start()
    for c, (kind, _, r, col0) in enumerate(chunks):
        slot = c % 2
        copy(c).wait()
        if kind == "gu":
            for j in range(N_FF_CHUNKS):
                wgu_scr[j, r * GU_ROWS:(r + 1) * GU_ROWS, col0:col0 + FF_CHUNK] = (
                    gu_stage[slot, :, j * FF_CHUNK:(j + 1) * FF_CHUNK].astype(BF16))
        else:
            wd_scr[r * DN_ROWS:(r + 1) * DN_ROWS, :] = dn_stage[slot].astype(BF16)
        if c + 2 < len(chunks):
            copy(c + 2).start()


def _ffn_rows(rows, x_ref, a_ref, g_ref, gf_ref, kvg_ref, tabs, o_ref, k_ref, v_ref,
              wgu_scr, wd_scr, wo_scr, wkv_scr, xn_ref, h_ref, xr_ref):
    rs = slice(0, rows)
    if a_ref is not None:
        xr_ref[rs, :] = x_ref[...] + jnp.dot(a_ref[...], wo_scr[...], preferred_element_type=F32)
        res = lambda sl: xr_ref[rs, sl]
    else:
        res = lambda sl: x_ref[:, sl]
    xn_ref[rs, :] = _rms(res(slice(None)), g_ref[...]).astype(BF16)
    for j in range(N_FF_CHUNKS):
        r = jnp.dot(xn_ref[rs, :], wgu_scr[j], preferred_element_type=F32)
        gate, up = r[:, :FF_CHUNK], r[:, FF_CHUNK:]
        h_ref[rs, j * FF_CHUNK:(j + 1) * FF_CHUNK] = (gate * jax.nn.sigmoid(gate) * up).astype(BF16)
    for n in range(N_OUT_CHUNKS):
        sl = slice(n * OUT_CHUNK, (n + 1) * OUT_CHUNK)
        y = jnp.dot(h_ref[rs, :], wd_scr[:, sl], preferred_element_type=F32)
        o_ref[:, sl] = res(sl) + 0.5 * y
    if kvg_ref is not None:
        kvn = _rms(o_ref[...], kvg_ref[...]).astype(BF16)
        kv = jnp.dot(kvn, wkv_scr[...], preferred_element_type=F32)
        tb = [t[...] for t in tabs]
        if tb[0].shape[0] != rows:
            tb = [jnp.broadcast_to(t, (rows, LANES)) for t in tb]
        k_ref[...] = _rope(kv[:, :KV_DIM], *tb)
        v_ref[...] = kv[:, KV_DIM:]
    if gf_ref is not None:
        o_ref[...] = _rms(o_ref[...], gf_ref[...])


def _ffn_kernel(*refs, nt, tm, ms, layer, has_attn, has_kv, has_final):
    it = iter(refs)
    x_ref, xs_ref = next(it), next(it)
    a_ref = as_ref = wo_ref = gf_ref = kvg_ref = wkv_ref = None
    tabs_p = tabs_s = None
    if has_attn:
        a_ref, as_ref, wo_ref = next(it), next(it), next(it)
    g_ref, wg_hbm, wu_hbm, wd_hbm = next(it), next(it), next(it), next(it)
    if has_kv:
        kvg_ref, wkv_ref = next(it), next(it)
        tabs_p = [next(it) for _ in range(3)]
        tabs_s = [next(it) for _ in range(3)]
    if has_final:
        gf_ref = next(it)
    o_ref, os_ref = next(it), next(it)
    k_ref = v_ref = ks_ref = vs_ref = None
    if has_kv:
        k_ref, v_ref, ks_ref, vs_ref = next(it), next(it), next(it), next(it)
    wgu_scr, wd_scr, gu_stage, dn_stage, gu_sem, dn_sem, xn_ref, h_ref = [next(it) for _ in range(8)]
    wo_scr = xr_ref = wkv_scr = None
    if has_attn:
        wo_scr, xr_ref = next(it), next(it)
    if has_kv:
        wkv_scr = next(it)

    i = pl.program_id(0)

    @pl.when(i == 0)
    def _():
        _stage_ffn_weights(wg_hbm.at[layer], wu_hbm.at[layer], wd_hbm.at[layer], wgu_scr, wd_scr,
                           gu_stage, dn_stage, gu_sem, dn_sem)
        if has_attn:
            wo_scr[...] = wo_ref[...].astype(BF16)
        if has_kv:
            wkv_scr[...] = wkv_ref[...].astype(BF16)

    common = (wgu_scr, wd_scr, wo_scr, wkv_scr, xn_ref, h_ref, xr_ref)

    @pl.when((i >= 1) & (i <= nt))
    def _():
        _ffn_rows(tm, x_ref, a_ref, g_ref, gf_ref, kvg_ref, tabs_p, o_ref, k_ref, v_ref, *common)

    @pl.when(i == nt + 1)
    def _():
        _ffn_rows(ms, xs_ref, as_ref, g_ref, gf_ref, kvg_ref, tabs_s, os_ref, ks_ref, vs_ref, *common)


def _ffn(x, xs, g, wg, wu, wd, *, layer, tm, seq, attn=None, attn_s=None, wo=None,
         kv_g=None, wkv=None, tabs_p=None, tabs_s=None, final_g=None):
    mp, ms = x.shape[0], xs.shape[0]
    nt = mp // tm
    has_attn, has_kv, has_final = attn is not None, kv_g is not None, final_g is not None
    tile = lambda i: jnp.clip(i - 1, 0, nt - 1)
    row = lambda w: pl.BlockSpec((tm, w), lambda i: (tile(i), 0))
    hbm = pl.BlockSpec(memory_space=pl.ANY)

    args, specs = [x, xs], [row(D_MODEL), _const_spec(xs.shape)]
    if has_attn:
        args += [attn, attn_s, wo]
        specs += [row(D_MODEL), _const_spec(attn_s.shape), _const_spec(wo.shape)]
    args += [g, wg, wu, wd]
    specs += [_const_spec(g.shape), hbm, hbm, hbm]
    if has_kv:
        per_seq = seq // tm
        args += [kv_g, wkv, *tabs_p, *tabs_s]
        specs += [_const_spec(kv_g.shape), _const_spec(wkv.shape)]
        specs += [pl.BlockSpec((tm, LANES), lambda i: (tile(i) % per_seq, 0))] * 3
        specs += [_const_spec((1, LANES))] * 3
    if has_final:
        args.append(final_g)
        specs.append(_const_spec(final_g.shape))

    out_shape = [jax.ShapeDtypeStruct((mp, D_MODEL), F32), jax.ShapeDtypeStruct((ms, D_MODEL), F32)]
    out_specs = [row(D_MODEL), _const_spec((ms, D_MODEL))]
    if has_kv:
        out_shape += [jax.ShapeDtypeStruct((mp, KV_DIM), F32)] * 2 + [jax.ShapeDtypeStruct((ms, KV_DIM), F32)] * 2
        out_specs += [row(KV_DIM)] * 2 + [_const_spec((ms, KV_DIM))] * 2

    scratch = [pltpu.VMEM((N_FF_CHUNKS, D_MODEL, 2 * FF_CHUNK), BF16),
               pltpu.VMEM((D_FF, D_MODEL), BF16),
               pltpu.VMEM((2, GU_ROWS, D_FF), F32),
               pltpu.VMEM((2, DN_ROWS, D_MODEL), F32),
               pltpu.SemaphoreType.DMA((2,)),
               pltpu.SemaphoreType.DMA((2,)),
               pltpu.VMEM((tm, D_MODEL), BF16),
               pltpu.VMEM((tm, D_FF), BF16)]
    if has_attn:
        scratch += [pltpu.VMEM((D_MODEL, D_MODEL), BF16), pltpu.VMEM((tm, D_MODEL), F32)]
    if has_kv:
        scratch += [pltpu.VMEM((D_MODEL, 2 * KV_DIM), BF16)]

    return pl.pallas_call(
        functools.partial(_ffn_kernel, nt=nt, tm=tm, ms=ms, layer=layer, has_attn=has_attn, has_kv=has_kv,
                          has_final=has_final),
        grid=(nt + 2,),
        in_specs=specs,
        out_specs=out_specs,
        out_shape=out_shape,
        scratch_shapes=scratch,
        compiler_params=pltpu.CompilerParams(
            dimension_semantics=("arbitrary",), vmem_limit_bytes=VMEM_LIMIT),
        name="ffn",
    )(*args)


HALO = 16


def _pool_prompt_kernel(x_ref, g_ref, w_ref, sc_ref, o_ref, buf_ref, hext_ref, *, tp):
    t = pl.program_id(1)

    @pl.when(t == 0)
    def _():
        hext_ref[0:HALO, :] = jnp.zeros((HALO, D_MODEL), F32)

    x = x_ref[...]
    h = _rms(x, g_ref[...])
    hext_ref[HALO:HALO + tp, :] = h
    pos = (t * tp + lax.broadcasted_iota(jnp.int32, (tp, 1), 0)).astype(F32)
    mixed = []
    for gi, w in enumerate(POOL_WINDOWS):
        cs = slice(gi * POOL_GROUP, (gi + 1) * POOL_GROUP)
        s = h[:, cs]
        for k in range(1, w):
            s = s + hext_ref[HALO - k:HALO - k + tp, cs]
        cnt = jnp.minimum(float(w), pos + 1.0)
        diff = (s / cnt - h[:, cs]).astype(BF16)
        mixed.append(jnp.dot(diff, w_ref[gi].astype(BF16), preferred_element_type=F32))
    out = jnp.concatenate(mixed, axis=1) * sc_ref[...]
    o_ref[...] = x + out

    @pl.when(t == pl.num_programs(1) - 1)
    def _():
        buf_ref[0] = hext_ref[HALO + tp - POOL_BUF:HALO + tp, :]

    hext_ref[0:HALO, :] = hext_ref[tp:tp + HALO, :]


def _pool_prompt(x, g, w, sc, *, batch, tp):
    m = x.shape[0]
    nb = m // batch // tp
    blk = pl.BlockSpec((tp, D_MODEL), lambda i, j: (i * nb + j, 0))
    return pl.pallas_call(
        functools.partial(_pool_prompt_kernel, tp=tp),
        grid=(batch, nb),
        in_specs=[blk, _const_spec(g.shape), _const_spec(w.shape), _const_spec(sc.shape)],
        out_specs=[blk, pl.BlockSpec((1, POOL_BUF, D_MODEL), lambda i, j: (i, 0, 0))],
        out_shape=[jax.ShapeDtypeStruct(x.shape, F32),
                   jax.ShapeDtypeStruct((batch, POOL_BUF, D_MODEL), F32)],
        scratch_shapes=[pltpu.VMEM((HALO + tp, D_MODEL), F32)],
        compiler_params=pltpu.CompilerParams(
            dimension_semantics=("arbitrary", "arbitrary"), vmem_limit_bytes=VMEM_LIMIT),
        name="pool_prompt",
    )(x, g, w, sc)


def _pool_sample_kernel(x_ref, buf_ref, g_ref, w_ref, sc_ref, o_ref, nbuf_ref):
    x = x_ref[...]
    h = _rms(x, g_ref[...])
    nbuf_ref[0:POOL_BUF - 1] = buf_ref[1:POOL_BUF]
    nbuf_ref[POOL_BUF - 1] = h
    mixed = []
    for gi, w in enumerate(POOL_WINDOWS):
        cs = slice(gi * POOL_GROUP, (gi + 1) * POOL_GROUP)
        s = h[:, cs]
        for k in range(1, w):
            s = s + buf_ref[POOL_BUF - k, :, cs]
        diff = (s / float(w) - h[:, cs]).astype(BF16)
        mixed.append(jnp.dot(diff, w_ref[gi].astype(BF16), preferred_element_type=F32))
    o_ref[...] = x + jnp.concatenate(mixed, axis=1) * sc_ref[...]


def _pool_sample(x, buf_t, g, w, sc):
    m = x.shape[0]
    return pl.pallas_call(
        _pool_sample_kernel,
        out_shape=[jax.ShapeDtypeStruct((m, D_MODEL), F32), jax.ShapeDtypeStruct(buf_t.shape, F32)],
        compiler_params=pltpu.CompilerParams(vmem_limit_bytes=VMEM_LIMIT),
        name="pool_sample",
    )(x, buf_t, g, w, sc)


def _q_sample_kernel(x_ref, g_ref, w_ref, c_ref, s1_ref, s2_ref, o_ref):
    n = _rms(x_ref[...], g_ref[...]).astype(BF16)
    y = jnp.dot(n, w_ref[...].astype(BF16), preferred_element_type=F32)
    tabs = [jnp.broadcast_to(r[...], (y.shape[0], LANES)) for r in (c_ref, s1_ref, s2_ref)]
    o_ref[...] = _rope(y, *tabs) * (HEAD_DIM ** -0.5)


def _q_sample(x, g, w, tabs):
    return pl.pallas_call(
        _q_sample_kernel,
        out_shape=jax.ShapeDtypeStruct(x.shape, F32),
        compiler_params=pltpu.CompilerParams(vmem_limit_bytes=VMEM_LIMIT),
        name="q_sample",
    )(x, g, w, *tabs)


def _seg_mask(rows):
    lane = lax.broadcasted_iota(jnp.int32, (rows, KV_DIM), 1)
    return [(lane >= kv * HEAD_DIM) & (lane < (kv + 1) * HEAD_DIM) for kv in range(N_KV)]


def _attn_prompt_kernel(x_ref, g_ref, wq_ref, c_ref, s1_ref, s2_ref, k_ref, v_ref, sink_ref,
                        o_ref, kbuf, vbuf, *, tq):
    n = pl.program_id(1)

    @pl.when(n == 0)
    def _():
        kbuf[0:WINDOW, :] = jnp.zeros((WINDOW, KV_DIM), F32)
        vbuf[0:WINDOW, :] = jnp.zeros((WINDOW, KV_DIM), F32)

    kbuf[WINDOW:WINDOW + tq, :] = k_ref[...]
    vbuf[WINDOW:WINDOW + tq, :] = v_ref[...]

    hn = _rms(x_ref[...], g_ref[...]).astype(BF16)
    q = jnp.dot(hn, wq_ref[...], preferred_element_type=F32)
    q = (_rope(q, c_ref[...], s1_ref[...], s2_ref[...]) * (HEAD_DIM ** -0.5)).astype(BF16)

    seg2 = _seg_mask(2 * WINDOW)
    seg1 = _seg_mask(WINDOW)
    qi = lax.broadcasted_iota(jnp.int32, (WINDOW, WINDOW), 0)
    kj = lax.broadcasted_iota(jnp.int32, (WINDOW, WINDOW), 1)
    causal = kj <= qi
    prev_ok = kj <= qi + jnp.where(n > 0, WINDOW, 0)

    for i in range(tq // WINDOW):
        kk = kbuf[i * WINDOW:(i + 2) * WINDOW, :]
        vv = vbuf[i * WINDOW:(i + 2) * WINDOW, :]
        kbd = jnp.concatenate([jnp.where(m, kk, 0.0).astype(BF16) for m in seg2], axis=0)
        vbd = jnp.concatenate([jnp.where(m, vv, 0.0).astype(BF16) for m in seg2], axis=0)
        rows = slice(i * WINDOW, (i + 1) * WINDOW)
        qs = jnp.concatenate([q[rows, gm * KV_DIM:(gm + 1) * KV_DIM] for gm in range(GROUP)], axis=0)
        s = lax.dot_general(qs, kbd, (((1,), (1,)), ((), ())), preferred_element_type=F32)
        pg, rg = [], []
        for gm in range(GROUP):
            ps, rinv = [], jnp.zeros((WINDOW, KV_DIM), F32)
            for kv in range(N_KV):
                s_prev = s[gm * WINDOW:(gm + 1) * WINDOW, kv * 2 * WINDOW:kv * 2 * WINDOW + WINDOW]
                s_own = s[gm * WINDOW:(gm + 1) * WINDOW, kv * 2 * WINDOW + WINDOW:(kv + 1) * 2 * WINDOW]
                sc = jnp.where(causal, s_own, s_prev)
                if i == 0:
                    sc = jnp.where(prev_ok, sc, -jnp.inf)
                sink = sink_ref[gm, kv]
                mx = jnp.maximum(jnp.max(sc, axis=-1, keepdims=True), sink)
                p = jnp.exp(sc - mx)
                den = jnp.sum(p, axis=-1, keepdims=True) + jnp.exp(sink - mx)
                ps.append(jnp.where(causal, 0.0, p).astype(BF16))
                ps.append(jnp.where(causal, p, 0.0).astype(BF16))
                rinv = jnp.where(seg1[kv], 1.0 / den, rinv)
            pg.append(jnp.concatenate(ps, axis=1))
            rg.append(rinv)
        og = jnp.dot(jnp.concatenate(pg, axis=0), vbd, preferred_element_type=F32)
        for gm in range(GROUP):
            o_ref[rows, gm * KV_DIM:(gm + 1) * KV_DIM] = (
                og[gm * WINDOW:(gm + 1) * WINDOW] * rg[gm]).astype(o_ref.dtype)

    kbuf[0:WINDOW, :] = kbuf[tq:tq + WINDOW, :]
    vbuf[0:WINDOW, :] = vbuf[tq:tq + WINDOW, :]


def _attn_prompt(x, g, wq, tabs, k, v, sinks, *, batch, tq):
    m = x.shape[0]
    nb = m // batch // tq
    row = lambda w: pl.BlockSpec((tq, w), lambda b, j: (b * nb + j, 0))
    return pl.pallas_call(
        functools.partial(_attn_prompt_kernel, tq=tq),
        grid=(batch, nb),
        in_specs=[row(D_MODEL), _const_spec(g.shape), _const_spec(wq.shape),
                  pl.BlockSpec((tq, LANES), lambda b, j: (j, 0)),
                  pl.BlockSpec((tq, LANES), lambda b, j: (j, 0)),
                  pl.BlockSpec((tq, LANES), lambda b, j: (j, 0)),
                  row(KV_DIM), row(KV_DIM), pl.BlockSpec(memory_space=pltpu.SMEM)],
        out_specs=row(D_MODEL),
        out_shape=jax.ShapeDtypeStruct((m, D_MODEL), BF16),
        scratch_shapes=[pltpu.VMEM((WINDOW + tq, KV_DIM), F32), pltpu.VMEM((WINDOW + tq, KV_DIM), F32)],
        compiler_params=pltpu.CompilerParams(
            dimension_semantics=("arbitrary", "arbitrary"), vmem_limit_bytes=VMEM_LIMIT),
        name="attn_prompt",
    )(x, g, wq, *tabs, k, v, sinks)


def _attn_sample_kernel(q_ref, kt_ref, vt_ref, knt_ref, vnt_ref, sink_ref, o_ref, ko_ref, vo_ref, *, bb):
    i = pl.program_id(0)
    lane = lax.broadcasted_iota(jnp.int32, (HEAD_DIM, WINDOW), 1)
    col = lax.broadcasted_iota(jnp.int32, (N_HEADS, N_KV * WINDOW), 1)
    row = lax.broadcasted_iota(jnp.int32, (N_HEADS, N_KV * WINDOW), 0)
    own = (col // WINDOW) == (row % N_KV)
    sink = sink_ref[...]
    for j in range(bb):
        shift = (WINDOW - 1) - (i * bb + j)
        kn = pltpu.roll(knt_ref[...], shift, axis=1)
        vn = pltpu.roll(vnt_ref[...], shift, axis=1)
        ks, vs = [], []
        for kv in range(N_KV):
            hs = slice(kv * HEAD_DIM, (kv + 1) * HEAD_DIM)
            k1 = jnp.where(lane == WINDOW - 1, kn[hs], pltpu.roll(kt_ref[j, kv], WINDOW - 1, axis=1))
            v1 = jnp.where(lane == WINDOW - 1, vn[hs], pltpu.roll(vt_ref[j, kv], WINDOW - 1, axis=1))
            ko_ref[j, kv] = k1
            vo_ref[j, kv] = v1
            ks.append(k1.astype(BF16))
            vs.append(v1.astype(BF16))
        kcat = jnp.concatenate(ks, axis=1)
        vcat = jnp.concatenate(vs, axis=1)
        s = jnp.dot(q_ref[j].astype(BF16), kcat, preferred_element_type=F32)
        s = jnp.where(own, s, -jnp.inf)
        mx = jnp.maximum(jnp.max(s, axis=-1, keepdims=True), sink)
        p = jnp.exp(s - mx)
        den = jnp.sum(p, axis=-1, keepdims=True) + jnp.exp(sink - mx)
        o = lax.dot_general(p.astype(BF16), vcat, (((1,), (1,)), ((), ())), preferred_element_type=F32)
        o_ref[j] = o / den


def _attn_sample(q3, kt, vt, knt, vnt, sinks, *, bb):
    b = q3.shape[0]
    blk4 = pl.BlockSpec((bb, N_KV, HEAD_DIM, WINDOW), lambda i: (i, 0, 0, 0))
    blk3 = pl.BlockSpec((bb, N_HEADS, HEAD_DIM), lambda i: (i, 0, 0))
    return pl.pallas_call(
        functools.partial(_attn_sample_kernel, bb=bb),
        grid=(b // bb,),
        in_specs=[blk3, blk4, blk4, _const_spec(knt.shape), _const_spec(vnt.shape), _const_spec(sinks.shape)],
        out_specs=[blk3, blk4, blk4],
        out_shape=[jax.ShapeDtypeStruct(q3.shape, F32), jax.ShapeDtypeStruct(kt.shape, F32),
                   jax.ShapeDtypeStruct(vt.shape, F32)],
        compiler_params=pltpu.CompilerParams(
            dimension_semantics=("arbitrary",), vmem_limit_bytes=VMEM_LIMIT),
        name="attn_sample",
    )(q3, kt, vt, knt, vnt, sinks)


def kernel(x_prompt, x_sample, state_pool, cache_k_win, cache_v_win, norm_g, ffn_w_gate, ffn_w_up,
           ffn_w_down, pool_w, pool_scale, kv_norm_g, w_kv, w_q, w_o, attn_sinks, final_norm_g):
    bp, seq, _ = x_prompt.shape
    bs = x_sample.shape[0]
    tm = 512
    g = lambda l, i: norm_g[l, i].reshape(1, D_MODEL)
    ffn_w = (ffn_w_gate.reshape(4, D_MODEL, D_FF), ffn_w_up.reshape(4, D_MODEL, D_FF),
             ffn_w_down.reshape(4, D_FF, D_MODEL))
    psc = pool_scale[0].reshape(1, D_MODEL)
    kvg = kv_norm_g.reshape(1, D_MODEL)
    wq = w_q[0].reshape(D_MODEL, N_KV, GROUP, HEAD_DIM).transpose(0, 2, 1, 3).reshape(D_MODEL, D_MODEL)
    wo = w_o[0].reshape(N_KV, GROUP, HEAD_DIM, D_MODEL).transpose(1, 0, 2, 3).reshape(D_MODEL, D_MODEL)
    sinks_gk = attn_sinks[0].reshape(N_KV, GROUP).T
    gfin = final_norm_g.reshape(1, D_MODEL)
    tabs_p = _rope_tables(np.arange(seq))
    tabs_s = _rope_tables(np.full((1,), PAST_LEN))

    x = x_prompt.reshape(bp * seq, D_MODEL)
    xs = x_sample.reshape(bs, D_MODEL)

    x, xs = _ffn(x, xs, g(0, 0), *ffn_w, layer=0, tm=tm, seq=seq)
    x, pool_p = _pool_prompt(x, g(0, 1), pool_w[0], psc, batch=bp, tp=512)
    buf_t = jnp.transpose(state_pool[0], (1, 0, 2))
    xs, nbuf_t = _pool_sample(xs, buf_t, g(0, 1), pool_w[0], psc)
    x, xs, k_p, v_p, k_s, v_s = _ffn(x, xs, g(0, 2), *ffn_w, layer=1, tm=tm, seq=seq,
                                     kv_g=kvg, wkv=w_kv, tabs_p=tabs_p, tabs_s=tabs_s)
    x, xs = _ffn(x, xs, g(1, 0), *ffn_w, layer=2, tm=tm, seq=seq)
    attn = _attn_prompt(x, g(1, 1), wq.astype(BF16), tabs_p, k_p, v_p, sinks_gk, batch=bp, tq=512)
    q_s = _q_sample(xs, g(1, 1), wq, tabs_s)
    kt = jnp.transpose(cache_k_win, (0, 2, 3, 1))
    vt = jnp.transpose(cache_v_win, (0, 2, 3, 1))
    o_s, kt_new, vt_new = _attn_sample(q_s.reshape(bs, N_HEADS, HEAD_DIM), kt, vt, k_s.T, v_s.T,
                                       sinks_gk.reshape(N_HEADS, 1), bb=8)
    attn_s = o_s.reshape(bs, D_MODEL).astype(BF16)
    y_p, y_s = _ffn(x, xs, g(1, 2), *ffn_w, layer=3, tm=tm, seq=seq, attn=attn, attn_s=attn_s, wo=wo,
                    final_g=gfin)

    y_prompt = y_p.reshape(bp, seq, D_MODEL)
    y_sample = y_s.reshape(bs, 1, D_MODEL)
    pool_prompt = pool_p[None]
    pool_sample = jnp.transpose(nbuf_t, (1, 0, 2))[None]
    k_win_prompt = k_p.reshape(bp, seq, KV_DIM)[:, seq - WINDOW:].reshape(bp, WINDOW, N_KV, HEAD_DIM)
    v_win_prompt = v_p.reshape(bp, seq, KV_DIM)[:, seq - WINDOW:].reshape(bp, WINDOW, N_KV, HEAD_DIM)
    k_win_sample = jnp.transpose(kt_new, (0, 3, 1, 2))
    v_win_sample = jnp.transpose(vt_new, (0, 3, 1, 2))
    return (y_prompt, y_sample, pool_prompt, pool_sample, k_win_prompt, v_win_prompt,
            k_win_sample, v_win_sample)
```

```python
import functools

import numpy as np
import jax
import jax.numpy as jnp
from jax import lax
from jax.experimental import pallas as pl
from jax.experimental.pallas import tpu as pltpu

F32 = jnp.float32
BF16 = jnp.bfloat16

D_MODEL = 1024
D_FF = 2816
HEAD_DIM = 64
N_HEADS = 16
N_KV = 4
GROUP = 4
KV_DIM = N_KV * HEAD_DIM
WINDOW = 128
ROT_DIM = 16
ROPE_THETA = 500000.0
EPS = 1e-5
POOL_WINDOWS = (2, 4, 8, 16)
POOL_GROUP = 256
POOL_BUF = 15
PAST_LEN = 16384

LANES = 128
FF_CHUNK = 256
N_FF_CHUNKS = D_FF // FF_CHUNK
OUT_CHUNK = 256
N_OUT_CHUNKS = D_MODEL // OUT_CHUNK
VMEM_LIMIT = 60 * 1024 * 1024

GU_SLABS = 16
GU_ROWS = D_MODEL // GU_SLABS
DN_SLABS = 16
DN_ROWS = D_FF // DN_SLABS


def _const_spec(shape):
    nd = len(shape)
    return pl.BlockSpec(shape, lambda *_: (0,) * nd, pipeline_mode=pl.Buffered(1))


def _rms(x, g):
    ms = jnp.mean(x * x, axis=-1, keepdims=True)
    return x * lax.rsqrt(ms + EPS) * g


def _rope(x, c, s1, s2):
    pieces = []
    for j in range(x.shape[1] // LANES):
        xc = x[:, j * LANES:(j + 1) * LANES]
        up = pltpu.roll(xc, LANES - ROT_DIM // 2, axis=1)
        dn = pltpu.roll(xc, ROT_DIM // 2, axis=1)
        pieces.append(xc * c + up * s1 + dn * s2)
    return jnp.concatenate(pieces, axis=1) if len(pieces) > 1 else pieces[0]


def _rope_tables(pos):
    half = ROT_DIM // 2
    inv = np.power(ROPE_THETA, -np.arange(half, dtype=np.float64) * (2.0 / ROT_DIM))
    ang = np.asarray(pos, np.float64)[:, None] * inv[None, :]
    cos, sin = np.cos(ang), np.sin(ang)
    n = ang.shape[0]
    ones = np.ones((n, HEAD_DIM - ROT_DIM))
    zeros8 = np.zeros((n, half))
    zeros = np.zeros((n, HEAD_DIM - ROT_DIM))
    c = np.concatenate([cos, cos, ones], axis=1)
    s1 = np.concatenate([-sin, zeros8, zeros], axis=1)
    s2 = np.concatenate([zeros8, sin, zeros], axis=1)
    return [jnp.asarray(np.concatenate([t, t], axis=1), F32) for t in (c, s1, s2)]


def _stage_ffn_weights(wg_hbm, wu_hbm, wd_hbm, wgu_scr, wd_scr, gu_stage, dn_stage, gu_sem, dn_sem):
    chunks = []
    for src, col0 in ((wg_hbm, 0), (wu_hbm, FF_CHUNK)):
        for r in range(GU_SLABS):
            chunks.append(("gu", src, r, col0))
    for r in range(DN_SLABS):
        chunks.append(("dn", wd_hbm, r, 0))

    def copy(c):
        kind, src, r, _ = chunks[c]
        slot = c % 2
        if kind == "gu":
            return pltpu.make_async_copy(src.at[pl.ds(r * GU_ROWS, GU_ROWS), :], gu_stage.at[slot], gu_sem.at[slot])
        return pltpu.make_async_copy(src.at[pl.ds(r * DN_ROWS, DN_ROWS), :], dn_stage.at[slot], dn_sem.at[slot])

    copy(0).start()
    copy(1).start()
    for c, (kind, _, r, col0) in enumerate(chunks):
        slot = c % 2
        copy(c).wait()
        if kind == "gu":
            for j in range(N_FF_CHUNKS):
                wgu_scr[j, r * GU_ROWS:(r + 1) * GU_ROWS, col0:col0 + FF_CHUNK] = (
                    gu_stage[slot, :, j * FF_CHUNK:(j + 1) * FF_CHUNK].astype(BF16))
        else:
            wd_scr[r * DN_ROWS:(r + 1) * DN_ROWS, :] = dn_stage[slot].astype(BF16)
        if c + 2 < len(chunks):
            copy(c + 2).start()


def _run(*phase_generators, steps=None):
    live = list(zip(phase_generators, steps or (1,) * len(phase_generators)))
    while live:
        for gen, n in list(live):
            try:
                for _ in range(n):
                    next(gen)
            except StopIteration:
                live.remove((gen, n))


def _ffn_rows(rows, x_ref, a_ref, g_ref, gf_ref, kvg_ref, tabs, o_ref, k_ref, v_ref,
              wgu_scr, wd_scr, wo_ref, wkv_scr, xn_ref, h_ref, xr_ref):
    rs = slice(0, rows)
    if a_ref is not None:
        xr_ref[rs, :] = x_ref[...] + jnp.dot(a_ref[...], wo_ref[...], preferred_element_type=F32)
        res = lambda sl: xr_ref[rs, sl]
    else:
        res = lambda sl: x_ref[:, sl]
    xn_ref[rs, :] = _rms(res(slice(None)), g_ref[...]).astype(BF16)
    yield
    for j in range(N_FF_CHUNKS):
        r = jnp.dot(xn_ref[rs, :], wgu_scr[j], preferred_element_type=F32)
        gate, up = r[:, :FF_CHUNK], r[:, FF_CHUNK:]
        h_ref[rs, j * FF_CHUNK:(j + 1) * FF_CHUNK] = (gate * jax.nn.sigmoid(gate) * up).astype(BF16)
        yield
    for n in range(N_OUT_CHUNKS):
        sl = slice(n * OUT_CHUNK, (n + 1) * OUT_CHUNK)
        y = jnp.dot(h_ref[rs, :], wd_scr[:, sl], preferred_element_type=F32)
        o_ref[:, sl] = res(sl) + 0.5 * y
        yield
    if kvg_ref is not None:
        kvn = _rms(o_ref[...], kvg_ref[...]).astype(BF16)
        kv = jnp.dot(kvn, wkv_scr[...], preferred_element_type=F32)
        tb = [t[...] for t in tabs]
        if tb[0].shape[0] != rows:
            tb = [jnp.broadcast_to(t, (rows, LANES)) for t in tb]
        k_ref[...] = _rope(kv[:, :KV_DIM], *tb)
        v_ref[...] = kv[:, KV_DIM:]
    if gf_ref is not None:
        o_ref[...] = _rms(o_ref[...], gf_ref[...])


def _ffn_kernel(*refs, nt, tm, ms, per_seq, layer, has_attn, has_kv, has_final):
    it = iter(refs)
    x_ref, xs_ref = next(it), next(it)
    a_ref = as_ref = wo_ref = gf_ref = kvg_ref = wkv_ref = None
    tabs_p = tabs_s = None
    if has_attn:
        xq_ref, gq_ref, wq_ref = next(it), next(it), next(it)
        tabs_q = [next(it) for _ in range(3)]
        kq_ref, vq_ref, sink_ref, as_ref, wo_ref = [next(it) for _ in range(5)]
    g_ref, wg_hbm, wu_hbm, wd_hbm = next(it), next(it), next(it), next(it)
    if has_kv:
        kvg_ref, wkv_ref = next(it), next(it)
        tabs_p = [next(it) for _ in range(3)]
        tabs_s = [next(it) for _ in range(3)]
    if has_final:
        gf_ref = next(it)
    o_ref, os_ref = next(it), next(it)
    k_ref = v_ref = ks_ref = vs_ref = None
    if has_kv:
        k_ref, v_ref, ks_ref, vs_ref = next(it), next(it), next(it), next(it)
    wgu_scr, wd_scr, gu_stage, dn_stage, gu_sem, dn_sem, xn_ref, h_ref = [next(it) for _ in range(8)]
    xr_ref = wkv_scr = None
    if has_attn:
        a_ref, xr_ref, kbuf, vbuf = [next(it) for _ in range(4)]
    if has_kv:
        wkv_scr = next(it)

    i = pl.program_id(0)

    def attention(tile):
        nprev = jnp.where(tile % per_seq == 0, 0, WINDOW)
        return _attn_rows(nprev, xq_ref, gq_ref, wq_ref, tabs_q, kq_ref, vq_ref, sink_ref, a_ref,
                          kbuf, vbuf, tm)

    @pl.when(i == 0)
    def _():
        _stage_ffn_weights(wg_hbm.at[layer], wu_hbm.at[layer], wd_hbm.at[layer], wgu_scr, wd_scr,
                           gu_stage, dn_stage, gu_sem, dn_sem)
        if has_attn:
            kbuf[...] = jnp.zeros(kbuf.shape, F32)
            vbuf[...] = jnp.zeros(vbuf.shape, F32)
            _run(attention(0))
        if has_kv:
            wkv_scr[...] = wkv_ref[...].astype(BF16)

    common = (wgu_scr, wd_scr, wo_ref, wkv_scr, xn_ref, h_ref, xr_ref)

    @pl.when((i >= 1) & (i <= nt))
    def _():
        ffn = _ffn_rows(tm, x_ref, a_ref, g_ref, gf_ref, kvg_ref, tabs_p, o_ref, k_ref, v_ref, *common)
        if has_attn:
            _run(ffn, attention(jnp.minimum(i, nt - 1)))
        else:
            _run(ffn)

    @pl.when(i == nt + 1)
    def _():
        _run(_ffn_rows(ms, xs_ref, as_ref, g_ref, gf_ref, kvg_ref, tabs_s, os_ref, ks_ref, vs_ref, *common))


def _ffn(x, xs, g, wg, wu, wd, *, layer, tm, seq, attn=None, kv_g=None, wkv=None, tabs_p=None,
         tabs_s=None, final_g=None):
    mp, ms = x.shape[0], xs.shape[0]
    nt = mp // tm
    per_seq = seq // tm
    has_attn, has_kv, has_final = attn is not None, kv_g is not None, final_g is not None
    tile = lambda i: jnp.clip(i - 1, 0, nt - 1)
    nxt = lambda i: jnp.minimum(i, nt - 1)
    row = lambda w: pl.BlockSpec((tm, w), lambda i: (tile(i), 0))
    hbm = pl.BlockSpec(memory_space=pl.ANY)

    args, specs = [x, xs], [row(D_MODEL), _const_spec(xs.shape)]
    if has_attn:
        gq, wq, k, v, sinks, attn_s, wo = attn
        nrow = lambda w: pl.BlockSpec((tm, w), lambda i: (nxt(i), 0))
        args += [x, gq, wq, *tabs_p, k, v, sinks, attn_s, wo]
        specs += [nrow(D_MODEL), _const_spec(gq.shape), _const_spec(wq.shape)]
        specs += [pl.BlockSpec((tm, LANES), lambda i: (nxt(i) % per_seq, 0))] * 3
        specs += [nrow(KV_DIM), nrow(KV_DIM), pl.BlockSpec(memory_space=pltpu.SMEM),
                  _const_spec(attn_s.shape), _const_spec(wo.shape)]
    args += [g, wg, wu, wd]
    specs += [_const_spec(g.shape), hbm, hbm, hbm]
    if has_kv:
        args += [kv_g, wkv, *tabs_p, *tabs_s]
        specs += [_const_spec(kv_g.shape), _const_spec(wkv.shape)]
        specs += [pl.BlockSpec((tm, LANES), lambda i: (tile(i) % per_seq, 0))] * 3
        specs += [_const_spec((1, LANES))] * 3
    if has_final:
        args.append(final_g)
        specs.append(_const_spec(final_g.shape))

    out_shape = [jax.ShapeDtypeStruct((mp, D_MODEL), F32), jax.ShapeDtypeStruct((ms, D_MODEL), F32)]
    out_specs = [row(D_MODEL), _const_spec((ms, D_MODEL))]
    if has_kv:
        out_shape += [jax.ShapeDtypeStruct((mp, KV_DIM), F32)] * 2 + [jax.ShapeDtypeStruct((ms, KV_DIM), F32)] * 2
        out_specs += [row(KV_DIM)] * 2 + [_const_spec((ms, KV_DIM))] * 2

    scratch = [pltpu.VMEM((N_FF_CHUNKS, D_MODEL, 2 * FF_CHUNK), BF16),
               pltpu.VMEM((D_FF, D_MODEL), BF16),
               pltpu.VMEM((2, GU_ROWS, D_FF), F32),
               pltpu.VMEM((2, DN_ROWS, D_MODEL), F32),
               pltpu.SemaphoreType.DMA((2,)),
               pltpu.SemaphoreType.DMA((2,)),
               pltpu.VMEM((tm, D_MODEL), BF16),
               pltpu.VMEM((tm, D_FF), BF16)]
    if has_attn:
        scratch += [pltpu.VMEM((tm, D_MODEL), BF16), pltpu.VMEM((tm, D_MODEL), F32),
                    pltpu.VMEM((WINDOW + tm, KV_DIM), F32), pltpu.VMEM((WINDOW + tm, KV_DIM), F32)]
    if has_kv:
        scratch += [pltpu.VMEM((D_MODEL, 2 * KV_DIM), BF16)]

    return pl.pallas_call(
        functools.partial(_ffn_kernel, nt=nt, tm=tm, ms=ms, per_seq=per_seq, layer=layer, has_attn=has_attn,
                          has_kv=has_kv, has_final=has_final),
        grid=(nt + 2,),
        in_specs=specs,
        out_specs=out_specs,
        out_shape=out_shape,
        scratch_shapes=scratch,
        compiler_params=pltpu.CompilerParams(
            dimension_semantics=("arbitrary",), vmem_limit_bytes=VMEM_LIMIT),
        name="ffn",
    )(*args)


HALO = 16
assert POOL_WINDOWS == tuple(2 ** (i + 1) for i in range(len(POOL_WINDOWS))) and max(POOL_WINDOWS) <= HALO


def _pool_prompt_kernel(x_ref, g_ref, w_ref, sc_ref, o_ref, buf_ref, hext_ref, *, tp):
    t = pl.program_id(1)

    @pl.when(t == 0)
    def _():
        hext_ref[0:HALO, :] = jnp.zeros((HALO, D_MODEL), F32)

    x = x_ref[...]
    h = _rms(x, g_ref[...])
    hext_ref[HALO:HALO + tp, :] = h
    pos = (t * tp + lax.broadcasted_iota(jnp.int32, (tp, 1), 0)).astype(F32)
    sums, cur, shift = [], hext_ref[...], 1
    for gi in range(len(POOL_WINDOWS)):
        cur = cur + pltpu.roll(cur, shift, axis=0)
        sums.append(cur[HALO:, :POOL_GROUP])
        if gi + 1 < len(POOL_WINDOWS):
            cur, shift = cur[:, POOL_GROUP:], 2 * shift
    mixed = []
    for gi, w in enumerate(POOL_WINDOWS):
        cs = slice(gi * POOL_GROUP, (gi + 1) * POOL_GROUP)
        cnt = jnp.minimum(float(w), pos + 1.0)
        diff = (sums[gi] / cnt - h[:, cs]).astype(BF16)
        mixed.append(jnp.dot(diff, w_ref[gi].astype(BF16), preferred_element_type=F32))
    out = jnp.concatenate(mixed, axis=1) * sc_ref[...]
    o_ref[...] = x + out

    @pl.when(t == pl.num_programs(1) - 1)
    def _():
        buf_ref[0] = hext_ref[HALO + tp - POOL_BUF:HALO + tp, :]

    hext_ref[0:HALO, :] = hext_ref[tp:tp + HALO, :]


def _pool_prompt(x, g, w, sc, *, batch, tp):
    m = x.shape[0]
    nb = m // batch // tp
    blk = pl.BlockSpec((tp, D_MODEL), lambda i, j: (i * nb + j, 0))
    return pl.pallas_call(
        functools.partial(_pool_prompt_kernel, tp=tp),
        grid=(batch, nb),
        in_specs=[blk, _const_spec(g.shape), _const_spec(w.shape), _const_spec(sc.shape)],
        out_specs=[blk, pl.BlockSpec((1, POOL_BUF, D_MODEL), lambda i, j: (i, 0, 0))],
        out_shape=[jax.ShapeDtypeStruct(x.shape, F32),
                   jax.ShapeDtypeStruct((batch, POOL_BUF, D_MODEL), F32)],
        scratch_shapes=[pltpu.VMEM((HALO + tp, D_MODEL), F32)],
        compiler_params=pltpu.CompilerParams(
            dimension_semantics=("arbitrary", "arbitrary"), vmem_limit_bytes=VMEM_LIMIT),
        name="pool_prompt",
    )(x, g, w, sc)


def _pool_sample_kernel(x_ref, buf_ref, g_ref, w_ref, sc_ref, o_ref, nbuf_ref):
    x = x_ref[...]
    h = _rms(x, g_ref[...])
    nbuf_ref[0:POOL_BUF - 1] = buf_ref[1:POOL_BUF]
    nbuf_ref[POOL_BUF - 1] = h
    mixed = []
    for gi, w in enumerate(POOL_WINDOWS):
        cs = slice(gi * POOL_GROUP, (gi + 1) * POOL_GROUP)
        s = h[:, cs]
        for k in range(1, w):
            s = s + buf_ref[POOL_BUF - k, :, cs]
        diff = (s / float(w) - h[:, cs]).astype(BF16)
        mixed.append(jnp.dot(diff, w_ref[gi].astype(BF16), preferred_element_type=F32))
    o_ref[...] = x + jnp.concatenate(mixed, axis=1) * sc_ref[...]


def _pool_sample(x, buf_t, g, w, sc):
    m = x.shape[0]
    return pl.pallas_call(
        _pool_sample_kernel,
        out_shape=[jax.ShapeDtypeStruct((m, D_MODEL), F32), jax.ShapeDtypeStruct(buf_t.shape, F32)],
        compiler_params=pltpu.CompilerParams(vmem_limit_bytes=VMEM_LIMIT),
        name="pool_sample",
    )(x, buf_t, g, w, sc)


def _q_sample_kernel(x_ref, g_ref, w_ref, c_ref, s1_ref, s2_ref, o_ref):
    n = _rms(x_ref[...], g_ref[...]).astype(BF16)
    y = jnp.dot(n, w_ref[...].astype(BF16), preferred_element_type=F32)
    tabs = [jnp.broadcast_to(r[...], (y.shape[0], LANES)) for r in (c_ref, s1_ref, s2_ref)]
    o_ref[...] = _rope(y, *tabs) * (HEAD_DIM ** -0.5)


def _q_sample(x, g, w, tabs):
    return pl.pallas_call(
        _q_sample_kernel,
        out_shape=jax.ShapeDtypeStruct(x.shape, F32),
        compiler_params=pltpu.CompilerParams(vmem_limit_bytes=VMEM_LIMIT),
        name="q_sample",
    )(x, g, w, *tabs)


def _seg_mask(rows):
    lane = lax.broadcasted_iota(jnp.int32, (rows, KV_DIM), 1)
    return [(lane >= kv * HEAD_DIM) & (lane < (kv + 1) * HEAD_DIM) for kv in range(N_KV)]


def _attn_rows(nprev, x_ref, g_ref, wq_ref, tabs, k_ref, v_ref, sink_ref, a_ref, kbuf, vbuf, tq):
    hrow = lax.broadcasted_iota(jnp.int32, (WINDOW, KV_DIM), 0)
    kbuf[0:WINDOW, :] = jnp.where(hrow < nprev, kbuf[tq:tq + WINDOW, :], 0.0)
    vbuf[0:WINDOW, :] = jnp.where(hrow < nprev, vbuf[tq:tq + WINDOW, :], 0.0)
    kbuf[WINDOW:WINDOW + tq, :] = k_ref[...]
    vbuf[WINDOW:WINDOW + tq, :] = v_ref[...]

    hn = _rms(x_ref[...], g_ref[...]).astype(BF16)
    q = jnp.dot(hn, wq_ref[...], preferred_element_type=F32)
    q = (_rope(q, *[t[...] for t in tabs]) * (HEAD_DIM ** -0.5)).astype(BF16)
    yield

    seg2 = _seg_mask(2 * WINDOW)
    seg1 = _seg_mask(WINDOW)
    qi = lax.broadcasted_iota(jnp.int32, (WINDOW, WINDOW), 0)
    kj = lax.broadcasted_iota(jnp.int32, (WINDOW, WINDOW), 1)
    causal = kj <= qi
    prev_ok = kj <= qi + nprev

    for i in range(tq // WINDOW):
        kk = kbuf[i * WINDOW:(i + 2) * WINDOW, :]
        vv = vbuf[i * WINDOW:(i + 2) * WINDOW, :]
        kbd = jnp.concatenate([jnp.where(m, kk, 0.0).astype(BF16) for m in seg2], axis=0)
        vbd = jnp.concatenate([jnp.where(m, vv, 0.0).astype(BF16) for m in seg2], axis=0)
        rows = slice(i * WINDOW, (i + 1) * WINDOW)
        qs = jnp.concatenate([q[rows, gm * KV_DIM:(gm + 1) * KV_DIM] for gm in range(GROUP)], axis=0)
        s = lax.dot_general(qs, kbd, (((1,), (1,)), ((), ())), preferred_element_type=F32)
        yield
        pg, rg = [], []
        for gm in range(GROUP):
            ps, rinv = [], jnp.zeros((WINDOW, KV_DIM), F32)
            for kv in range(N_KV):
                s_prev = s[gm * WINDOW:(gm + 1) * WINDOW, kv * 2 * WINDOW:kv * 2 * WINDOW + WINDOW]
                s_own = s[gm * WINDOW:(gm + 1) * WINDOW, kv * 2 * WINDOW + WINDOW:(kv + 1) * 2 * WINDOW]
                sc = jnp.where(causal, s_own, s_prev)
                if i == 0:
                    sc = jnp.where(prev_ok, sc, -jnp.inf)
                sink = sink_ref[gm, kv]
                mx = jnp.maximum(jnp.max(sc, axis=-1, keepdims=True), sink)
                p = jnp.exp(sc - mx)
                den = jnp.sum(p, axis=-1, keepdims=True) + jnp.exp(sink - mx)
                ps.append(jnp.where(causal, 0.0, p).astype(BF16))
                ps.append(jnp.where(causal, p, 0.0).astype(BF16))
                rinv = jnp.where(seg1[kv], 1.0 / den, rinv)
            pg.append(jnp.concatenate(ps, axis=1))
            rg.append(rinv)
            yield
        og = jnp.dot(jnp.concatenate(pg, axis=0), vbd, preferred_element_type=F32)
        for gm in range(GROUP):
            a_ref[rows, gm * KV_DIM:(gm + 1) * KV_DIM] = (
                og[gm * WINDOW:(gm + 1) * WINDOW] * rg[gm]).astype(a_ref.dtype)
        yield


def _attn_sample_kernel(q_ref, kt_ref, vt_ref, knt_ref, vnt_ref, sink_ref, o_ref, ko_ref, vo_ref, *, bb):
    i = pl.program_id(0)
    lane = lax.broadcasted_iota(jnp.int32, (HEAD_DIM, WINDOW), 1)
    col = lax.broadcasted_iota(jnp.int32, (N_HEADS, N_KV * WINDOW), 1)
    row = lax.broadcasted_iota(jnp.int32, (N_HEADS, N_KV * WINDOW), 0)
    own = (col // WINDOW) == (row % N_KV)
    sink = sink_ref[...]
    for j in range(bb):
        shift = (WINDOW - 1) - (i * bb + j)
        kn = pltpu.roll(knt_ref[...], shift, axis=1)
        vn = pltpu.roll(vnt_ref[...], shift, axis=1)
        ks, vs = [], []
        for kv in range(N_KV):
            hs = slice(kv * HEAD_DIM, (kv + 1) * HEAD_DIM)
            k1 = jnp.where(lane == WINDOW - 1, kn[hs], pltpu.roll(kt_ref[j, kv], WINDOW - 1, axis=1))
            v1 = jnp.where(lane == WINDOW - 1, vn[hs], pltpu.roll(vt_ref[j, kv], WINDOW - 1, axis=1))
            ko_ref[j, kv] = k1
            vo_ref[j, kv] = v1
            ks.append(k1.astype(BF16))
            vs.append(v1.astype(BF16))
        kcat = jnp.concatenate(ks, axis=1)
        vcat = jnp.concatenate(vs, axis=1)
        s = jnp.dot(q_ref[j].astype(BF16), kcat, preferred_element_type=F32)
        s = jnp.where(own, s, -jnp.inf)
        mx = jnp.maximum(jnp.max(s, axis=-1, keepdims=True), sink)
        p = jnp.exp(s - mx)
        den = jnp.sum(p, axis=-1, keepdims=True) + jnp.exp(sink - mx)
        o = lax.dot_general(p.astype(BF16), vcat, (((1,), (1,)), ((), ())), preferred_element_type=F32)
        o_ref[j] = o / den


def _attn_sample(q3, kt, vt, knt, vnt, sinks, *, bb):
    b = q3.shape[0]
    blk4 = pl.BlockSpec((bb, N_KV, HEAD_DIM, WINDOW), lambda i: (i, 0, 0, 0))
    blk3 = pl.BlockSpec((bb, N_HEADS, HEAD_DIM), lambda i: (i, 0, 0))
    return pl.pallas_call(
        functools.partial(_attn_sample_kernel, bb=bb),
        grid=(b // bb,),
        in_specs=[blk3, blk4, blk4, _const_spec(knt.shape), _const_spec(vnt.shape), _const_spec(sinks.shape)],
        out_specs=[blk3, blk4, blk4],
        out_shape=[jax.ShapeDtypeStruct(q3.shape, F32), jax.ShapeDtypeStruct(kt.shape, F32),
                   jax.ShapeDtypeStruct(vt.shape, F32)],
        compiler_params=pltpu.CompilerParams(
            dimension_semantics=("arbitrary",), vmem_limit_bytes=VMEM_LIMIT),
        name="attn_sample",
    )(q3, kt, vt, knt, vnt, sinks)


def kernel(x_prompt, x_sample, state_pool, cache_k_win, cache_v_win, norm_g, ffn_w_gate, ffn_w_up,
           ffn_w_down, pool_w, pool_scale, kv_norm_g, w_kv, w_q, w_o, attn_sinks, final_norm_g):
    bp, seq, _ = x_prompt.shape
    bs = x_sample.shape[0]
    tm = 512
    g = lambda l, i: norm_g[l, i].reshape(1, D_MODEL)
    ffn_w = (ffn_w_gate.reshape(4, D_MODEL, D_FF), ffn_w_up.reshape(4, D_MODEL, D_FF),
             ffn_w_down.reshape(4, D_FF, D_MODEL))
    psc = pool_scale[0].reshape(1, D_MODEL)
    kvg = kv_norm_g.reshape(1, D_MODEL)
    wq = w_q[0].reshape(D_MODEL, N_KV, GROUP, HEAD_DIM).transpose(0, 2, 1, 3).reshape(D_MODEL, D_MODEL)
    wo = w_o[0].reshape(N_KV, GROUP, HEAD_DIM, D_MODEL).transpose(1, 0, 2, 3).reshape(D_MODEL, D_MODEL)
    sinks_gk = attn_sinks[0].reshape(N_KV, GROUP).T
    gfin = final_norm_g.reshape(1, D_MODEL)
    tabs_p = _rope_tables(np.arange(seq))
    tabs_s = _rope_tables(np.full((1,), PAST_LEN))

    x = x_prompt.reshape(bp * seq, D_MODEL)
    xs = x_sample.reshape(bs, D_MODEL)

    x, xs = _ffn(x, xs, g(0, 0), *ffn_w, layer=0, tm=tm, seq=seq)
    x, pool_p = _pool_prompt(x, g(0, 1), pool_w[0], psc, batch=bp, tp=512)
    buf_t = jnp.transpose(state_pool[0], (1, 0, 2))
    xs, nbuf_t = _pool_sample(xs, buf_t, g(0, 1), pool_w[0], psc)
    x, xs, k_p, v_p, k_s, v_s = _ffn(x, xs, g(0, 2), *ffn_w, layer=1, tm=tm, seq=seq,
                                     kv_g=kvg, wkv=w_kv, tabs_p=tabs_p, tabs_s=tabs_s)
    x, xs = _ffn(x, xs, g(1, 0), *ffn_w, layer=2, tm=tm, seq=seq)
    q_s = _q_sample(xs, g(1, 1), wq, tabs_s)
    kt = jnp.transpose(cache_k_win, (0, 2, 3, 1))
    vt = jnp.transpose(cache_v_win, (0, 2, 3, 1))
    o_s, kt_new, vt_new = _attn_sample(q_s.reshape(bs, N_HEADS, HEAD_DIM), kt, vt, k_s.T, v_s.T,
                                       sinks_gk.reshape(N_HEADS, 1), bb=8)
    attn_s = o_s.reshape(bs, D_MODEL).astype(BF16)
    y_p, y_s = _ffn(x, xs, g(1, 2), *ffn_w, layer=3, tm=tm, seq=seq, tabs_p=tabs_p, final_g=gfin,
                    attn=(g(1, 1), wq.astype(BF16), k_p, v_p, sinks_gk, attn_s, wo.astype(BF16)))

    y_prompt = y_p.reshape(bp, seq, D_MODEL)
    y_sample = y_s.reshape(bs, 1, D_MODEL)
    pool_prompt = pool_p[None]
    pool_sample = jnp.transpose(nbuf_t, (1, 0, 2))[None]
    k_win_prompt = k_p.reshape(bp, seq, KV_DIM)[:, seq - WINDOW:].reshape(bp, WINDOW, N_KV, HEAD_DIM)
    v_win_prompt = v_p.reshape(bp, seq, KV_DIM)[:, seq - WINDOW:].reshape(bp, WINDOW, N_KV, HEAD_DIM)
    k_win_sample = jnp.transpose(kt_new, (0, 3, 1, 2))
    v_win_sample = jnp.transpose(vt_new, (0, 3, 1, 2))
    return (y_prompt, y_sample, pool_prompt, pool_sample, k_win_prompt, v_win_prompt,
            k_win_sample, v_win_sample)
```

```python
import functools

import numpy as np
import jax
import jax.numpy as jnp
from jax import lax
from jax.experimental import pallas as pl
from jax.experimental.pallas import tpu as pltpu

F32 = jnp.float32
BF16 = jnp.bfloat16

D_MODEL = 1024
D_FF = 2816
HEAD_DIM = 64
N_HEADS = 16
N_KV = 4
GROUP = 4
KV_DIM = N_KV * HEAD_DIM
WINDOW = 128
ROT_DIM = 16
ROPE_THETA = 500000.0
EPS = 1e-5
POOL_WINDOWS = (2, 4, 8, 16)
POOL_GROUP = 256
POOL_BUF = 15
PAST_LEN = 16384

LANES = 128
FF_CHUNK = 256
N_FF_CHUNKS = D_FF // FF_CHUNK
OUT_CHUNK = 256
N_OUT_CHUNKS = D_MODEL // OUT_CHUNK
VMEM_LIMIT = 60 * 1024 * 1024

GU_SLABS = 16
GU_ROWS = D_MODEL // GU_SLABS
DN_SLABS = 16
DN_ROWS = D_FF // DN_SLABS
STAGE_SLOTS = 4

CAST_GU_ROWS = 32
CAST_DN_ROWS = 128


def _const_spec(shape):
    nd = len(shape)
    return pl.BlockSpec(shape, lambda *_: (0,) * nd, pipeline_mode=pl.Buffered(1))


def _rms(x, g):
    ms = jnp.mean(x * x, axis=-1, keepdims=True)
    return x * lax.rsqrt(ms + EPS) * g


def _rope(x, c, s1, s2):
    pieces = []
    for j in range(x.shape[1] // LANES):
        xc = x[:, j * LANES:(j + 1) * LANES]
        up = pltpu.roll(xc, LANES - ROT_DIM // 2, axis=1)
        dn = pltpu.roll(xc, ROT_DIM // 2, axis=1)
        pieces.append(xc * c + up * s1 + dn * s2)
    return jnp.concatenate(pieces, axis=1) if len(pieces) > 1 else pieces[0]


def _rope_tables(pos):
    half = ROT_DIM // 2
    inv = np.power(ROPE_THETA, -np.arange(half, dtype=np.float64) * (2.0 / ROT_DIM))
    ang = np.asarray(pos, np.float64)[:, None] * inv[None, :]
    cos, sin = np.cos(ang), np.sin(ang)
    n = ang.shape[0]
    ones = np.ones((n, HEAD_DIM - ROT_DIM))
    zeros8 = np.zeros((n, half))
    zeros = np.zeros((n, HEAD_DIM - ROT_DIM))
    c = np.concatenate([cos, cos, ones], axis=1)
    s1 = np.concatenate([-sin, zeros8, zeros], axis=1)
    s2 = np.concatenate([zeros8, sin, zeros], axis=1)
    return [jnp.asarray(np.concatenate([t, t], axis=1), F32) for t in (c, s1, s2)]


def _stage_ffn_weights(wg_hbm, wu_hbm, wd_hbm, wgu_scr, wd_scr, gu_stage, dn_stage, gu_sem, dn_sem):
    chunks = []
    for src, col0 in ((wg_hbm, 0), (wu_hbm, FF_CHUNK)):
        for r in range(GU_SLABS):
            chunks.append(("gu", src, r, col0))
    for r in range(DN_SLABS):
        chunks.append(("dn", wd_hbm, r, 0))

    def copy(c):
        kind, src, r, _ = chunks[c]
        slot = c % STAGE_SLOTS
        if kind == "gu":
            return pltpu.make_async_copy(src.at[pl.ds(r * GU_ROWS, GU_ROWS), :], gu_stage.at[slot], gu_sem.at[slot])
        return pltpu.make_async_copy(src.at[pl.ds(r * DN_ROWS, DN_ROWS), :], dn_stage.at[slot], dn_sem.at[slot])

    for c in range(STAGE_SLOTS):
        copy(c).start()
    for c, (kind, _, r, col0) in enumerate(chunks):
        slot = c % STAGE_SLOTS
        copy(c).wait()
        if kind == "gu":
            for j in range(N_FF_CHUNKS):
                wgu_scr[j, r * GU_ROWS:(r + 1) * GU_ROWS, col0:col0 + FF_CHUNK] = (
                    gu_stage[slot, :, j * FF_CHUNK:(j + 1) * FF_CHUNK].astype(BF16))
        else:
            wd_scr[r * DN_ROWS:(r + 1) * DN_ROWS, :] = dn_stage[slot].astype(BF16)
        if c + STAGE_SLOTS < len(chunks):
            copy(c + STAGE_SLOTS).start()


def _run(*phase_generators, steps=None):
    live = list(zip(phase_generators, steps or (1,) * len(phase_generators)))
    while live:
        for gen, n in list(live):
            try:
                for _ in range(n):
                    next(gen)
            except StopIteration:
                live.remove((gen, n))


def _ffn_rows(rows, x_ref, a_ref, g_ref, gf_ref, kvg_ref, tabs, o_ref, k_ref, v_ref,
              wgu_scr, wd_scr, wo_ref, wkv_scr, xn_ref, h_ref, xr_ref):
    rs = slice(0, rows)
    if a_ref is not None:
        xr_ref[rs, :] = x_ref[...] + jnp.dot(a_ref[...], wo_ref[...], preferred_element_type=F32)
        res = lambda sl: xr_ref[rs, sl]
    else:
        res = lambda sl: x_ref[:, sl]
    xn_ref[rs, :] = _rms(res(slice(None)), g_ref[...]).astype(BF16)
    yield
    for j in range(N_FF_CHUNKS):
        r = jnp.dot(xn_ref[rs, :], wgu_scr[j], preferred_element_type=F32)
        gate, up = r[:, :FF_CHUNK], r[:, FF_CHUNK:]
        h_ref[rs, j * FF_CHUNK:(j + 1) * FF_CHUNK] = (gate * jax.nn.sigmoid(gate) * up).astype(BF16)
        yield
    for n in range(N_OUT_CHUNKS):
        sl = slice(n * OUT_CHUNK, (n + 1) * OUT_CHUNK)
        y = jnp.dot(h_ref[rs, :], wd_scr[:, sl], preferred_element_type=F32)
        o_ref[:, sl] = res(sl) + 0.5 * y
        yield
    if kvg_ref is not None:
        kvn = _rms(o_ref[...], kvg_ref[...]).astype(BF16)
        kv = jnp.dot(kvn, wkv_scr[...], preferred_element_type=F32)
        tb = [t[...] for t in tabs]
        if tb[0].shape[0] != rows:
            tb = [jnp.broadcast_to(t, (rows, LANES)) for t in tb]
        k_ref[...] = _rope(kv[:, :KV_DIM], *tb)
        v_ref[...] = kv[:, KV_DIM:]
    if gf_ref is not None:
        o_ref[...] = _rms(o_ref[...], gf_ref[...])


def _cast_slabs(wg_in, wu_in, wd_in, wgu_out, wd_out):
    for j in range(N_FF_CHUNKS):
        cs = slice(j * FF_CHUNK, (j + 1) * FF_CHUNK)
        wgu_out[j, :, 0:FF_CHUNK] = wg_in[:, cs].astype(BF16)
        wgu_out[j, :, FF_CHUNK:2 * FF_CHUNK] = wu_in[:, cs].astype(BF16)
    wd_out[...] = wd_in[...].astype(BF16)
    yield


def _ffn_kernel(*refs, nt, tm, ms, per_seq, layer, n_cast, has_attn, has_kv, has_final):
    it = iter(refs)
    x_ref, xs_ref = next(it), next(it)
    a_ref = as_ref = wo_ref = gf_ref = kvg_ref = wkv_ref = None
    tabs_p = tabs_s = None
    if has_attn:
        xq_ref, gq_ref, wq_ref = next(it), next(it), next(it)
        tabs_q = [next(it) for _ in range(3)]
        kq_ref, vq_ref, sink_ref, as_ref, wo_ref = [next(it) for _ in range(5)]
    g_ref = next(it)
    if layer is not None:
        wg_hbm, wu_hbm, wd_hbm = next(it), next(it), next(it)
    else:
        wgu_hbm, wdn_hbm = next(it), next(it)
    cast_in = [[next(it) for _ in range(3)] for _ in range(n_cast)]
    if has_kv:
        kvg_ref, wkv_ref = next(it), next(it)
        tabs_p = [next(it) for _ in range(3)]
        tabs_s = [next(it) for _ in range(3)]
    if has_final:
        gf_ref = next(it)
    o_ref, os_ref = next(it), next(it)
    k_ref = v_ref = ks_ref = vs_ref = None
    if has_kv:
        k_ref, v_ref, ks_ref, vs_ref = next(it), next(it), next(it), next(it)
    cast_out = [[next(it) for _ in range(2)] for _ in range(n_cast)]
    wgu_scr, wd_scr = next(it), next(it)
    if layer is not None:
        gu_stage, dn_stage, gu_sem, dn_sem = [next(it) for _ in range(4)]
    else:
        w_sem = next(it)
    xn_ref, h_ref = next(it), next(it)
    xr_ref = wkv_scr = None
    if has_attn:
        a_ref, xr_ref, kbuf, vbuf = [next(it) for _ in range(4)]
    if has_kv:
        wkv_scr = next(it)

    i = pl.program_id(0)

    def attention(tile):
        nprev = jnp.where(tile % per_seq == 0, 0, WINDOW)
        return _attn_rows(nprev, xq_ref, gq_ref, wq_ref, tabs_q, kq_ref, vq_ref, sink_ref, a_ref,
                          kbuf, vbuf, tm)

    @pl.when(i == 0)
    def _():
        if layer is not None:
            _stage_ffn_weights(wg_hbm.at[layer], wu_hbm.at[layer], wd_hbm.at[layer], wgu_scr, wd_scr,
                               gu_stage, dn_stage, gu_sem, dn_sem)
        else:
            copies = [pltpu.make_async_copy(wgu_hbm, wgu_scr, w_sem.at[0]),
                      pltpu.make_async_copy(wdn_hbm, wd_scr, w_sem.at[1])]
            for c in copies:
                c.start()
        if has_attn:
            kbuf[...] = jnp.zeros(kbuf.shape, F32)
            vbuf[...] = jnp.zeros(vbuf.shape, F32)
            _run(attention(0))
        if has_kv:
            wkv_scr[...] = wkv_ref[...].astype(BF16)
        if layer is None:
            for c in copies:
                c.wait()

    common = (wgu_scr, wd_scr, wo_ref, wkv_scr, xn_ref, h_ref, xr_ref)

    @pl.when((i >= 1) & (i <= nt))
    def _():
        side = [_cast_slabs(*cast_in[c], *cast_out[c]) for c in range(n_cast)]
        if has_attn:
            side.append(attention(jnp.minimum(i, nt - 1)))
        _run(_ffn_rows(tm, x_ref, a_ref, g_ref, gf_ref, kvg_ref, tabs_p, o_ref, k_ref, v_ref, *common), *side)

    @pl.when(i == nt + 1)
    def _():
        _run(_ffn_rows(ms, xs_ref, as_ref, g_ref, gf_ref, kvg_ref, tabs_s, os_ref, ks_ref, vs_ref, *common))


def _ffn(x, xs, g, weights, *, tm, seq, cast_layers=(), attn=None, kv_g=None, wkv=None, tabs_p=None,
         tabs_s=None, final_g=None):
    mp, ms = x.shape[0], xs.shape[0]
    nt = mp // tm
    per_seq = seq // tm
    layer = weights[3] if len(weights) == 4 else None
    n_cast = len(cast_layers)
    assert n_cast == 0 or (nt * CAST_GU_ROWS == D_MODEL and nt * CAST_DN_ROWS >= D_FF)
    has_attn, has_kv, has_final = attn is not None, kv_g is not None, final_g is not None
    tile = lambda i: jnp.clip(i - 1, 0, nt - 1)
    nxt = lambda i: jnp.minimum(i, nt - 1)
    row = lambda w: pl.BlockSpec((tm, w), lambda i: (tile(i), 0))
    hbm = pl.BlockSpec(memory_space=pl.ANY)

    args, specs = [x, xs], [row(D_MODEL), _const_spec(xs.shape)]
    if has_attn:
        gq, wq, k, v, sinks, attn_s, wo = attn
        nrow = lambda w: pl.BlockSpec((tm, w), lambda i: (nxt(i), 0))
        args += [x, gq, wq, *tabs_p, k, v, sinks, attn_s, wo]
        specs += [nrow(D_MODEL), _const_spec(gq.shape), _const_spec(wq.shape)]
        specs += [pl.BlockSpec((tm, LANES), lambda i: (nxt(i) % per_seq, 0))] * 3
        specs += [nrow(KV_DIM), nrow(KV_DIM), pl.BlockSpec(memory_space=pltpu.SMEM),
                  _const_spec(attn_s.shape), _const_spec(wo.shape)]
    args += [g, *weights[:3]] if layer is not None else [g, *weights]
    specs += [_const_spec(g.shape)] + [hbm] * (3 if layer is not None else 2)
    dn_blocks = D_FF // CAST_DN_ROWS
    dn_tile = lambda i: jnp.minimum(tile(i), dn_blocks - 1)
    for cl in cast_layers:
        args += list(weights[:3])
        specs += [pl.BlockSpec((None, CAST_GU_ROWS, D_FF), lambda i, cl=cl: (cl, tile(i), 0)),
                  pl.BlockSpec((None, CAST_GU_ROWS, D_FF), lambda i, cl=cl: (cl, tile(i), 0)),
                  pl.BlockSpec((None, CAST_DN_ROWS, D_MODEL), lambda i, cl=cl: (cl, dn_tile(i), 0))]
    if has_kv:
        args += [kv_g, wkv, *tabs_p, *tabs_s]
        specs += [_const_spec(kv_g.shape), _const_spec(wkv.shape)]
        specs += [pl.BlockSpec((tm, LANES), lambda i: (tile(i) % per_seq, 0))] * 3
        specs += [_const_spec((1, LANES))] * 3
    if has_final:
        args.append(final_g)
        specs.append(_const_spec(final_g.shape))

    out_shape = [jax.ShapeDtypeStruct((mp, D_MODEL), F32), jax.ShapeDtypeStruct((ms, D_MODEL), F32)]
    out_specs = [row(D_MODEL), _const_spec((ms, D_MODEL))]
    if has_kv:
        out_shape += [jax.ShapeDtypeStruct((mp, KV_DIM), F32)] * 2 + [jax.ShapeDtypeStruct((ms, KV_DIM), F32)] * 2
        out_specs += [row(KV_DIM)] * 2 + [_const_spec((ms, KV_DIM))] * 2
    wgu_shape, wdn_shape = (N_FF_CHUNKS, D_MODEL, 2 * FF_CHUNK), (D_FF, D_MODEL)
    for _ in cast_layers:
        out_shape += [jax.ShapeDtypeStruct(wgu_shape, BF16), jax.ShapeDtypeStruct(wdn_shape, BF16)]
        out_specs += [pl.BlockSpec((N_FF_CHUNKS, CAST_GU_ROWS, 2 * FF_CHUNK), lambda i: (0, tile(i), 0)),
                      pl.BlockSpec((CAST_DN_ROWS, D_MODEL), lambda i: (dn_tile(i), 0))]

    scratch = [pltpu.VMEM(wgu_shape, BF16), pltpu.VMEM(wdn_shape, BF16)]
    if layer is not None:
        scratch += [pltpu.VMEM((STAGE_SLOTS, GU_ROWS, D_FF), F32),
                    pltpu.VMEM((STAGE_SLOTS, DN_ROWS, D_MODEL), F32),
                    pltpu.SemaphoreType.DMA((STAGE_SLOTS,)),
                    pltpu.SemaphoreType.DMA((STAGE_SLOTS,))]
    else:
        scratch += [pltpu.SemaphoreType.DMA((2,))]
    scratch += [pltpu.VMEM((tm, D_MODEL), BF16),
                pltpu.VMEM((tm, D_FF), BF16)]
    if has_attn:
        scratch += [pltpu.VMEM((tm, D_MODEL), BF16), pltpu.VMEM((tm, D_MODEL), F32),
                    pltpu.VMEM((WINDOW + tm, KV_DIM), F32), pltpu.VMEM((WINDOW + tm, KV_DIM), F32)]
    if has_kv:
        scratch += [pltpu.VMEM((D_MODEL, 2 * KV_DIM), BF16)]

    return pl.pallas_call(
        functools.partial(_ffn_kernel, nt=nt, tm=tm, ms=ms, per_seq=per_seq, layer=layer, n_cast=n_cast,
                          has_attn=has_attn, has_kv=has_kv, has_final=has_final),
        grid=(nt + 2,),
        in_specs=specs,
        out_specs=out_specs,
        out_shape=out_shape,
        scratch_shapes=scratch,
        compiler_params=pltpu.CompilerParams(
            dimension_semantics=("arbitrary",), vmem_limit_bytes=VMEM_LIMIT),
        name="ffn",
    )(*args)


HALO = 16
assert POOL_WINDOWS == tuple(2 ** (i + 1) for i in range(len(POOL_WINDOWS))) and max(POOL_WINDOWS) <= HALO


def _pool_prompt_kernel(x_ref, g_ref, w_ref, sc_ref, o_ref, buf_ref, hext_ref, *, tp):
    t = pl.program_id(1)

    @pl.when(t == 0)
    def _():
        hext_ref[0:HALO, :] = jnp.zeros((HALO, D_MODEL), F32)

    x = x_ref[...]
    h = _rms(x, g_ref[...])
    hext_ref[HALO:HALO + tp, :] = h
    pos = (t * tp + lax.broadcasted_iota(jnp.int32, (tp, 1), 0)).astype(F32)
    sums, cur, shift = [], hext_ref[...], 1
    for gi in range(len(POOL_WINDOWS)):
        cur = cur + pltpu.roll(cur, shift, axis=0)
        sums.append(cur[HALO:, :POOL_GROUP])
        if gi + 1 < len(POOL_WINDOWS):
            cur, shift = cur[:, POOL_GROUP:], 2 * shift
    mixed = []
    for gi, w in enumerate(POOL_WINDOWS):
        cs = slice(gi * POOL_GROUP, (gi + 1) * POOL_GROUP)
        cnt = jnp.minimum(float(w), pos + 1.0)
        diff = (sums[gi] / cnt - h[:, cs]).astype(BF16)
        mixed.append(jnp.dot(diff, w_ref[gi].astype(BF16), preferred_element_type=F32))
    out = jnp.concatenate(mixed, axis=1) * sc_ref[...]
    o_ref[...] = x + out

    @pl.when(t == pl.num_programs(1) - 1)
    def _():
        buf_ref[0] = hext_ref[HALO + tp - POOL_BUF:HALO + tp, :]

    hext_ref[0:HALO, :] = hext_ref[tp:tp + HALO, :]


def _pool_prompt(x, g, w, sc, *, batch, tp):
    m = x.shape[0]
    nb = m // batch // tp
    blk = pl.BlockSpec((tp, D_MODEL), lambda i, j: (i * nb + j, 0))
    return pl.pallas_call(
        functools.partial(_pool_prompt_kernel, tp=tp),
        grid=(batch, nb),
        in_specs=[blk, _const_spec(g.shape), _const_spec(w.shape), _const_spec(sc.shape)],
        out_specs=[blk, pl.BlockSpec((1, POOL_BUF, D_MODEL), lambda i, j: (i, 0, 0))],
        out_shape=[jax.ShapeDtypeStruct(x.shape, F32),
                   jax.ShapeDtypeStruct((batch, POOL_BUF, D_MODEL), F32)],
        scratch_shapes=[pltpu.VMEM((HALO + tp, D_MODEL), F32)],
        compiler_params=pltpu.CompilerParams(
            dimension_semantics=("arbitrary", "arbitrary"), vmem_limit_bytes=VMEM_LIMIT),
        name="pool_prompt",
    )(x, g, w, sc)


def _pool_sample_kernel(x_ref, buf_ref, g_ref, w_ref, sc_ref, o_ref, nbuf_ref):
    x = x_ref[...]
    h = _rms(x, g_ref[...])
    nbuf_ref[0:POOL_BUF - 1] = buf_ref[1:POOL_BUF]
    nbuf_ref[POOL_BUF - 1] = h
    mixed = []
    for gi, w in enumerate(POOL_WINDOWS):
        cs = slice(gi * POOL_GROUP, (gi + 1) * POOL_GROUP)
        s = h[:, cs]
        for k in range(1, w):
            s = s + buf_ref[POOL_BUF - k, :, cs]
        diff = (s / float(w) - h[:, cs]).astype(BF16)
        mixed.append(jnp.dot(diff, w_ref[gi].astype(BF16), preferred_element_type=F32))
    o_ref[...] = x + jnp.concatenate(mixed, axis=1) * sc_ref[...]


def _pool_sample(x, buf_t, g, w, sc):
    m = x.shape[0]
    return pl.pallas_call(
        _pool_sample_kernel,
        out_shape=[jax.ShapeDtypeStruct((m, D_MODEL), F32), jax.ShapeDtypeStruct(buf_t.shape, F32)],
        compiler_params=pltpu.CompilerParams(vmem_limit_bytes=VMEM_LIMIT),
        name="pool_sample",
    )(x, buf_t, g, w, sc)


def _q_sample_kernel(x_ref, g_ref, w_ref, c_ref, s1_ref, s2_ref, o_ref):
    n = _rms(x_ref[...], g_ref[...]).astype(BF16)
    y = jnp.dot(n, w_ref[...].astype(BF16), preferred_element_type=F32)
    tabs = [jnp.broadcast_to(r[...], (y.shape[0], LANES)) for r in (c_ref, s1_ref, s2_ref)]
    o_ref[...] = _rope(y, *tabs) * (HEAD_DIM ** -0.5)


def _q_sample(x, g, w, tabs):
    return pl.pallas_call(
        _q_sample_kernel,
        out_shape=jax.ShapeDtypeStruct(x.shape, F32),
        compiler_params=pltpu.CompilerParams(vmem_limit_bytes=VMEM_LIMIT),
        name="q_sample",
    )(x, g, w, *tabs)


def _seg_mask(rows):
    lane = lax.broadcasted_iota(jnp.int32, (rows, KV_DIM), 1)
    return [(lane >= kv * HEAD_DIM) & (lane < (kv + 1) * HEAD_DIM) for kv in range(N_KV)]


def _attn_rows(nprev, x_ref, g_ref, wq_ref, tabs, k_ref, v_ref, sink_ref, a_ref, kbuf, vbuf, tq):
    hrow = lax.broadcasted_iota(jnp.int32, (WINDOW, KV_DIM), 0)
    kbuf[0:WINDOW, :] = jnp.where(hrow < nprev, kbuf[tq:tq + WINDOW, :], 0.0)
    vbuf[0:WINDOW, :] = jnp.where(hrow < nprev, vbuf[tq:tq + WINDOW, :], 0.0)
    kbuf[WINDOW:WINDOW + tq, :] = k_ref[...]
    vbuf[WINDOW:WINDOW + tq, :] = v_ref[...]

    hn = _rms(x_ref[...], g_ref[...]).astype(BF16)
    q = jnp.dot(hn, wq_ref[...], preferred_element_type=F32)
    q = (_rope(q, *[t[...] for t in tabs]) * (HEAD_DIM ** -0.5)).astype(BF16)
    yield

    seg2 = _seg_mask(2 * WINDOW)
    seg1 = _seg_mask(WINDOW)
    qi = lax.broadcasted_iota(jnp.int32, (WINDOW, WINDOW), 0)
    kj = lax.broadcasted_iota(jnp.int32, (WINDOW, WINDOW), 1)
    causal = kj <= qi
    prev_ok = kj <= qi + nprev

    for i in range(tq // WINDOW):
        kk = kbuf[i * WINDOW:(i + 2) * WINDOW, :]
        vv = vbuf[i * WINDOW:(i + 2) * WINDOW, :]
        kbd = jnp.concatenate([jnp.where(m, kk, 0.0).astype(BF16) for m in seg2], axis=0)
        vbd = jnp.concatenate([jnp.where(m, vv, 0.0).astype(BF16) for m in seg2], axis=0)
        rows = slice(i * WINDOW, (i + 1) * WINDOW)
        qs = jnp.concatenate([q[rows, gm * KV_DIM:(gm + 1) * KV_DIM] for gm in range(GROUP)], axis=0)
        s = lax.dot_general(qs, kbd, (((1,), (1,)), ((), ())), preferred_element_type=F32)
        yield
        pg, rg = [], []
        for gm in range(GROUP):
            ps, rinv = [], jnp.zeros((WINDOW, KV_DIM), F32)
            for kv in range(N_KV):
                s_prev = s[gm * WINDOW:(gm + 1) * WINDOW, kv * 2 * WINDOW:kv * 2 * WINDOW + WINDOW]
                s_own = s[gm * WINDOW:(gm + 1) * WINDOW, kv * 2 * WINDOW + WINDOW:(kv + 1) * 2 * WINDOW]
                sc = jnp.where(causal, s_own, s_prev)
                if i == 0:
                    sc = jnp.where(prev_ok, sc, -jnp.inf)
                sink = sink_ref[gm, kv]
                mx = jnp.maximum(jnp.max(sc, axis=-1, keepdims=True), sink)
                p = jnp.exp(sc - mx)
                den = jnp.sum(p, axis=-1, keepdims=True) + jnp.exp(sink - mx)
                ps.append(jnp.where(causal, 0.0, p).astype(BF16))
                ps.append(jnp.where(causal, p, 0.0).astype(BF16))
                rinv = jnp.where(seg1[kv], 1.0 / den, rinv)
            pg.append(jnp.concatenate(ps, axis=1))
            rg.append(rinv)
            yield
        og = jnp.dot(jnp.concatenate(pg, axis=0), vbd, preferred_element_type=F32)
        for gm in range(GROUP):
            a_ref[rows, gm * KV_DIM:(gm + 1) * KV_DIM] = (
                og[gm * WINDOW:(gm + 1) * WINDOW] * rg[gm]).astype(a_ref.dtype)
        yield


def _attn_sample_kernel(q_ref, kt_ref, vt_ref, knt_ref, vnt_ref, sink_ref, o_ref, ko_ref, vo_ref, *, bb):
    i = pl.program_id(0)
    lane = lax.broadcasted_iota(jnp.int32, (HEAD_DIM, WINDOW), 1)
    col = lax.broadcasted_iota(jnp.int32, (N_HEADS, N_KV * WINDOW), 1)
    row = lax.broadcasted_iota(jnp.int32, (N_HEADS, N_KV * WINDOW), 0)
    own = (col // WINDOW) == (row % N_KV)
    sink = sink_ref[...]
    for j in range(bb):
        shift = (WINDOW - 1) - (i * bb + j)
        kn = pltpu.roll(knt_ref[...], shift, axis=1)
        vn = pltpu.roll(vnt_ref[...], shift, axis=1)
        ks, vs = [], []
        for kv in range(N_KV):
            hs = slice(kv * HEAD_DIM, (kv + 1) * HEAD_DIM)
            k1 = jnp.where(lane == WINDOW - 1, kn[hs], pltpu.roll(kt_ref[j, kv], WINDOW - 1, axis=1))
            v1 = jnp.where(lane == WINDOW - 1, vn[hs], pltpu.roll(vt_ref[j, kv], WINDOW - 1, axis=1))
            ko_ref[j, kv] = k1
            vo_ref[j, kv] = v1
            ks.append(k1.astype(BF16))
            vs.append(v1.astype(BF16))
        kcat = jnp.concatenate(ks, axis=1)
        vcat = jnp.concatenate(vs, axis=1)
        s = jnp.dot(q_ref[j].astype(BF16), kcat, preferred_element_type=F32)
        s = jnp.where(own, s, -jnp.inf)
        mx = jnp.maximum(jnp.max(s, axis=-1, keepdims=True), sink)
        p = jnp.exp(s - mx)
        den = jnp.sum(p, axis=-1, keepdims=True) + jnp.exp(sink - mx)
        o = lax.dot_general(p.astype(BF16), vcat, (((1,), (1,)), ((), ())), preferred_element_type=F32)
        o_ref[j] = o / den


def _attn_sample(q3, kt, vt, knt, vnt, sinks, *, bb):
    b = q3.shape[0]
    blk4 = pl.BlockSpec((bb, N_KV, HEAD_DIM, WINDOW), lambda i: (i, 0, 0, 0))
    blk3 = pl.BlockSpec((bb, N_HEADS, HEAD_DIM), lambda i: (i, 0, 0))
    return pl.pallas_call(
        functools.partial(_attn_sample_kernel, bb=bb),
        grid=(b // bb,),
        in_specs=[blk3, blk4, blk4, _const_spec(knt.shape), _const_spec(vnt.shape), _const_spec(sinks.shape)],
        out_specs=[blk3, blk4, blk4],
        out_shape=[jax.ShapeDtypeStruct(q3.shape, F32), jax.ShapeDtypeStruct(kt.shape, F32),
                   jax.ShapeDtypeStruct(vt.shape, F32)],
        compiler_params=pltpu.CompilerParams(
            dimension_semantics=("arbitrary",), vmem_limit_bytes=VMEM_LIMIT),
        name="attn_sample",
    )(q3, kt, vt, knt, vnt, sinks)


def kernel(x_prompt, x_sample, state_pool, cache_k_win, cache_v_win, norm_g, ffn_w_gate, ffn_w_up,
           ffn_w_down, pool_w, pool_scale, kv_norm_g, w_kv, w_q, w_o, attn_sinks, final_norm_g):
    bp, seq, _ = x_prompt.shape
    bs = x_sample.shape[0]
    tm = 512
    g = lambda l, i: norm_g[l, i].reshape(1, D_MODEL)
    ffn_w = (ffn_w_gate.reshape(4, D_MODEL, D_FF), ffn_w_up.reshape(4, D_MODEL, D_FF),
             ffn_w_down.reshape(4, D_FF, D_MODEL))
    psc = pool_scale[0].reshape(1, D_MODEL)
    kvg = kv_norm_g.reshape(1, D_MODEL)
    wq = w_q[0].reshape(D_MODEL, N_KV, GROUP, HEAD_DIM).transpose(0, 2, 1, 3).reshape(D_MODEL, D_MODEL)
    wo = w_o[0].reshape(N_KV, GROUP, HEAD_DIM, D_MODEL).transpose(1, 0, 2, 3).reshape(D_MODEL, D_MODEL)
    sinks_gk = attn_sinks[0].reshape(N_KV, GROUP).T
    gfin = final_norm_g.reshape(1, D_MODEL)
    tabs_p = _rope_tables(np.arange(seq))
    tabs_s = _rope_tables(np.full((1,), PAST_LEN))

    x = x_prompt.reshape(bp * seq, D_MODEL)
    xs = x_sample.reshape(bs, D_MODEL)

    x, xs, *w_bf16 = _ffn(x, xs, g(0, 0), (*ffn_w, 0), tm=tm, seq=seq, cast_layers=(1, 2, 3))
    w1, w2, w3 = w_bf16[0:2], w_bf16[2:4], w_bf16[4:6]
    x, pool_p = _pool_prompt(x, g(0, 1), pool_w[0], psc, batch=bp, tp=512)
    buf_t = jnp.transpose(state_pool[0], (1, 0, 2))
    xs, nbuf_t = _pool_sample(xs, buf_t, g(0, 1), pool_w[0], psc)
    x, xs, k_p, v_p, k_s, v_s = _ffn(x, xs, g(0, 2), w1, tm=tm, seq=seq,
                                     kv_g=kvg, wkv=w_kv, tabs_p=tabs_p, tabs_s=tabs_s)
    x, xs = _ffn(x, xs, g(1, 0), w2, tm=tm, seq=seq)
    q_s = _q_sample(xs, g(1, 1), wq, tabs_s)
    kt = jnp.transpose(cache_k_win, (0, 2, 3, 1))
    vt = jnp.transpose(cache_v_win, (0, 2, 3, 1))
    o_s, kt_new, vt_new = _attn_sample(q_s.reshape(bs, N_HEADS, HEAD_DIM), kt, vt, k_s.T, v_s.T,
                                       sinks_gk.reshape(N_HEADS, 1), bb=8)
    attn_s = o_s.reshape(bs, D_MODEL).astype(BF16)
    y_p, y_s = _ffn(x, xs, g(1, 2), w3, tm=tm, seq=seq, tabs_p=tabs_p, final_g=gfin,
                    attn=(g(1, 1), wq.astype(BF16), k_p, v_p, sinks_gk, attn_s, wo.astype(BF16)))

    y_prompt = y_p.reshape(bp, seq, D_MODEL)
    y_sample = y_s.reshape(bs, 1, D_MODEL)
    pool_prompt = pool_p[None]
    pool_sample = jnp.transpose(nbuf_t, (1, 0, 2))[None]
    k_win_prompt = k_p.reshape(bp, seq, KV_DIM)[:, seq - WINDOW:].reshape(bp, WINDOW, N_KV, HEAD_DIM)
    v_win_prompt = v_p.reshape(bp, seq, KV_DIM)[:, seq - WINDOW:].reshape(bp, WINDOW, N_KV, HEAD_DIM)
    k_win_sample = jnp.transpose(kt_new, (0, 3, 1, 2))
    v_win_sample = jnp.transpose(vt_new, (0, 3, 1, 2))
    return (y_prompt, y_sample, pool_prompt, pool_sample, k_win_prompt, v_win_prompt,
            k_win_sample, v_win_sample)
```

```python
import functools

import numpy as np
import jax
import jax.numpy as jnp
from jax import lax
from jax.experimental import pallas as pl
from jax.experimental.pallas import tpu as pltpu

F32 = jnp.float32
BF16 = jnp.bfloat16

D_MODEL = 1024
D_FF = 2816
HEAD_DIM = 64
N_HEADS = 16
N_KV = 4
GROUP = 4
KV_DIM = N_KV * HEAD_DIM
WINDOW = 128
ROT_DIM = 16
ROPE_THETA = 500000.0
EPS = 1e-5
POOL_WINDOWS = (2, 4, 8, 16)
POOL_GROUP = 256
POOL_BUF = 15
PAST_LEN = 16384

LANES = 128
FF_CHUNK = 256
N_FF_CHUNKS = D_FF // FF_CHUNK
OUT_CHUNK = 256
N_OUT_CHUNKS = D_MODEL // OUT_CHUNK
VMEM_LIMIT = 60 * 1024 * 1024

GU_SLABS = 8
GU_ROWS = D_MODEL // GU_SLABS
DN_SLABS = 8
DN_ROWS = D_FF // DN_SLABS
STAGE_SLOTS = 4

CAST_GU_ROWS = 32
CAST_DN_ROWS = 128


def _const_spec(shape):
    nd = len(shape)
    return pl.BlockSpec(shape, lambda *_: (0,) * nd, pipeline_mode=pl.Buffered(1))


def _rms(x, g):
    ms = jnp.mean(x * x, axis=-1, keepdims=True)
    return x * lax.rsqrt(ms + EPS) * g


def _rope(x, c, s1, s2):
    pieces = []
    for j in range(x.shape[1] // LANES):
        xc = x[:, j * LANES:(j + 1) * LANES]
        up = pltpu.roll(xc, LANES - ROT_DIM // 2, axis=1)
        dn = pltpu.roll(xc, ROT_DIM // 2, axis=1)
        pieces.append(xc * c + up * s1 + dn * s2)
    return jnp.concatenate(pieces, axis=1) if len(pieces) > 1 else pieces[0]


def _rope_tables(pos):
    half = ROT_DIM // 2
    inv = np.power(ROPE_THETA, -np.arange(half, dtype=np.float64) * (2.0 / ROT_DIM))
    ang = np.asarray(pos, np.float64)[:, None] * inv[None, :]
    cos, sin = np.cos(ang), np.sin(ang)
    n = ang.shape[0]
    ones = np.ones((n, HEAD_DIM - ROT_DIM))
    zeros8 = np.zeros((n, half))
    zeros = np.zeros((n, HEAD_DIM - ROT_DIM))
    c = np.concatenate([cos, cos, ones], axis=1)
    s1 = np.concatenate([-sin, zeros8, zeros], axis=1)
    s2 = np.concatenate([zeros8, sin, zeros], axis=1)
    return [jnp.asarray(np.concatenate([t, t], axis=1), F32) for t in (c, s1, s2)]


def _stage_ffn_weights(wg_hbm, wu_hbm, wd_hbm, wgu_scr, wd_scr, gu_stage, dn_stage, gu_sem, dn_sem):
    chunks = []
    for src, col0 in ((wg_hbm, 0), (wu_hbm, FF_CHUNK)):
        for r in range(GU_SLABS):
            chunks.append(("gu", src, r, col0))
    for r in range(DN_SLABS):
        chunks.append(("dn", wd_hbm, r, 0))

    def copy(c):
        kind, src, r, _ = chunks[c]
        slot = c % STAGE_SLOTS
        if kind == "gu":
            return pltpu.make_async_copy(src.at[pl.ds(r * GU_ROWS, GU_ROWS), :], gu_stage.at[slot], gu_sem.at[slot])
        return pltpu.make_async_copy(src.at[pl.ds(r * DN_ROWS, DN_ROWS), :], dn_stage.at[slot], dn_sem.at[slot])

    for c in range(STAGE_SLOTS):
        copy(c).start()
    for c, (kind, _, r, col0) in enumerate(chunks):
        slot = c % STAGE_SLOTS
        copy(c).wait()
        if kind == "gu":
            for j in range(N_FF_CHUNKS):
                wgu_scr[j, r * GU_ROWS:(r + 1) * GU_ROWS, col0:col0 + FF_CHUNK] = (
                    gu_stage[slot, :, j * FF_CHUNK:(j + 1) * FF_CHUNK].astype(BF16))
        else:
            wd_scr[r * DN_ROWS:(r + 1) * DN_ROWS, :] = dn_stage[slot].astype(BF16)
        if c + STAGE_SLOTS < len(chunks):
            copy(c + STAGE_SLOTS).start()


def _run(*phase_generators, steps=None):
    live = list(zip(phase_generators, steps or (1,) * len(phase_generators)))
    while live:
        for gen, n in list(live):
            try:
                for _ in range(n):
                    next(gen)
            except StopIteration:
                live.remove((gen, n))


def _ffn_rows(rows, x_ref, a_ref, g_ref, gf_ref, kvg_ref, tabs, o_ref, k_ref, v_ref,
              wgu_scr, wd_scr, wo_ref, wkv_scr, xn_ref, h_ref, xr_ref):
    rs = slice(0, rows)
    if a_ref is not None:
        xr_ref[rs, :] = x_ref[...] + jnp.dot(a_ref[...], wo_ref[...], preferred_element_type=F32)
    elif xr_ref is not None:
        xr_ref[rs, :] = x_ref[...]
    res = (lambda sl: xr_ref[rs, sl]) if xr_ref is not None else (lambda sl: x_ref[:, sl])
    xn_ref[rs, :] = _rms(res(slice(None)), g_ref[...]).astype(BF16)
    yield
    for j in range(N_FF_CHUNKS):
        r = jnp.dot(xn_ref[rs, :], wgu_scr[j], preferred_element_type=F32)
        gate, up = r[:, :FF_CHUNK], r[:, FF_CHUNK:]
        h_ref[rs, j * FF_CHUNK:(j + 1) * FF_CHUNK] = (gate * jax.nn.sigmoid(gate) * up).astype(BF16)
        yield
    for n in range(N_OUT_CHUNKS):
        sl = slice(n * OUT_CHUNK, (n + 1) * OUT_CHUNK)
        y = jnp.dot(h_ref[rs, :], wd_scr[:, sl], preferred_element_type=F32)
        o_ref[:, sl] = res(sl) + 0.5 * y
        yield
    if kvg_ref is not None:
        kvn = _rms(o_ref[...], kvg_ref[...]).astype(BF16)
        kv = jnp.dot(kvn, wkv_scr[...], preferred_element_type=F32)
        tb = [t[...] for t in tabs]
        if tb[0].shape[0] != rows:
            tb = [jnp.broadcast_to(t, (rows, LANES)) for t in tb]
        k_ref[...] = _rope(kv[:, :KV_DIM], *tb)
        v_ref[...] = kv[:, KV_DIM:]
    if gf_ref is not None:
        o_ref[...] = _rms(o_ref[...], gf_ref[...])


def _cast_slabs(wg_in, wu_in, wd_in, wgu_out, wd_out):
    for j in range(N_FF_CHUNKS):
        cs = slice(j * FF_CHUNK, (j + 1) * FF_CHUNK)
        wgu_out[j, :, 0:FF_CHUNK] = wg_in[:, cs].astype(BF16)
        wgu_out[j, :, FF_CHUNK:2 * FF_CHUNK] = wu_in[:, cs].astype(BF16)
    wd_out[...] = wd_in[...].astype(BF16)
    yield


def _ffn_kernel(*refs, nt, tm, ms, per_seq, layer, n_cast, has_pool, has_attn, has_kv, has_final):
    it = iter(refs)
    x_ref, xs_ref = next(it), next(it)
    a_ref = as_ref = wo_ref = gf_ref = kvg_ref = wkv_ref = None
    tabs_p = tabs_s = None
    if has_pool:
        gp_ref, pw_ref, psc_ref = next(it), next(it), next(it)
    if has_attn:
        xq_ref, gq_ref, wq_ref = next(it), next(it), next(it)
        tabs_q = [next(it) for _ in range(3)]
        kq_ref, vq_ref, sink_ref, as_ref, wo_ref = [next(it) for _ in range(5)]
    g_ref = next(it)
    if layer is not None:
        wg_hbm, wu_hbm, wd_hbm = next(it), next(it), next(it)
    else:
        wgu_hbm, wdn_hbm = next(it), next(it)
    cast_in = [[next(it) for _ in range(3)] for _ in range(n_cast)]
    if has_kv:
        kvg_ref, wkv_ref = next(it), next(it)
        tabs_p = [next(it) for _ in range(3)]
        tabs_s = [next(it) for _ in range(3)]
    if has_final:
        gf_ref = next(it)
    o_ref, os_ref = next(it), next(it)
    k_ref = v_ref = ks_ref = vs_ref = None
    if has_kv:
        k_ref, v_ref, ks_ref, vs_ref = next(it), next(it), next(it), next(it)
    cast_out = [[next(it) for _ in range(2)] for _ in range(n_cast)]
    if has_pool:
        pbuf_ref = next(it)
    wgu_scr, wd_scr = next(it), next(it)
    if layer is not None:
        gu_stage, dn_stage, gu_sem, dn_sem = [next(it) for _ in range(4)]
    else:
        w_sem = next(it)
    xn_ref, h_ref = next(it), next(it)
    xr_ref = wkv_scr = None
    if has_attn:
        a_ref, xr_ref, kbuf, vbuf = [next(it) for _ in range(4)]
    if has_pool:
        x2_scr, xr_ref, hext_ref = [next(it) for _ in range(3)]
    if has_kv:
        wkv_scr = next(it)

    i = pl.program_id(0)

    def pool(tile):
        return _pool_rows(tile % per_seq, x_ref, gp_ref, pw_ref, psc_ref, x2_scr, pbuf_ref, hext_ref, tm)

    def attention(tile):
        nprev = jnp.where(tile % per_seq == 0, 0, WINDOW)
        return _attn_rows(nprev, xq_ref, gq_ref, wq_ref, tabs_q, kq_ref, vq_ref, sink_ref, a_ref,
                          kbuf, vbuf, tm)

    @pl.when(i == 0)
    def _():
        if layer is not None:
            _stage_ffn_weights(wg_hbm.at[layer], wu_hbm.at[layer], wd_hbm.at[layer], wgu_scr, wd_scr,
                               gu_stage, dn_stage, gu_sem, dn_sem)
        else:
            copies = [pltpu.make_async_copy(wgu_hbm, wgu_scr, w_sem.at[0]),
                      pltpu.make_async_copy(wdn_hbm, wd_scr, w_sem.at[1])]
            for c in copies:
                c.start()
        if has_attn:
            kbuf[...] = jnp.zeros(kbuf.shape, F32)
            vbuf[...] = jnp.zeros(vbuf.shape, F32)
            _run(attention(0))
        if has_pool:
            hext_ref[...] = jnp.zeros(hext_ref.shape, F32)
            _run(pool(0))
        if has_kv:
            wkv_scr[...] = wkv_ref[...].astype(BF16)
        if layer is None:
            for c in copies:
                c.wait()

    common = (wgu_scr, wd_scr, wo_ref, wkv_scr, xn_ref, h_ref, xr_ref)

    @pl.when((i >= 1) & (i <= nt))
    def _():
        side = [_cast_slabs(*cast_in[c], *cast_out[c]) for c in range(n_cast)]
        if has_attn:
            side.append(attention(jnp.minimum(i, nt - 1)))
        if has_pool:
            side.append(pool(jnp.minimum(i, nt - 1)))
        xin = x2_scr if has_pool else x_ref
        _run(_ffn_rows(tm, xin, a_ref, g_ref, gf_ref, kvg_ref, tabs_p, o_ref, k_ref, v_ref, *common), *side)

    @pl.when(i == nt + 1)
    def _():
        _run(_ffn_rows(ms, xs_ref, as_ref, g_ref, gf_ref, kvg_ref, tabs_s, os_ref, ks_ref, vs_ref, *common))


def _ffn(x, xs, g, weights, *, tm, seq, cast_layers=(), pool=None, attn=None, kv_g=None, wkv=None,
         tabs_p=None, tabs_s=None, final_g=None):
    mp, ms = x.shape[0], xs.shape[0]
    nt = mp // tm
    per_seq = seq // tm
    layer = weights[3] if len(weights) == 4 else None
    n_cast = len(cast_layers)
    assert n_cast == 0 or (nt * CAST_GU_ROWS == D_MODEL and nt * CAST_DN_ROWS >= D_FF)
    has_attn, has_kv, has_final = attn is not None, kv_g is not None, final_g is not None
    has_pool = pool is not None
    tile = lambda i: jnp.clip(i - 1, 0, nt - 1)
    nxt = lambda i: jnp.minimum(i, nt - 1)
    row = lambda w: pl.BlockSpec((tm, w), lambda i: (tile(i), 0))
    nrow = lambda w: pl.BlockSpec((tm, w), lambda i: (nxt(i), 0))
    hbm = pl.BlockSpec(memory_space=pl.ANY)

    args, specs = [x, xs], [nrow(D_MODEL) if has_pool else row(D_MODEL), _const_spec(xs.shape)]
    if has_pool:
        args += list(pool)
        specs += [_const_spec(a.shape) for a in pool]
    if has_attn:
        gq, wq, k, v, sinks, attn_s, wo = attn
        args += [x, gq, wq, *tabs_p, k, v, sinks, attn_s, wo]
        specs += [nrow(D_MODEL), _const_spec(gq.shape), _const_spec(wq.shape)]
        specs += [pl.BlockSpec((tm, LANES), lambda i: (nxt(i) % per_seq, 0))] * 3
        specs += [nrow(KV_DIM), nrow(KV_DIM), pl.BlockSpec(memory_space=pltpu.SMEM),
                  _const_spec(attn_s.shape), _const_spec(wo.shape)]
    args += [g, *weights[:3]] if layer is not None else [g, *weights]
    specs += [_const_spec(g.shape)] + [hbm] * (3 if layer is not None else 2)
    dn_blocks = D_FF // CAST_DN_ROWS
    dn_tile = lambda i: jnp.minimum(tile(i), dn_blocks - 1)
    for cl in cast_layers:
        args += list(weights[:3])
        specs += [pl.BlockSpec((None, CAST_GU_ROWS, D_FF), lambda i, cl=cl: (cl, tile(i), 0)),
                  pl.BlockSpec((None, CAST_GU_ROWS, D_FF), lambda i, cl=cl: (cl, tile(i), 0)),
                  pl.BlockSpec((None, CAST_DN_ROWS, D_MODEL), lambda i, cl=cl: (cl, dn_tile(i), 0))]
    if has_kv:
        args += [kv_g, wkv, *tabs_p, *tabs_s]
        specs += [_const_spec(kv_g.shape), _const_spec(wkv.shape)]
        specs += [pl.BlockSpec((tm, LANES), lambda i: (tile(i) % per_seq, 0))] * 3
        specs += [_const_spec((1, LANES))] * 3
    if has_final:
        args.append(final_g)
        specs.append(_const_spec(final_g.shape))

    out_shape = [jax.ShapeDtypeStruct((mp, D_MODEL), F32), jax.ShapeDtypeStruct((ms, D_MODEL), F32)]
    out_specs = [row(D_MODEL), _const_spec((ms, D_MODEL))]
    if has_kv:
        out_shape += [jax.ShapeDtypeStruct((mp, KV_DIM), F32)] * 2 + [jax.ShapeDtypeStruct((ms, KV_DIM), F32)] * 2
        out_specs += [row(KV_DIM)] * 2 + [_const_spec((ms, KV_DIM))] * 2
    wgu_shape, wdn_shape = (N_FF_CHUNKS, D_MODEL, 2 * FF_CHUNK), (D_FF, D_MODEL)
    for _ in cast_layers:
        out_shape += [jax.ShapeDtypeStruct(wgu_shape, BF16), jax.ShapeDtypeStruct(wdn_shape, BF16)]
        out_specs += [pl.BlockSpec((N_FF_CHUNKS, CAST_GU_ROWS, 2 * FF_CHUNK), lambda i: (0, tile(i), 0)),
                      pl.BlockSpec((CAST_DN_ROWS, D_MODEL), lambda i: (dn_tile(i), 0))]
    if has_pool:
        out_shape += [jax.ShapeDtypeStruct((mp // seq, POOL_BUF, D_MODEL), F32)]
        out_specs += [pl.BlockSpec((1, POOL_BUF, D_MODEL), lambda i: (nxt(i) // per_seq, 0, 0))]

    scratch = [pltpu.VMEM(wgu_shape, BF16), pltpu.VMEM(wdn_shape, BF16)]
    if layer is not None:
        scratch += [pltpu.VMEM((STAGE_SLOTS, GU_ROWS, D_FF), F32),
                    pltpu.VMEM((STAGE_SLOTS, DN_ROWS, D_MODEL), F32),
                    pltpu.SemaphoreType.DMA((STAGE_SLOTS,)),
                    pltpu.SemaphoreType.DMA((STAGE_SLOTS,))]
    else:
        scratch += [pltpu.SemaphoreType.DMA((2,))]
    scratch += [pltpu.VMEM((tm, D_MODEL), BF16),
                pltpu.VMEM((tm, D_FF), BF16)]
    if has_attn:
        scratch += [pltpu.VMEM((tm, D_MODEL), BF16), pltpu.VMEM((tm, D_MODEL), F32),
                    pltpu.VMEM((WINDOW + tm, KV_DIM), F32), pltpu.VMEM((WINDOW + tm, KV_DIM), F32)]
    if has_pool:
        scratch += [pltpu.VMEM((tm, D_MODEL), F32), pltpu.VMEM((tm, D_MODEL), F32),
                    pltpu.VMEM((HALO + tm, D_MODEL), F32)]
    if has_kv:
        scratch += [pltpu.VMEM((D_MODEL, 2 * KV_DIM), BF16)]

    return pl.pallas_call(
        functools.partial(_ffn_kernel, nt=nt, tm=tm, ms=ms, per_seq=per_seq, layer=layer, n_cast=n_cast,
                          has_pool=has_pool, has_attn=has_attn, has_kv=has_kv, has_final=has_final),
        grid=(nt + 2,),
        in_specs=specs,
        out_specs=out_specs,
        out_shape=out_shape,
        scratch_shapes=scratch,
        compiler_params=pltpu.CompilerParams(
            dimension_semantics=("arbitrary",), vmem_limit_bytes=VMEM_LIMIT),
        name="ffn",
    )(*args)


HALO = 16
assert POOL_WINDOWS == tuple(2 ** (i + 1) for i in range(len(POOL_WINDOWS))) and max(POOL_WINDOWS) <= HALO


def _pool_rows(t_seq, x_ref, g_ref, w_ref, sc_ref, o_ref, buf_ref, hext_ref, tp):
    keep = jnp.where(t_seq == 0, 0, HALO)
    hrow = lax.broadcasted_iota(jnp.int32, (HALO, D_MODEL), 0)
    hext_ref[0:HALO, :] = jnp.where(hrow < keep, hext_ref[tp:tp + HALO, :], 0.0)
    x = x_ref[...]
    h = _rms(x, g_ref[...])
    hext_ref[HALO:HALO + tp, :] = h
    buf_ref[0] = hext_ref[HALO + tp - POOL_BUF:HALO + tp, :]
    yield
    pos = (t_seq * tp + lax.broadcasted_iota(jnp.int32, (tp, 1), 0)).astype(F32)
    sums, cur, shift = [], hext_ref[...], 1
    for gi in range(len(POOL_WINDOWS)):
        cur = cur + pltpu.roll(cur, shift, axis=0)
        sums.append(cur[HALO:, :POOL_GROUP])
        if gi + 1 < len(POOL_WINDOWS):
            cur, shift = cur[:, POOL_GROUP:], 2 * shift
        yield
    for gi, w in enumerate(POOL_WINDOWS):
        cs = slice(gi * POOL_GROUP, (gi + 1) * POOL_GROUP)
        cnt = jnp.minimum(float(w), pos + 1.0)
        diff = (sums[gi] / cnt - hext_ref[HALO:HALO + tp, cs]).astype(BF16)
        mixed = jnp.dot(diff, w_ref[gi].astype(BF16), preferred_element_type=F32)
        o_ref[:, cs] = x_ref[:, cs] + mixed * sc_ref[:, cs]
        yield


def _pool_sample_kernel(x_ref, buf_ref, g_ref, w_ref, sc_ref, o_ref, nbuf_ref):
    x = x_ref[...]
    h = _rms(x, g_ref[...])
    nbuf_ref[0:POOL_BUF - 1] = buf_ref[1:POOL_BUF]
    nbuf_ref[POOL_BUF - 1] = h
    mixed = []
    for gi, w in enumerate(POOL_WINDOWS):
        cs = slice(gi * POOL_GROUP, (gi + 1) * POOL_GROUP)
        s = h[:, cs]
        for k in range(1, w):
            s = s + buf_ref[POOL_BUF - k, :, cs]
        diff = (s / float(w) - h[:, cs]).astype(BF16)
        mixed.append(jnp.dot(diff, w_ref[gi].astype(BF16), preferred_element_type=F32))
    o_ref[...] = x + jnp.concatenate(mixed, axis=1) * sc_ref[...]


def _pool_sample(x, buf_t, g, w, sc):
    m = x.shape[0]
    return pl.pallas_call(
        _pool_sample_kernel,
        out_shape=[jax.ShapeDtypeStruct((m, D_MODEL), F32), jax.ShapeDtypeStruct(buf_t.shape, F32)],
        compiler_params=pltpu.CompilerParams(vmem_limit_bytes=VMEM_LIMIT),
        name="pool_sample",
    )(x, buf_t, g, w, sc)


def _q_sample_kernel(x_ref, g_ref, w_ref, c_ref, s1_ref, s2_ref, o_ref):
    n = _rms(x_ref[...], g_ref[...]).astype(BF16)
    y = jnp.dot(n, w_ref[...].astype(BF16), preferred_element_type=F32)
    tabs = [jnp.broadcast_to(r[...], (y.shape[0], LANES)) for r in (c_ref, s1_ref, s2_ref)]
    o_ref[...] = _rope(y, *tabs) * (HEAD_DIM ** -0.5)


def _q_sample(x, g, w, tabs):
    return pl.pallas_call(
        _q_sample_kernel,
        out_shape=jax.ShapeDtypeStruct(x.shape, F32),
        compiler_params=pltpu.CompilerParams(vmem_limit_bytes=VMEM_LIMIT),
        name="q_sample",
    )(x, g, w, *tabs)


def _seg_mask(rows):
    lane = lax.broadcasted_iota(jnp.int32, (rows, KV_DIM), 1)
    return [(lane >= kv * HEAD_DIM) & (lane < (kv + 1) * HEAD_DIM) for kv in range(N_KV)]


def _attn_rows(nprev, x_ref, g_ref, wq_ref, tabs, k_ref, v_ref, sink_ref, a_ref, kbuf, vbuf, tq):
    hrow = lax.broadcasted_iota(jnp.int32, (WINDOW, KV_DIM), 0)
    kbuf[0:WINDOW, :] = jnp.where(hrow < nprev, kbuf[tq:tq + WINDOW, :], 0.0)
    vbuf[0:WINDOW, :] = jnp.where(hrow < nprev, vbuf[tq:tq + WINDOW, :], 0.0)
    kbuf[WINDOW:WINDOW + tq, :] = k_ref[...]
    vbuf[WINDOW:WINDOW + tq, :] = v_ref[...]

    hn = _rms(x_ref[...], g_ref[...]).astype(BF16)
    q = jnp.dot(hn, wq_ref[...], preferred_element_type=F32)
    q = (_rope(q, *[t[...] for t in tabs]) * (HEAD_DIM ** -0.5)).astype(BF16)
    yield

    seg2 = _seg_mask(2 * WINDOW)
    seg1 = _seg_mask(WINDOW)
    qi = lax.broadcasted_iota(jnp.int32, (WINDOW, WINDOW), 0)
    kj = lax.broadcasted_iota(jnp.int32, (WINDOW, WINDOW), 1)
    causal = kj <= qi
    prev_ok = kj <= qi + nprev

    for i in range(tq // WINDOW):
        kk = kbuf[i * WINDOW:(i + 2) * WINDOW, :]
        vv = vbuf[i * WINDOW:(i + 2) * WINDOW, :]
        kbd = jnp.concatenate([jnp.where(m, kk, 0.0).astype(BF16) for m in seg2], axis=0)
        vbd = jnp.concatenate([jnp.where(m, vv, 0.0).astype(BF16) for m in seg2], axis=0)
        rows = slice(i * WINDOW, (i + 1) * WINDOW)
        qs = jnp.concatenate([q[rows, gm * KV_DIM:(gm + 1) * KV_DIM] for gm in range(GROUP)], axis=0)
        s = lax.dot_general(qs, kbd, (((1,), (1,)), ((), ())), preferred_element_type=F32)
        yield
        pg, rg = [], []
        for gm in range(GROUP):
            ps, rinv = [], jnp.zeros((WINDOW, KV_DIM), F32)
            for kv in range(N_KV):
                s_prev = s[gm * WINDOW:(gm + 1) * WINDOW, kv * 2 * WINDOW:kv * 2 * WINDOW + WINDOW]
                s_own = s[gm * WINDOW:(gm + 1) * WINDOW, kv * 2 * WINDOW + WINDOW:(kv + 1) * 2 * WINDOW]
                sc = jnp.where(causal, s_own, s_prev)
                if i == 0:
                    sc = jnp.where(prev_ok, sc, -jnp.inf)
                sink = sink_ref[gm, kv]
                mx = jnp.maximum(jnp.max(sc, axis=-1, keepdims=True), sink)
                p = jnp.exp(sc - mx)
                den = jnp.sum(p, axis=-1, keepdims=True) + jnp.exp(sink - mx)
                ps.append(jnp.where(causal, 0.0, p).astype(BF16))
                ps.append(jnp.where(causal, p, 0.0).astype(BF16))
                rinv = jnp.where(seg1[kv], 1.0 / den, rinv)
            pg.append(jnp.concatenate(ps, axis=1))
            rg.append(rinv)
            yield
        og = jnp.dot(jnp.concatenate(pg, axis=0), vbd, preferred_element_type=F32)
        for gm in range(GROUP):
            a_ref[rows, gm * KV_DIM:(gm + 1) * KV_DIM] = (
                og[gm * WINDOW:(gm + 1) * WINDOW] * rg[gm]).astype(a_ref.dtype)
        yield


def _attn_sample_kernel(q_ref, kt_ref, vt_ref, knt_ref, vnt_ref, sink_ref, o_ref, ko_ref, vo_ref, *, bb):
    i = pl.program_id(0)
    lane = lax.broadcasted_iota(jnp.int32, (HEAD_DIM, WINDOW), 1)
    col = lax.broadcasted_iota(jnp.int32, (N_HEADS, N_KV * WINDOW), 1)
    row = lax.broadcasted_iota(jnp.int32, (N_HEADS, N_KV * WINDOW), 0)
    own = (col // WINDOW) == (row % N_KV)
    sink = sink_ref[...]
    for j in range(bb):
        shift = (WINDOW - 1) - (i * bb + j)
        kn = pltpu.roll(knt_ref[...], shift, axis=1)
        vn = pltpu.roll(vnt_ref[...], shift, axis=1)
        ks, vs = [], []
        for kv in range(N_KV):
            hs = slice(kv * HEAD_DIM, (kv + 1) * HEAD_DIM)
            k1 = jnp.where(lane == WINDOW - 1, kn[hs], pltpu.roll(kt_ref[j, kv], WINDOW - 1, axis=1))
            v1 = jnp.where(lane == WINDOW - 1, vn[hs], pltpu.roll(vt_ref[j, kv], WINDOW - 1, axis=1))
            ko_ref[j, kv] = k1
            vo_ref[j, kv] = v1
            ks.append(k1.astype(BF16))
            vs.append(v1.astype(BF16))
        kcat = jnp.concatenate(ks, axis=1)
        vcat = jnp.concatenate(vs, axis=1)
        s = jnp.dot(q_ref[j].astype(BF16), kcat, preferred_element_type=F32)
        s = jnp.where(own, s, -jnp.inf)
        mx = jnp.maximum(jnp.max(s, axis=-1, keepdims=True), sink)
        p = jnp.exp(s - mx)
        den = jnp.sum(p, axis=-1, keepdims=True) + jnp.exp(sink - mx)
        o = lax.dot_general(p.astype(BF16), vcat, (((1,), (1,)), ((), ())), preferred_element_type=F32)
        o_ref[j] = o / den


def _attn_sample(q3, kt, vt, knt, vnt, sinks, *, bb):
    b = q3.shape[0]
    blk4 = pl.BlockSpec((bb, N_KV, HEAD_DIM, WINDOW), lambda i: (i, 0, 0, 0))
    blk3 = pl.BlockSpec((bb, N_HEADS, HEAD_DIM), lambda i: (i, 0, 0))
    return pl.pallas_call(
        functools.partial(_attn_sample_kernel, bb=bb),
        grid=(b // bb,),
        in_specs=[blk3, blk4, blk4, _const_spec(knt.shape), _const_spec(vnt.shape), _const_spec(sinks.shape)],
        out_specs=[blk3, blk4, blk4],
        out_shape=[jax.ShapeDtypeStruct(q3.shape, F32), jax.ShapeDtypeStruct(kt.shape, F32),
                   jax.ShapeDtypeStruct(vt.shape, F32)],
        compiler_params=pltpu.CompilerParams(
            dimension_semantics=("arbitrary",), vmem_limit_bytes=VMEM_LIMIT),
        name="attn_sample",
    )(q3, kt, vt, knt, vnt, sinks)


def kernel(x_prompt, x_sample, state_pool, cache_k_win, cache_v_win, norm_g, ffn_w_gate, ffn_w_up,
           ffn_w_down, pool_w, pool_scale, kv_norm_g, w_kv, w_q, w_o, attn_sinks, final_norm_g):
    bp, seq, _ = x_prompt.shape
    bs = x_sample.shape[0]
    tm = 512
    g = lambda l, i: norm_g[l, i].reshape(1, D_MODEL)
    ffn_w = (ffn_w_gate.reshape(4, D_MODEL, D_FF), ffn_w_up.reshape(4, D_MODEL, D_FF),
             ffn_w_down.reshape(4, D_FF, D_MODEL))
    psc = pool_scale[0].reshape(1, D_MODEL)
    kvg = kv_norm_g.reshape(1, D_MODEL)
    wq = w_q[0].reshape(D_MODEL, N_KV, GROUP, HEAD_DIM).transpose(0, 2, 1, 3).reshape(D_MODEL, D_MODEL)
    wo = w_o[0].reshape(N_KV, GROUP, HEAD_DIM, D_MODEL).transpose(1, 0, 2, 3).reshape(D_MODEL, D_MODEL)
    sinks_gk = attn_sinks[0].reshape(N_KV, GROUP).T
    gfin = final_norm_g.reshape(1, D_MODEL)
    tabs_p = _rope_tables(np.arange(seq))
    tabs_s = _rope_tables(np.full((1,), PAST_LEN))

    x = x_prompt.reshape(bp * seq, D_MODEL)
    xs = x_sample.reshape(bs, D_MODEL)

    x, xs, *w_bf16 = _ffn(x, xs, g(0, 0), (*ffn_w, 0), tm=tm, seq=seq, cast_layers=(1, 2, 3))
    w1, w2, w3 = w_bf16[0:2], w_bf16[2:4], w_bf16[4:6]
    buf_t = jnp.transpose(state_pool[0], (1, 0, 2))
    xs, nbuf_t = _pool_sample(xs, buf_t, g(0, 1), pool_w[0], psc)
    x, xs, k_p, v_p, k_s, v_s, pool_p = _ffn(x, xs, g(0, 2), w1, tm=tm, seq=seq,
                                             pool=(g(0, 1), pool_w[0], psc),
                                             kv_g=kvg, wkv=w_kv, tabs_p=tabs_p, tabs_s=tabs_s)
    x, xs = _ffn(x, xs, g(1, 0), w2, tm=tm, seq=seq)
    q_s = _q_sample(xs, g(1, 1), wq, tabs_s)
    kt = jnp.transpose(cache_k_win, (0, 2, 3, 1))
    vt = jnp.transpose(cache_v_win, (0, 2, 3, 1))
    o_s, kt_new, vt_new = _attn_sample(q_s.reshape(bs, N_HEADS, HEAD_DIM), kt, vt, k_s.T, v_s.T,
                                       sinks_gk.reshape(N_HEADS, 1), bb=8)
    attn_s = o_s.reshape(bs, D_MODEL).astype(BF16)
    y_p, y_s = _ffn(x, xs, g(1, 2), w3, tm=tm, seq=seq, tabs_p=tabs_p, final_g=gfin,
                    attn=(g(1, 1), wq.astype(BF16), k_p, v_p, sinks_gk, attn_s, wo.astype(BF16)))

    y_prompt = y_p.reshape(bp, seq, D_MODEL)
    y_sample = y_s.reshape(bs, 1, D_MODEL)
    pool_prompt = pool_p[None]
    pool_sample = jnp.transpose(nbuf_t, (1, 0, 2))[None]
    k_win_prompt = k_p.reshape(bp, seq, KV_DIM)[:, seq - WINDOW:].reshape(bp, WINDOW, N_KV, HEAD_DIM)
    v_win_prompt = v_p.reshape(bp, seq, KV_DIM)[:, seq - WINDOW:].reshape(bp, WINDOW, N_KV, HEAD_DIM)
    k_win_sample = jnp.transpose(kt_new, (0, 3, 1, 2))
    v_win_sample = jnp.transpose(vt_new, (0, 3, 1, 2))
    return (y_prompt, y_sample, pool_prompt, pool_sample, k_win_prompt, v_win_prompt,
            k_win_sample, v_win_sample)
```

```python
import functools

import numpy as np
import jax
import jax.numpy as jnp
from jax import lax
from jax.experimental import pallas as pl
from jax.experimental.pallas import tpu as pltpu

F32 = jnp.float32
BF16 = jnp.bfloat16

D_MODEL = 1024
D_FF = 2816
HEAD_DIM = 64
N_HEADS = 16
N_KV = 4
GROUP = 4
KV_DIM = N_KV * HEAD_DIM
WINDOW = 128
ROT_DIM = 16
ROPE_THETA = 500000.0
EPS = 1e-5
POOL_WINDOWS = (2, 4, 8, 16)
POOL_GROUP = 256
POOL_BUF = 15
PAST_LEN = 16384

LANES = 128
FF_CHUNK = 256
N_FF_CHUNKS = D_FF // FF_CHUNK
OUT_CHUNK = 256
N_OUT_CHUNKS = D_MODEL // OUT_CHUNK
VMEM_LIMIT = 60 * 1024 * 1024

GU_SLABS = 8
GU_ROWS = D_MODEL // GU_SLABS
DN_SLABS = 8
DN_ROWS = D_FF // DN_SLABS
STAGE_SLOTS = 4

CAST_GU_ROWS = 32
CAST_DN_ROWS = 128


def _const_spec(shape):
    nd = len(shape)
    return pl.BlockSpec(shape, lambda *_: (0,) * nd, pipeline_mode=pl.Buffered(1))


def _rms(x, g):
    ms = jnp.mean(x * x, axis=-1, keepdims=True)
    return x * lax.rsqrt(ms + EPS) * g


def _rope(x, c, s1, s2):
    pieces = []
    for j in range(x.shape[1] // LANES):
        xc = x[:, j * LANES:(j + 1) * LANES]
        up = pltpu.roll(xc, LANES - ROT_DIM // 2, axis=1)
        dn = pltpu.roll(xc, ROT_DIM // 2, axis=1)
        pieces.append(xc * c + up * s1 + dn * s2)
    return jnp.concatenate(pieces, axis=1) if len(pieces) > 1 else pieces[0]


def _rope_tables(pos):
    half = ROT_DIM // 2
    inv = np.power(ROPE_THETA, -np.arange(half, dtype=np.float64) * (2.0 / ROT_DIM))
    ang = np.asarray(pos, np.float64)[:, None] * inv[None, :]
    cos, sin = np.cos(ang), np.sin(ang)
    n = ang.shape[0]
    ones = np.ones((n, HEAD_DIM - ROT_DIM))
    zeros8 = np.zeros((n, half))
    zeros = np.zeros((n, HEAD_DIM - ROT_DIM))
    c = np.concatenate([cos, cos, ones], axis=1)
    s1 = np.concatenate([-sin, zeros8, zeros], axis=1)
    s2 = np.concatenate([zeros8, sin, zeros], axis=1)
    return [jnp.asarray(np.concatenate([t, t], axis=1), F32) for t in (c, s1, s2)]


def _stage_ffn_weights(wg_hbm, wu_hbm, wd_hbm, wgu_scr, wd_scr, gu_stage, dn_stage, gu_sem, dn_sem):
    chunks = []
    for src, col0 in ((wg_hbm, 0), (wu_hbm, FF_CHUNK)):
        for r in range(GU_SLABS):
            chunks.append(("gu", src, r, col0))
    for r in range(DN_SLABS):
        chunks.append(("dn", wd_hbm, r, 0))

    def copy(c):
        kind, src, r, _ = chunks[c]
        slot = c % STAGE_SLOTS
        if kind == "gu":
            return pltpu.make_async_copy(src.at[pl.ds(r * GU_ROWS, GU_ROWS), :], gu_stage.at[slot], gu_sem.at[slot])
        return pltpu.make_async_copy(src.at[pl.ds(r * DN_ROWS, DN_ROWS), :], dn_stage.at[slot], dn_sem.at[slot])

    for c in range(STAGE_SLOTS):
        copy(c).start()
    for c, (kind, _, r, col0) in enumerate(chunks):
        slot = c % STAGE_SLOTS
        copy(c).wait()
        if kind == "gu":
            for j in range(N_FF_CHUNKS):
                wgu_scr[j, r * GU_ROWS:(r + 1) * GU_ROWS, col0:col0 + FF_CHUNK] = (
                    gu_stage[slot, :, j * FF_CHUNK:(j + 1) * FF_CHUNK].astype(BF16))
        else:
            wd_scr[r * DN_ROWS:(r + 1) * DN_ROWS, :] = dn_stage[slot].astype(BF16)
        if c + STAGE_SLOTS < len(chunks):
            copy(c + STAGE_SLOTS).start()


def _run(*phase_generators, steps=None):
    live = list(zip(phase_generators, steps or (1,) * len(phase_generators)))
    while live:
        for gen, n in list(live):
            try:
                for _ in range(n):
                    next(gen)
            except StopIteration:
                live.remove((gen, n))


def _kv_rows(rows, src_ref, kvg_ref, wkv_scr, tabs, k_ref, v_ref):
    kvn = _rms(src_ref[0:rows, :], kvg_ref[...]).astype(BF16)
    yield
    kv = jnp.dot(kvn, wkv_scr[...], preferred_element_type=F32)
    yield
    tb = [t[...] for t in tabs]
    if tb[0].shape[0] != rows:
        tb = [jnp.broadcast_to(t, (rows, LANES)) for t in tb]
    k_ref[...] = _rope(kv[:, :KV_DIM], *tb)
    v_ref[...] = kv[:, KV_DIM:]
    yield


def _ffn_rows(rows, x_ref, a_ref, g_ref, gf_ref, kvg_ref, tabs, o_ref, k_ref, v_ref,
              wgu_scr, wd_scr, wo_ref, wkv_scr, xn_ref, h_ref, xr_ref, keep_ref=None):
    rs = slice(0, rows)
    if a_ref is not None:
        xr_ref[rs, :] = x_ref[...] + jnp.dot(a_ref[...], wo_ref[...], preferred_element_type=F32)
    elif xr_ref is not None:
        xr_ref[rs, :] = x_ref[...]
    res = (lambda sl: xr_ref[rs, sl]) if xr_ref is not None else (lambda sl: x_ref[:, sl])
    xn_ref[rs, :] = _rms(res(slice(None)), g_ref[...]).astype(BF16)
    yield
    for j in range(N_FF_CHUNKS):
        r = jnp.dot(xn_ref[rs, :], wgu_scr[j], preferred_element_type=F32)
        gate, up = r[:, :FF_CHUNK], r[:, FF_CHUNK:]
        h_ref[rs, j * FF_CHUNK:(j + 1) * FF_CHUNK] = (gate * jax.nn.sigmoid(gate) * up).astype(BF16)
        yield
    for n in range(N_OUT_CHUNKS):
        sl = slice(n * OUT_CHUNK, (n + 1) * OUT_CHUNK)
        y = jnp.dot(h_ref[rs, :], wd_scr[:, sl], preferred_element_type=F32)
        out = res(sl) + 0.5 * y
        o_ref[:, sl] = out
        if keep_ref is not None:
            keep_ref[rs, sl] = out
        yield
    if kvg_ref is not None and keep_ref is None:
        yield from _kv_rows(rows, o_ref, kvg_ref, wkv_scr, tabs, k_ref, v_ref)
    if gf_ref is not None:
        o_ref[...] = _rms(o_ref[...], gf_ref[...])


def _cast_slabs(wg_in, wu_in, wd_in, wgu_out, wd_out):
    for j in range(N_FF_CHUNKS):
        cs = slice(j * FF_CHUNK, (j + 1) * FF_CHUNK)
        wgu_out[j, :, 0:FF_CHUNK] = wg_in[:, cs].astype(BF16)
        wgu_out[j, :, FF_CHUNK:2 * FF_CHUNK] = wu_in[:, cs].astype(BF16)
    wd_out[...] = wd_in[...].astype(BF16)
    yield


def _ffn_kernel(*refs, nt, tm, ms, per_seq, layer, n_cast, has_pool, has_attn, has_kv, has_final):
    it = iter(refs)
    x_ref, xs_ref = next(it), next(it)
    a_ref = as_ref = wo_ref = gf_ref = kvg_ref = wkv_ref = None
    tabs_p = tabs_s = None
    if has_pool:
        gp_ref, pw_ref, psc_ref = next(it), next(it), next(it)
    if has_attn:
        xq_ref, gq_ref, wq_ref = next(it), next(it), next(it)
        tabs_q = [next(it) for _ in range(3)]
        kq_ref, vq_ref, sink_ref, as_ref, wo_ref = [next(it) for _ in range(5)]
    g_ref = next(it)
    if layer is not None:
        wg_hbm, wu_hbm, wd_hbm = next(it), next(it), next(it)
    else:
        wgu_hbm, wdn_hbm = next(it), next(it)
    cast_in = [[next(it) for _ in range(3)] for _ in range(n_cast)]
    if has_kv:
        kvg_ref, wkv_ref = next(it), next(it)
        tabs_p = [next(it) for _ in range(3)]
        tabs_s = [next(it) for _ in range(3)]
    if has_final:
        gf_ref = next(it)
    o_ref, os_ref = next(it), next(it)
    k_ref = v_ref = ks_ref = vs_ref = None
    if has_kv:
        k_ref, v_ref, ks_ref, vs_ref = next(it), next(it), next(it), next(it)
    cast_out = [[next(it) for _ in range(2)] for _ in range(n_cast)]
    if has_pool:
        pbuf_ref = next(it)
    wgu_scr, wd_scr = next(it), next(it)
    if layer is not None:
        gu_stage, dn_stage, gu_sem, dn_sem = [next(it) for _ in range(4)]
    else:
        w_sem = next(it)
    xn_ref, h_ref = next(it), next(it)
    xr_ref = wkv_scr = None
    if has_attn:
        a_ref, xr_ref, kbuf, vbuf = [next(it) for _ in range(4)]
    if has_pool:
        x2_scr, xr_ref, hext_ref = [next(it) for _ in range(3)]
    xkeep = None
    if has_kv:
        wkv_scr, xkeep = next(it), next(it)

    i = pl.program_id(0)

    def pool(tile):
        return _pool_rows(tile % per_seq, x_ref, gp_ref, pw_ref, psc_ref, x2_scr, pbuf_ref, hext_ref, tm)

    def attention(tile):
        nprev = jnp.where(tile % per_seq == 0, 0, WINDOW)
        return _attn_rows(nprev, xq_ref, gq_ref, wq_ref, tabs_q, kq_ref, vq_ref, sink_ref, a_ref,
                          kbuf, vbuf, tm)

    @pl.when(i == 0)
    def _():
        if layer is not None:
            _stage_ffn_weights(wg_hbm.at[layer], wu_hbm.at[layer], wd_hbm.at[layer], wgu_scr, wd_scr,
                               gu_stage, dn_stage, gu_sem, dn_sem)
        else:
            copies = [pltpu.make_async_copy(wgu_hbm, wgu_scr, w_sem.at[0]),
                      pltpu.make_async_copy(wdn_hbm, wd_scr, w_sem.at[1])]
            for c in copies:
                c.start()
        if has_attn:
            kbuf[...] = jnp.zeros(kbuf.shape, F32)
            vbuf[...] = jnp.zeros(vbuf.shape, F32)
            _run(attention(0))
        if has_pool:
            hext_ref[...] = jnp.zeros(hext_ref.shape, F32)
            _run(pool(0))
        if has_kv:
            wkv_scr[...] = wkv_ref[...].astype(BF16)
            xkeep[...] = jnp.zeros(xkeep.shape, F32)
        if layer is None:
            for c in copies:
                c.wait()

    common = (wgu_scr, wd_scr, wo_ref, wkv_scr, xn_ref, h_ref, xr_ref)

    @pl.when((i >= 1) & (i <= nt))
    def _():
        side = [_cast_slabs(*cast_in[c], *cast_out[c]) for c in range(n_cast)]
        if has_kv:
            side.append(_kv_rows(tm, xkeep, kvg_ref, wkv_scr, tabs_p, k_ref, v_ref))
        if has_attn:
            side.append(attention(jnp.minimum(i, nt - 1)))
        if has_pool:
            side.append(pool(jnp.minimum(i, nt - 1)))
        xin = x2_scr if has_pool else x_ref
        _run(_ffn_rows(tm, xin, a_ref, g_ref, gf_ref, kvg_ref, tabs_p, o_ref, k_ref, v_ref, *common,
                       keep_ref=xkeep), *side)

    @pl.when(i == nt + 1)
    def _():
        side = [_kv_rows(tm, xkeep, kvg_ref, wkv_scr, tabs_p, k_ref, v_ref)] if has_kv else []
        _run(_ffn_rows(ms, xs_ref, as_ref, g_ref, gf_ref, kvg_ref, tabs_s, os_ref, ks_ref, vs_ref, *common),
             *side)


def _ffn(x, xs, g, weights, *, tm, seq, cast_layers=(), pool=None, attn=None, kv_g=None, wkv=None,
         tabs_p=None, tabs_s=None, final_g=None):
    mp, ms = x.shape[0], xs.shape[0]
    nt = mp // tm
    per_seq = seq // tm
    layer = weights[3] if len(weights) == 4 else None
    n_cast = len(cast_layers)
    assert n_cast == 0 or (nt * CAST_GU_ROWS == D_MODEL and nt * CAST_DN_ROWS >= D_FF)
    has_attn, has_kv, has_final = attn is not None, kv_g is not None, final_g is not None
    has_pool = pool is not None
    tile = lambda i: jnp.clip(i - 1, 0, nt - 1)
    nxt = lambda i: jnp.minimum(i, nt - 1)
    prev = lambda i: jnp.clip(i - 2, 0, nt - 1)
    row = lambda w: pl.BlockSpec((tm, w), lambda i: (tile(i), 0))
    nrow = lambda w: pl.BlockSpec((tm, w), lambda i: (nxt(i), 0))
    hbm = pl.BlockSpec(memory_space=pl.ANY)

    args, specs = [x, xs], [nrow(D_MODEL) if has_pool else row(D_MODEL), _const_spec(xs.shape)]
    if has_pool:
        args += list(pool)
        specs += [_const_spec(a.shape) for a in pool]
    if has_attn:
        gq, wq, k, v, sinks, attn_s, wo = attn
        args += [x, gq, wq, *tabs_p, k, v, sinks, attn_s, wo]
        specs += [nrow(D_MODEL), _const_spec(gq.shape), _const_spec(wq.shape)]
        specs += [pl.BlockSpec((tm, LANES), lambda i: (nxt(i) % per_seq, 0))] * 3
        specs += [nrow(KV_DIM), nrow(KV_DIM), pl.BlockSpec(memory_space=pltpu.SMEM),
                  _const_spec(attn_s.shape), _const_spec(wo.shape)]
    args += [g, *weights[:3]] if layer is not None else [g, *weights]
    specs += [_const_spec(g.shape)] + [hbm] * (3 if layer is not None else 2)
    dn_blocks = D_FF // CAST_DN_ROWS
    dn_tile = lambda i: jnp.minimum(tile(i), dn_blocks - 1)
    for cl in cast_layers:
        args += list(weights[:3])
        specs += [pl.BlockSpec((None, CAST_GU_ROWS, D_FF), lambda i, cl=cl: (cl, tile(i), 0)),
                  pl.BlockSpec((None, CAST_GU_ROWS, D_FF), lambda i, cl=cl: (cl, tile(i), 0)),
                  pl.BlockSpec((None, CAST_DN_ROWS, D_MODEL), lambda i, cl=cl: (cl, dn_tile(i), 0))]
    if has_kv:
        args += [kv_g, wkv, *tabs_p, *tabs_s]
        specs += [_const_spec(kv_g.shape), _const_spec(wkv.shape)]
        specs += [pl.BlockSpec((tm, LANES), lambda i: (prev(i) % per_seq, 0))] * 3
        specs += [_const_spec((1, LANES))] * 3
    if has_final:
        args.append(final_g)
        specs.append(_const_spec(final_g.shape))

    out_shape = [jax.ShapeDtypeStruct((mp, D_MODEL), F32), jax.ShapeDtypeStruct((ms, D_MODEL), F32)]
    out_specs = [row(D_MODEL), _const_spec((ms, D_MODEL))]
    if has_kv:
        out_shape += [jax.ShapeDtypeStruct((mp, KV_DIM), F32)] * 2 + [jax.ShapeDtypeStruct((ms, KV_DIM), F32)] * 2
        out_specs += [pl.BlockSpec((tm, KV_DIM), lambda i: (prev(i), 0))] * 2 + [_const_spec((ms, KV_DIM))] * 2
    wgu_shape, wdn_shape = (N_FF_CHUNKS, D_MODEL, 2 * FF_CHUNK), (D_FF, D_MODEL)
    for _ in cast_layers:
        out_shape += [jax.ShapeDtypeStruct(wgu_shape, BF16), jax.ShapeDtypeStruct(wdn_shape, BF16)]
        out_specs += [pl.BlockSpec((N_FF_CHUNKS, CAST_GU_ROWS, 2 * FF_CHUNK), lambda i: (0, tile(i), 0)),
                      pl.BlockSpec((CAST_DN_ROWS, D_MODEL), lambda i: (dn_tile(i), 0))]
    if has_pool:
        out_shape += [jax.ShapeDtypeStruct((mp // seq, POOL_BUF, D_MODEL), F32)]
        out_specs += [pl.BlockSpec((1, POOL_BUF, D_MODEL), lambda i: (nxt(i) // per_seq, 0, 0))]

    scratch = [pltpu.VMEM(wgu_shape, BF16), pltpu.VMEM(wdn_shape, BF16)]
    if layer is not None:
        scratch += [pltpu.VMEM((STAGE_SLOTS, GU_ROWS, D_FF), F32),
                    pltpu.VMEM((STAGE_SLOTS, DN_ROWS, D_MODEL), F32),
                    pltpu.SemaphoreType.DMA((STAGE_SLOTS,)),
                    pltpu.SemaphoreType.DMA((STAGE_SLOTS,))]
    else:
        scratch += [pltpu.SemaphoreType.DMA((2,))]
    scratch += [pltpu.VMEM((tm, D_MODEL), BF16),
                pltpu.VMEM((tm, D_FF), BF16)]
    if has_attn:
        scratch += [pltpu.VMEM((tm, D_MODEL), BF16), pltpu.VMEM((tm, D_MODEL), F32),
                    pltpu.VMEM((WINDOW + tm, KV_DIM), F32), pltpu.VMEM((WINDOW + tm, KV_DIM), F32)]
    if has_pool:
        scratch += [pltpu.VMEM((tm, D_MODEL), F32), pltpu.VMEM((tm, D_MODEL), F32),
                    pltpu.VMEM((HALO + tm, D_MODEL), F32)]
    if has_kv:
        scratch += [pltpu.VMEM((D_MODEL, 2 * KV_DIM), BF16), pltpu.VMEM((tm, D_MODEL), F32)]

    return pl.pallas_call(
        functools.partial(_ffn_kernel, nt=nt, tm=tm, ms=ms, per_seq=per_seq, layer=layer, n_cast=n_cast,
                          has_pool=has_pool, has_attn=has_attn, has_kv=has_kv, has_final=has_final),
        grid=(nt + 2,),
        in_specs=specs,
        out_specs=out_specs,
        out_shape=out_shape,
        scratch_shapes=scratch,
        compiler_params=pltpu.CompilerParams(
            dimension_semantics=("arbitrary",), vmem_limit_bytes=VMEM_LIMIT),
        name="ffn",
    )(*args)


HALO = 16
assert POOL_WINDOWS == tuple(2 ** (i + 1) for i in range(len(POOL_WINDOWS))) and max(POOL_WINDOWS) <= HALO


def _pool_rows(t_seq, x_ref, g_ref, w_ref, sc_ref, o_ref, buf_ref, hext_ref, tp):
    keep = jnp.where(t_seq == 0, 0, HALO)
    hrow = lax.broadcasted_iota(jnp.int32, (HALO, D_MODEL), 0)
    hext_ref[0:HALO, :] = jnp.where(hrow < keep, hext_ref[tp:tp + HALO, :], 0.0)
    x = x_ref[...]
    h = _rms(x, g_ref[...])
    hext_ref[HALO:HALO + tp, :] = h
    buf_ref[0] = hext_ref[HALO + tp - POOL_BUF:HALO + tp, :]
    yield
    pos = (t_seq * tp + lax.broadcasted_iota(jnp.int32, (tp, 1), 0)).astype(F32)
    sums, cur, shift = [], hext_ref[...], 1
    for gi in range(len(POOL_WINDOWS)):
        cur = cur + pltpu.roll(cur, shift, axis=0)
        sums.append(cur[HALO:, :POOL_GROUP])
        if gi + 1 < len(POOL_WINDOWS):
            cur, shift = cur[:, POOL_GROUP:], 2 * shift
        yield
    for gi, w in enumerate(POOL_WINDOWS):
        cs = slice(gi * POOL_GROUP, (gi + 1) * POOL_GROUP)
        cnt = jnp.minimum(float(w), pos + 1.0)
        diff = (sums[gi] / cnt - hext_ref[HALO:HALO + tp, cs]).astype(BF16)
        mixed = jnp.dot(diff, w_ref[gi].astype(BF16), preferred_element_type=F32)
        o_ref[:, cs] = x_ref[:, cs] + mixed * sc_ref[:, cs]
        yield


def _pool_sample_kernel(x_ref, buf_ref, g_ref, w_ref, sc_ref, o_ref, nbuf_ref):
    x = x_ref[...]
    h = _rms(x, g_ref[...])
    nbuf_ref[0:POOL_BUF - 1] = buf_ref[1:POOL_BUF]
    nbuf_ref[POOL_BUF - 1] = h
    mixed = []
    for gi, w in enumerate(POOL_WINDOWS):
        cs = slice(gi * POOL_GROUP, (gi + 1) * POOL_GROUP)
        s = h[:, cs]
        for k in range(1, w):
            s = s + buf_ref[POOL_BUF - k, :, cs]
        diff = (s / float(w) - h[:, cs]).astype(BF16)
        mixed.append(jnp.dot(diff, w_ref[gi].astype(BF16), preferred_element_type=F32))
    o_ref[...] = x + jnp.concatenate(mixed, axis=1) * sc_ref[...]


def _pool_sample(x, buf_t, g, w, sc):
    m = x.shape[0]
    return pl.pallas_call(
        _pool_sample_kernel,
        out_shape=[jax.ShapeDtypeStruct((m, D_MODEL), F32), jax.ShapeDtypeStruct(buf_t.shape, F32)],
        compiler_params=pltpu.CompilerParams(vmem_limit_bytes=VMEM_LIMIT),
        name="pool_sample",
    )(x, buf_t, g, w, sc)


def _q_sample_kernel(x_ref, g_ref, w_ref, c_ref, s1_ref, s2_ref, o_ref):
    n = _rms(x_ref[...], g_ref[...]).astype(BF16)
    y = jnp.dot(n, w_ref[...].astype(BF16), preferred_element_type=F32)
    tabs = [jnp.broadcast_to(r[...], (y.shape[0], LANES)) for r in (c_ref, s1_ref, s2_ref)]
    o_ref[...] = _rope(y, *tabs) * (HEAD_DIM ** -0.5)


def _q_sample(x, g, w, tabs):
    return pl.pallas_call(
        _q_sample_kernel,
        out_shape=jax.ShapeDtypeStruct(x.shape, F32),
        compiler_params=pltpu.CompilerParams(vmem_limit_bytes=VMEM_LIMIT),
        name="q_sample",
    )(x, g, w, *tabs)


def _seg_mask(rows):
    lane = lax.broadcasted_iota(jnp.int32, (rows, KV_DIM), 1)
    return [(lane >= kv * HEAD_DIM) & (lane < (kv + 1) * HEAD_DIM) for kv in range(N_KV)]


def _attn_rows(nprev, x_ref, g_ref, wq_ref, tabs, k_ref, v_ref, sink_ref, a_ref, kbuf, vbuf, tq):
    hrow = lax.broadcasted_iota(jnp.int32, (WINDOW, KV_DIM), 0)
    kbuf[0:WINDOW, :] = jnp.where(hrow < nprev, kbuf[tq:tq + WINDOW, :], 0.0)
    vbuf[0:WINDOW, :] = jnp.where(hrow < nprev, vbuf[tq:tq + WINDOW, :], 0.0)
    kbuf[WINDOW:WINDOW + tq, :] = k_ref[...]
    vbuf[WINDOW:WINDOW + tq, :] = v_ref[...]

    hn = _rms(x_ref[...], g_ref[...]).astype(BF16)
    q = jnp.dot(hn, wq_ref[...], preferred_element_type=F32)
    q = (_rope(q, *[t[...] for t in tabs]) * (HEAD_DIM ** -0.5)).astype(BF16)
    yield

    seg2 = _seg_mask(2 * WINDOW)
    seg1 = _seg_mask(WINDOW)
    qi = lax.broadcasted_iota(jnp.int32, (WINDOW, WINDOW), 0)
    kj = lax.broadcasted_iota(jnp.int32, (WINDOW, WINDOW), 1)
    causal = kj <= qi
    prev_ok = kj <= qi + nprev

    for i in range(tq // WINDOW):
        kk = kbuf[i * WINDOW:(i + 2) * WINDOW, :]
        vv = vbuf[i * WINDOW:(i + 2) * WINDOW, :]
        kbd = jnp.concatenate([jnp.where(m, kk, 0.0).astype(BF16) for m in seg2], axis=0)
        vbd = jnp.concatenate([jnp.where(m, vv, 0.0).astype(BF16) for m in seg2], axis=0)
        rows = slice(i * WINDOW, (i + 1) * WINDOW)
        qs = jnp.concatenate([q[rows, gm * KV_DIM:(gm + 1) * KV_DIM] for gm in range(GROUP)], axis=0)
        s = lax.dot_general(qs, kbd, (((1,), (1,)), ((), ())), preferred_element_type=F32)
        yield
        pg, rg = [], []
        for gm in range(GROUP):
            ps, rinv = [], jnp.zeros((WINDOW, KV_DIM), F32)
            for kv in range(N_KV):
                s_prev = s[gm * WINDOW:(gm + 1) * WINDOW, kv * 2 * WINDOW:kv * 2 * WINDOW + WINDOW]
                s_own = s[gm * WINDOW:(gm + 1) * WINDOW, kv * 2 * WINDOW + WINDOW:(kv + 1) * 2 * WINDOW]
                sc = jnp.where(causal, s_own, s_prev)
                if i == 0:
                    sc = jnp.where(prev_ok, sc, -jnp.inf)
                sink = sink_ref[gm, kv]
                mx = jnp.maximum(jnp.max(sc, axis=-1, keepdims=True), sink)
                p = jnp.exp(sc - mx)
                den = jnp.sum(p, axis=-1, keepdims=True) + jnp.exp(sink - mx)
                ps.append(jnp.where(causal, 0.0, p).astype(BF16))
                ps.append(jnp.where(causal, p, 0.0).astype(BF16))
                rinv = jnp.where(seg1[kv], 1.0 / den, rinv)
            pg.append(jnp.concatenate(ps, axis=1))
            rg.append(rinv)
            yield
        og = jnp.dot(jnp.concatenate(pg, axis=0), vbd, preferred_element_type=F32)
        for gm in range(GROUP):
            a_ref[rows, gm * KV_DIM:(gm + 1) * KV_DIM] = (
                og[gm * WINDOW:(gm + 1) * WINDOW] * rg[gm]).astype(a_ref.dtype)
        yield


def _attn_sample_kernel(q_ref, kt_ref, vt_ref, knt_ref, vnt_ref, sink_ref, o_ref, ko_ref, vo_ref, *, bb):
    i = pl.program_id(0)
    lane = lax.broadcasted_iota(jnp.int32, (HEAD_DIM, WINDOW), 1)
    col = lax.broadcasted_iota(jnp.int32, (bb * N_HEADS, N_KV * WINDOW), 1)
    row = lax.broadcasted_iota(jnp.int32, (bb * N_HEADS, N_KV * WINDOW), 0)
    own = (col // WINDOW) == (row % N_KV)
    sink = jnp.concatenate([sink_ref[...]] * bb, axis=0)

    def shifted(src_ref, new_t_ref, out_ref, j):
        new = pltpu.roll(new_t_ref[...], (WINDOW - 1) - (i * bb + j), axis=1)
        parts = []
        for kv in range(N_KV):
            hs = slice(kv * HEAD_DIM, (kv + 1) * HEAD_DIM)
            slab = jnp.where(lane == WINDOW - 1, new[hs], pltpu.roll(src_ref[j, kv], WINDOW - 1, axis=1))
            out_ref[j, kv] = slab
            parts.append(slab.astype(BF16))
        return jnp.concatenate(parts, axis=1)

    s = jnp.concatenate(
        [jnp.dot(q_ref[j].astype(BF16), shifted(kt_ref, knt_ref, ko_ref, j), preferred_element_type=F32)
         for j in range(bb)], axis=0)
    s = jnp.where(own, s, -jnp.inf)
    mx = jnp.maximum(jnp.max(s, axis=-1, keepdims=True), sink)
    p = jnp.exp(s - mx)
    den = jnp.sum(p, axis=-1, keepdims=True) + jnp.exp(sink - mx)
    p = p.astype(BF16)
    for j in range(bb):
        rs = slice(j * N_HEADS, (j + 1) * N_HEADS)
        vcat = shifted(vt_ref, vnt_ref, vo_ref, j)
        o = lax.dot_general(p[rs], vcat, (((1,), (1,)), ((), ())), preferred_element_type=F32)
        o_ref[j] = o / den[rs]


def _attn_sample(q3, kt, vt, knt, vnt, sinks, *, bb):
    b = q3.shape[0]
    blk4 = pl.BlockSpec((bb, N_KV, HEAD_DIM, WINDOW), lambda i: (i, 0, 0, 0))
    blk3 = pl.BlockSpec((bb, N_HEADS, HEAD_DIM), lambda i: (i, 0, 0))
    return pl.pallas_call(
        functools.partial(_attn_sample_kernel, bb=bb),
        grid=(b // bb,),
        in_specs=[blk3, blk4, blk4, _const_spec(knt.shape), _const_spec(vnt.shape), _const_spec(sinks.shape)],
        out_specs=[blk3, blk4, blk4],
        out_shape=[jax.ShapeDtypeStruct(q3.shape, F32), jax.ShapeDtypeStruct(kt.shape, F32),
                   jax.ShapeDtypeStruct(vt.shape, F32)],
        compiler_params=pltpu.CompilerParams(
            dimension_semantics=("arbitrary",), vmem_limit_bytes=VMEM_LIMIT),
        name="attn_sample",
    )(q3, kt, vt, knt, vnt, sinks)


def kernel(x_prompt, x_sample, state_pool, cache_k_win, cache_v_win, norm_g, ffn_w_gate, ffn_w_up,
           ffn_w_down, pool_w, pool_scale, kv_norm_g, w_kv, w_q, w_o, attn_sinks, final_norm_g):
    bp, seq, _ = x_prompt.shape
    bs = x_sample.shape[0]
    tm = 512
    g = lambda l, i: norm_g[l, i].reshape(1, D_MODEL)
    ffn_w = (ffn_w_gate.reshape(4, D_MODEL, D_FF), ffn_w_up.reshape(4, D_MODEL, D_FF),
             ffn_w_down.reshape(4, D_FF, D_MODEL))
    psc = pool_scale[0].reshape(1, D_MODEL)
    kvg = kv_norm_g.reshape(1, D_MODEL)
    wq = w_q[0].reshape(D_MODEL, N_KV, GROUP, HEAD_DIM).transpose(0, 2, 1, 3).reshape(D_MODEL, D_MODEL)
    wo = w_o[0].reshape(N_KV, GROUP, HEAD_DIM, D_MODEL).transpose(1, 0, 2, 3).reshape(D_MODEL, D_MODEL)
    sinks_gk = attn_sinks[0].reshape(N_KV, GROUP).T
    gfin = final_norm_g.reshape(1, D_MODEL)
    tabs_p = _rope_tables(np.arange(seq))
    tabs_s = _rope_tables(np.full((1,), PAST_LEN))

    x = x_prompt.reshape(bp * seq, D_MODEL)
    xs = x_sample.reshape(bs, D_MODEL)

    x, xs, *w_bf16 = _ffn(x, xs, g(0, 0), (*ffn_w, 0), tm=tm, seq=seq, cast_layers=(1, 2, 3))
    w1, w2, w3 = w_bf16[0:2], w_bf16[2:4], w_bf16[4:6]
    buf_t = jnp.transpose(state_pool[0], (1, 0, 2))
    xs, nbuf_t = _pool_sample(xs, buf_t, g(0, 1), pool_w[0], psc)
    x, xs, k_p, v_p, k_s, v_s, pool_p = _ffn(x, xs, g(0, 2), w1, tm=tm, seq=seq,
                                             pool=(g(0, 1), pool_w[0], psc),
                                             kv_g=kvg, wkv=w_kv, tabs_p=tabs_p, tabs_s=tabs_s)
    x, xs = _ffn(x, xs, g(1, 0), w2, tm=tm, seq=seq)
    q_s = _q_sample(xs, g(1, 1), wq, tabs_s)
    kt = jnp.transpose(cache_k_win, (0, 2, 3, 1))
    vt = jnp.transpose(cache_v_win, (0, 2, 3, 1))
    o_s, kt_new, vt_new = _attn_sample(q_s.reshape(bs, N_HEADS, HEAD_DIM), kt, vt, k_s.T, v_s.T,
                                       sinks_gk.reshape(N_HEADS, 1), bb=16)
    attn_s = o_s.reshape(bs, D_MODEL).astype(BF16)
    y_p, y_s = _ffn(x, xs, g(1, 2), w3, tm=tm, seq=seq, tabs_p=tabs_p, final_g=gfin,
                    attn=(g(1, 1), wq.astype(BF16), k_p, v_p, sinks_gk, attn_s, wo.astype(BF16)))

    y_prompt = y_p.reshape(bp, seq, D_MODEL)
    y_sample = y_s.reshape(bs, 1, D_MODEL)
    pool_prompt = pool_p[None]
    pool_sample = jnp.transpose(nbuf_t, (1, 0, 2))[None]
    k_win_prompt = k_p.reshape(bp, seq, KV_DIM)[:, seq - WINDOW:].reshape(bp, WINDOW, N_KV, HEAD_DIM)
    v_win_prompt = v_p.reshape(bp, seq, KV_DIM)[:, seq - WINDOW:].reshape(bp, WINDOW, N_KV, HEAD_DIM)
    k_win_sample = jnp.transpose(kt_new, (0, 3, 1, 2))
    v_win_sample = jnp.transpose(vt_new, (0, 3, 1, 2))
    return (y_prompt, y_sample, pool_prompt, pool_sample, k_win_prompt, v_win_prompt,
            k_win_sample, v_win_sample)
```

```python
import functools

import numpy as np
import jax
import jax.numpy as jnp
from jax import lax
from jax.experimental import pallas as pl
from jax.experimental.pallas import tpu as pltpu

F32 = jnp.float32
BF16 = jnp.bfloat16

D_MODEL = 1024
D_FF = 2816
HEAD_DIM = 64
N_HEADS = 16
N_KV = 4
GROUP = 4
KV_DIM = N_KV * HEAD_DIM
WINDOW = 128
ROT_DIM = 16
ROPE_THETA = 500000.0
EPS = 1e-5
POOL_WINDOWS = (2, 4, 8, 16)
POOL_GROUP = 256
POOL_BUF = 15
PAST_LEN = 16384

LANES = 128
FF_CHUNK = 256
N_FF_CHUNKS = D_FF // FF_CHUNK
OUT_CHUNK = 256
N_OUT_CHUNKS = D_MODEL // OUT_CHUNK
VMEM_LIMIT = 60 * 1024 * 1024

GU_SLABS = 8
GU_ROWS = D_MODEL // GU_SLABS
DN_SLABS = 8
DN_ROWS = D_FF // DN_SLABS
STAGE_SLOTS = 4

CAST_GU_ROWS = 32
CAST_DN_ROWS = 128


def _const_spec(shape):
    nd = len(shape)
    return pl.BlockSpec(shape, lambda *_: (0,) * nd, pipeline_mode=pl.Buffered(1))


def _rms(x, g):
    ms = jnp.mean(x * x, axis=-1, keepdims=True)
    return x * lax.rsqrt(ms + EPS) * g


def _rope(x, c, s1, s2):
    pieces = []
    for j in range(x.shape[1] // LANES):
        xc = x[:, j * LANES:(j + 1) * LANES]
        up = pltpu.roll(xc, LANES - ROT_DIM // 2, axis=1)
        dn = pltpu.roll(xc, ROT_DIM // 2, axis=1)
        pieces.append(xc * c + up * s1 + dn * s2)
    return jnp.concatenate(pieces, axis=1) if len(pieces) > 1 else pieces[0]


def _rope_tables(pos):
    half = ROT_DIM // 2
    inv = np.power(ROPE_THETA, -np.arange(half, dtype=np.float64) * (2.0 / ROT_DIM))
    ang = np.asarray(pos, np.float64)[:, None] * inv[None, :]
    cos, sin = np.cos(ang), np.sin(ang)
    n = ang.shape[0]
    ones = np.ones((n, HEAD_DIM - ROT_DIM))
    zeros8 = np.zeros((n, half))
    zeros = np.zeros((n, HEAD_DIM - ROT_DIM))
    c = np.concatenate([cos, cos, ones], axis=1)
    s1 = np.concatenate([-sin, zeros8, zeros], axis=1)
    s2 = np.concatenate([zeros8, sin, zeros], axis=1)
    return [jnp.asarray(np.concatenate([t, t], axis=1), F32) for t in (c, s1, s2)]


def _stage_ffn_weights(wg_hbm, wu_hbm, wd_hbm, wgu_scr, wd_scr, gu_stage, dn_stage, gu_sem, dn_sem):
    chunks = []
    for src, col0 in ((wg_hbm, 0), (wu_hbm, FF_CHUNK)):
        for r in range(GU_SLABS):
            chunks.append(("gu", src, r, col0))
    for r in range(DN_SLABS):
        chunks.append(("dn", wd_hbm, r, 0))

    def copy(c):
        kind, src, r, _ = chunks[c]
        slot = c % STAGE_SLOTS
        if kind == "gu":
            return pltpu.make_async_copy(src.at[pl.ds(r * GU_ROWS, GU_ROWS), :], gu_stage.at[slot], gu_sem.at[slot])
        return pltpu.make_async_copy(src.at[pl.ds(r * DN_ROWS, DN_ROWS), :], dn_stage.at[slot], dn_sem.at[slot])

    for c in range(STAGE_SLOTS):
        copy(c).start()
    for c, (kind, _, r, col0) in enumerate(chunks):
        slot = c % STAGE_SLOTS
        copy(c).wait()
        if kind == "gu":
            for j in range(N_FF_CHUNKS):
                wgu_scr[j, r * GU_ROWS:(r + 1) * GU_ROWS, col0:col0 + FF_CHUNK] = (
                    gu_stage[slot, :, j * FF_CHUNK:(j + 1) * FF_CHUNK].astype(BF16))
        else:
            wd_scr[r * DN_ROWS:(r + 1) * DN_ROWS, :] = dn_stage[slot].astype(BF16)
        if c + STAGE_SLOTS < len(chunks):
            copy(c + STAGE_SLOTS).start()


def _run(*phase_generators, steps=None):
    live = list(zip(phase_generators, steps or (1,) * len(phase_generators)))
    while live:
        for gen, n in list(live):
            try:
                for _ in range(n):
                    next(gen)
            except StopIteration:
                live.remove((gen, n))


def _kv_rows(rows, src_ref, kvg_ref, wkv_scr, tabs, k_ref, v_ref):
    kvn = _rms(src_ref[0:rows, :], kvg_ref[...]).astype(BF16)
    yield
    kv = jnp.dot(kvn, wkv_scr[...], preferred_element_type=F32)
    yield
    tb = [t[...] for t in tabs]
    if tb[0].shape[0] != rows:
        tb = [jnp.broadcast_to(t, (rows, LANES)) for t in tb]
    k_ref[...] = _rope(kv[:, :KV_DIM], *tb)
    v_ref[...] = kv[:, KV_DIM:]
    yield


def _ffn_rows(rows, x_ref, a_ref, g_ref, gf_ref, kvg_ref, tabs, o_ref, k_ref, v_ref,
              wgu_scr, wd_scr, wo_ref, wkv_scr, xn_ref, h_ref, xr_ref, keep_ref=None):
    rs = slice(0, rows)
    if a_ref is not None:
        xr_ref[rs, :] = x_ref[...] + jnp.dot(a_ref[...], wo_ref[...], preferred_element_type=F32)
    elif xr_ref is not None:
        xr_ref[rs, :] = x_ref[...]
    res = (lambda sl: xr_ref[rs, sl]) if xr_ref is not None else (lambda sl: x_ref[:, sl])
    xn_ref[rs, :] = _rms(res(slice(None)), g_ref[...]).astype(BF16)
    yield
    for j in range(N_FF_CHUNKS):
        r = jnp.dot(xn_ref[rs, :], wgu_scr[j], preferred_element_type=F32)
        gate, up = r[:, :FF_CHUNK], r[:, FF_CHUNK:]
        h_ref[rs, j * FF_CHUNK:(j + 1) * FF_CHUNK] = (gate * jax.nn.sigmoid(gate) * up).astype(BF16)
        yield
    for n in range(N_OUT_CHUNKS):
        sl = slice(n * OUT_CHUNK, (n + 1) * OUT_CHUNK)
        y = jnp.dot(h_ref[rs, :], wd_scr[:, sl], preferred_element_type=F32)
        out = res(sl) + 0.5 * y
        o_ref[:, sl] = out
        if keep_ref is not None:
            keep_ref[rs, sl] = out
        yield
    if kvg_ref is not None and keep_ref is None:
        yield from _kv_rows(rows, o_ref, kvg_ref, wkv_scr, tabs, k_ref, v_ref)
    if gf_ref is not None:
        o_ref[...] = _rms(o_ref[...], gf_ref[...])


def _cast_slabs(wg_in, wu_in, wd_in, wgu_out, wd_out):
    for j in range(N_FF_CHUNKS):
        cs = slice(j * FF_CHUNK, (j + 1) * FF_CHUNK)
        wgu_out[j, :, 0:FF_CHUNK] = wg_in[:, cs].astype(BF16)
        wgu_out[j, :, FF_CHUNK:2 * FF_CHUNK] = wu_in[:, cs].astype(BF16)
    wd_out[...] = wd_in[...].astype(BF16)
    yield


def _permute_qo(wq_ref, wo_ref, wqp_ref, wop_ref):
    lane = lax.broadcasted_iota(jnp.int32, (D_MODEL, LANES), 1)
    src_head = lambda h: (h % N_KV) * GROUP + h // N_KV
    for c in range(D_MODEL // LANES):
        halves = []
        for half in (0, 1):
            hs = src_head(2 * c + half)
            col = wq_ref[:, (hs // 2) * LANES:(hs // 2 + 1) * LANES]
            halves.append(col if hs % 2 == half else pltpu.roll(col, HEAD_DIM, axis=1))
        wqp_ref[:, c * LANES:(c + 1) * LANES] = jnp.where(lane < HEAD_DIM, halves[0], halves[1]).astype(BF16)
    for h in range(N_HEADS):
        hs = src_head(h)
        wop_ref[h * HEAD_DIM:(h + 1) * HEAD_DIM, :] = wo_ref[hs * HEAD_DIM:(hs + 1) * HEAD_DIM, :].astype(BF16)


def _ffn_kernel(*refs, nt, tm, ms, per_seq, layer, n_cast, has_pool, has_attn, has_kv, has_final, has_qprep):
    it = iter(refs)
    x_ref, xs_ref = next(it), next(it)
    a_ref = as_ref = wo_ref = gf_ref = kvg_ref = wkv_ref = None
    tabs_p = tabs_s = None
    if has_pool:
        gp_ref, pw_ref, psc_ref = next(it), next(it), next(it)
    if has_attn:
        xq_ref, gq_ref, wq_ref = next(it), next(it), next(it)
        tabs_q = [next(it) for _ in range(3)]
        kq_ref, vq_ref, sink_ref, as_ref, wo_ref = [next(it) for _ in range(5)]
    g_ref = next(it)
    if layer is not None:
        wg_hbm, wu_hbm, wd_hbm = next(it), next(it), next(it)
    else:
        wgu_hbm, wdn_hbm = next(it), next(it)
    cast_in = [[next(it) for _ in range(3)] for _ in range(n_cast)]
    if has_kv:
        kvg_ref, wkv_ref = next(it), next(it)
        tabs_p = [next(it) for _ in range(3)]
        tabs_s = [next(it) for _ in range(3)]
    if has_final:
        gf_ref = next(it)
    if has_qprep:
        wqf_ref, wof_ref, gqs_ref = next(it), next(it), next(it)
        tabs_qs = [next(it) for _ in range(3)]
    o_ref, os_ref = next(it), next(it)
    k_ref = v_ref = ks_ref = vs_ref = None
    if has_kv:
        k_ref, v_ref, ks_ref, vs_ref = next(it), next(it), next(it), next(it)
    cast_out = [[next(it) for _ in range(2)] for _ in range(n_cast)]
    if has_pool:
        pbuf_ref = next(it)
    if has_qprep:
        wqp_ref, wop_ref, qs_ref = next(it), next(it), next(it)
    wgu_scr, wd_scr = next(it), next(it)
    if layer is not None:
        gu_stage, dn_stage, gu_sem, dn_sem = [next(it) for _ in range(4)]
    else:
        w_sem = next(it)
    xn_ref, h_ref = next(it), next(it)
    xr_ref = wkv_scr = None
    if has_attn:
        a_ref, xr_ref, kbuf, vbuf = [next(it) for _ in range(4)]
    if has_pool:
        x2_scr, xr_ref, hext_ref = [next(it) for _ in range(3)]
    xkeep = None
    if has_kv:
        wkv_scr, xkeep = next(it), next(it)

    i = pl.program_id(0)

    def pool(tile):
        return _pool_rows(tile % per_seq, x_ref, gp_ref, pw_ref, psc_ref, x2_scr, pbuf_ref, hext_ref, tm)

    def attention(tile):
        nprev = jnp.where(tile % per_seq == 0, 0, WINDOW)
        return _attn_rows(nprev, xq_ref, gq_ref, wq_ref, tabs_q, kq_ref, vq_ref, sink_ref, a_ref,
                          kbuf, vbuf, tm)

    @pl.when(i == 0)
    def _():
        if layer is not None:
            _stage_ffn_weights(wg_hbm.at[layer], wu_hbm.at[layer], wd_hbm.at[layer], wgu_scr, wd_scr,
                               gu_stage, dn_stage, gu_sem, dn_sem)
        else:
            copies = [pltpu.make_async_copy(wgu_hbm, wgu_scr, w_sem.at[0]),
                      pltpu.make_async_copy(wdn_hbm, wd_scr, w_sem.at[1])]
            for c in copies:
                c.start()
        if has_attn:
            kbuf[...] = jnp.zeros(kbuf.shape, F32)
            vbuf[...] = jnp.zeros(vbuf.shape, F32)
            _run(attention(0))
        if has_pool:
            hext_ref[...] = jnp.zeros(hext_ref.shape, F32)
            _run(pool(0))
        if has_kv:
            wkv_scr[...] = wkv_ref[...].astype(BF16)
            xkeep[...] = jnp.zeros(xkeep.shape, F32)
        if has_qprep:
            _permute_qo(wqf_ref, wof_ref, wqp_ref, wop_ref)
        if layer is None:
            for c in copies:
                c.wait()

    common = (wgu_scr, wd_scr, wo_ref, wkv_scr, xn_ref, h_ref, xr_ref)

    @pl.when((i >= 1) & (i <= nt))
    def _():
        side = [_cast_slabs(*cast_in[c], *cast_out[c]) for c in range(n_cast)]
        if has_kv:
            side.append(_kv_rows(tm, xkeep, kvg_ref, wkv_scr, tabs_p, k_ref, v_ref))
        if has_attn:
            side.append(attention(jnp.minimum(i, nt - 1)))
        if has_pool:
            side.append(pool(jnp.minimum(i, nt - 1)))
        xin = x2_scr if has_pool else x_ref
        _run(_ffn_rows(tm, xin, a_ref, g_ref, gf_ref, kvg_ref, tabs_p, o_ref, k_ref, v_ref, *common,
                       keep_ref=xkeep), *side)

    @pl.when(i == nt + 1)
    def _():
        side = [_kv_rows(tm, xkeep, kvg_ref, wkv_scr, tabs_p, k_ref, v_ref)] if has_kv else []
        _run(_ffn_rows(ms, xs_ref, as_ref, g_ref, gf_ref, kvg_ref, tabs_s, os_ref, ks_ref, vs_ref, *common),
             *side)
        if has_qprep:
            qn = _rms(os_ref[...], gqs_ref[...]).astype(BF16)
            y = jnp.dot(qn, wqp_ref[...], preferred_element_type=F32)
            tb = [jnp.broadcast_to(t[...], (ms, LANES)) for t in tabs_qs]
            qs_ref[...] = _rope(y, *tb) * (HEAD_DIM ** -0.5)


def _ffn(x, xs, g, weights, *, tm, seq, cast_layers=(), pool=None, attn=None, kv_g=None, wkv=None,
         tabs_p=None, tabs_s=None, final_g=None, qprep=None):
    mp, ms = x.shape[0], xs.shape[0]
    nt = mp // tm
    per_seq = seq // tm
    layer = weights[3] if len(weights) == 4 else None
    n_cast = len(cast_layers)
    assert n_cast == 0 or (nt * CAST_GU_ROWS == D_MODEL and nt * CAST_DN_ROWS >= D_FF)
    has_attn, has_kv, has_final = attn is not None, kv_g is not None, final_g is not None
    has_pool = pool is not None
    tile = lambda i: jnp.clip(i - 1, 0, nt - 1)
    nxt = lambda i: jnp.minimum(i, nt - 1)
    prev = lambda i: jnp.clip(i - 2, 0, nt - 1)
    row = lambda w: pl.BlockSpec((tm, w), lambda i: (tile(i), 0))
    nrow = lambda w: pl.BlockSpec((tm, w), lambda i: (nxt(i), 0))
    hbm = pl.BlockSpec(memory_space=pl.ANY)

    args, specs = [x, xs], [nrow(D_MODEL) if has_pool else row(D_MODEL), _const_spec(xs.shape)]
    if has_pool:
        args += list(pool)
        specs += [_const_spec(a.shape) for a in pool]
    if has_attn:
        gq, wq, k, v, sinks, attn_s, wo = attn
        args += [x, gq, wq, *tabs_p, k, v, sinks, attn_s, wo]
        specs += [nrow(D_MODEL), _const_spec(gq.shape), _const_spec(wq.shape)]
        specs += [pl.BlockSpec((tm, LANES), lambda i: (nxt(i) % per_seq, 0))] * 3
        specs += [nrow(KV_DIM), nrow(KV_DIM), pl.BlockSpec(memory_space=pltpu.SMEM),
                  _const_spec(attn_s.shape), _const_spec(wo.shape)]
    args += [g, *weights[:3]] if layer is not None else [g, *weights]
    specs += [_const_spec(g.shape)] + [hbm] * (3 if layer is not None else 2)
    dn_blocks = D_FF // CAST_DN_ROWS
    dn_tile = lambda i: jnp.minimum(tile(i), dn_blocks - 1)
    for cl in cast_layers:
        args += list(weights[:3])
        specs += [pl.BlockSpec((None, CAST_GU_ROWS, D_FF), lambda i, cl=cl: (cl, tile(i), 0)),
                  pl.BlockSpec((None, CAST_GU_ROWS, D_FF), lambda i, cl=cl: (cl, tile(i), 0)),
                  pl.BlockSpec((None, CAST_DN_ROWS, D_MODEL), lambda i, cl=cl: (cl, dn_tile(i), 0))]
    if has_kv:
        args += [kv_g, wkv, *tabs_p, *tabs_s]
        specs += [_const_spec(kv_g.shape), _const_spec(wkv.shape)]
        specs += [pl.BlockSpec((tm, LANES), lambda i: (prev(i) % per_seq, 0))] * 3
        specs += [_const_spec((1, LANES))] * 3
    if has_final:
        args.append(final_g)
        specs.append(_const_spec(final_g.shape))
    has_qprep = qprep is not None
    if has_qprep:
        args += [*qprep, *tabs_s]
        specs += [_const_spec(a.shape) for a in qprep] + [_const_spec((1, LANES))] * 3

    out_shape = [jax.ShapeDtypeStruct((mp, D_MODEL), F32), jax.ShapeDtypeStruct((ms, D_MODEL), F32)]
    out_specs = [row(D_MODEL), _const_spec((ms, D_MODEL))]
    if has_kv:
        out_shape += [jax.ShapeDtypeStruct((mp, KV_DIM), F32)] * 2 + [jax.ShapeDtypeStruct((ms, KV_DIM), F32)] * 2
        out_specs += [pl.BlockSpec((tm, KV_DIM), lambda i: (prev(i), 0))] * 2 + [_const_spec((ms, KV_DIM))] * 2
    wgu_shape, wdn_shape = (N_FF_CHUNKS, D_MODEL, 2 * FF_CHUNK), (D_FF, D_MODEL)
    for _ in cast_layers:
        out_shape += [jax.ShapeDtypeStruct(wgu_shape, BF16), jax.ShapeDtypeStruct(wdn_shape, BF16)]
        out_specs += [pl.BlockSpec((N_FF_CHUNKS, CAST_GU_ROWS, 2 * FF_CHUNK), lambda i: (0, tile(i), 0)),
                      pl.BlockSpec((CAST_DN_ROWS, D_MODEL), lambda i: (dn_tile(i), 0))]
    if has_pool:
        out_shape += [jax.ShapeDtypeStruct((mp // seq, POOL_BUF, D_MODEL), F32)]
        out_specs += [pl.BlockSpec((1, POOL_BUF, D_MODEL), lambda i: (nxt(i) // per_seq, 0, 0))]
    if has_qprep:
        out_shape += [jax.ShapeDtypeStruct((D_MODEL, D_MODEL), BF16)] * 2 + [jax.ShapeDtypeStruct((ms, D_MODEL), F32)]
        out_specs += [_const_spec((D_MODEL, D_MODEL))] * 2 + [_const_spec((ms, D_MODEL))]

    scratch = [pltpu.VMEM(wgu_shape, BF16), pltpu.VMEM(wdn_shape, BF16)]
    if layer is not None:
        scratch += [pltpu.VMEM((STAGE_SLOTS, GU_ROWS, D_FF), F32),
                    pltpu.VMEM((STAGE_SLOTS, DN_ROWS, D_MODEL), F32),
                    pltpu.SemaphoreType.DMA((STAGE_SLOTS,)),
                    pltpu.SemaphoreType.DMA((STAGE_SLOTS,))]
    else:
        scratch += [pltpu.SemaphoreType.DMA((2,))]
    scratch += [pltpu.VMEM((tm, D_MODEL), BF16),
                pltpu.VMEM((tm, D_FF), BF16)]
    if has_attn:
        scratch += [pltpu.VMEM((tm, D_MODEL), BF16), pltpu.VMEM((tm, D_MODEL), F32),
                    pltpu.VMEM((WINDOW + tm, KV_DIM), F32), pltpu.VMEM((WINDOW + tm, KV_DIM), F32)]
    if has_pool:
        scratch += [pltpu.VMEM((tm, D_MODEL), F32), pltpu.VMEM((tm, D_MODEL), F32),
                    pltpu.VMEM((HALO + tm, D_MODEL), F32)]
    if has_kv:
        scratch += [pltpu.VMEM((D_MODEL, 2 * KV_DIM), BF16), pltpu.VMEM((tm, D_MODEL), F32)]

    return pl.pallas_call(
        functools.partial(_ffn_kernel, nt=nt, tm=tm, ms=ms, per_seq=per_seq, layer=layer, n_cast=n_cast,
                          has_pool=has_pool, has_attn=has_attn, has_kv=has_kv, has_final=has_final,
                          has_qprep=has_qprep),
        grid=(nt + 2,),
        in_specs=specs,
        out_specs=out_specs,
        out_shape=out_shape,
        scratch_shapes=scratch,
        compiler_params=pltpu.CompilerParams(
            dimension_semantics=("arbitrary",), vmem_limit_bytes=VMEM_LIMIT),
        name="ffn",
    )(*args)


HALO = 16
assert POOL_WINDOWS == tuple(2 ** (i + 1) for i in range(len(POOL_WINDOWS))) and max(POOL_WINDOWS) <= HALO


def _pool_rows(t_seq, x_ref, g_ref, w_ref, sc_ref, o_ref, buf_ref, hext_ref, tp):
    keep = jnp.where(t_seq == 0, 0, HALO)
    hrow = lax.broadcasted_iota(jnp.int32, (HALO, D_MODEL), 0)
    hext_ref[0:HALO, :] = jnp.where(hrow < keep, hext_ref[tp:tp + HALO, :], 0.0)
    x = x_ref[...]
    h = _rms(x, g_ref[...])
    hext_ref[HALO:HALO + tp, :] = h
    buf_ref[0] = hext_ref[HALO + tp - POOL_BUF:HALO + tp, :]
    yield
    pos = (t_seq * tp + lax.broadcasted_iota(jnp.int32, (tp, 1), 0)).astype(F32)
    sums, cur, shift = [], hext_ref[...], 1
    for gi in range(len(POOL_WINDOWS)):
        cur = cur + pltpu.roll(cur, shift, axis=0)
        sums.append(cur[HALO:, :POOL_GROUP])
        if gi + 1 < len(POOL_WINDOWS):
            cur, shift = cur[:, POOL_GROUP:], 2 * shift
        yield
    for gi, w in enumerate(POOL_WINDOWS):
        cs = slice(gi * POOL_GROUP, (gi + 1) * POOL_GROUP)
        cnt = jnp.minimum(float(w), pos + 1.0)
        diff = (sums[gi] / cnt - hext_ref[HALO:HALO + tp, cs]).astype(BF16)
        mixed = jnp.dot(diff, w_ref[gi].astype(BF16), preferred_element_type=F32)
        o_ref[:, cs] = x_ref[:, cs] + mixed * sc_ref[:, cs]
        yield


def _pool_sample_kernel(x_ref, buf_ref, g_ref, w_ref, sc_ref, o_ref, nbuf_ref):
    x = x_ref[...]
    h = _rms(x, g_ref[...])
    nbuf_ref[0:POOL_BUF - 1] = buf_ref[1:POOL_BUF]
    nbuf_ref[POOL_BUF - 1] = h
    mixed = []
    for gi, w in enumerate(POOL_WINDOWS):
        cs = slice(gi * POOL_GROUP, (gi + 1) * POOL_GROUP)
        s = h[:, cs]
        for k in range(1, w):
            s = s + buf_ref[POOL_BUF - k, :, cs]
        diff = (s / float(w) - h[:, cs]).astype(BF16)
        mixed.append(jnp.dot(diff, w_ref[gi].astype(BF16), preferred_element_type=F32))
    o_ref[...] = x + jnp.concatenate(mixed, axis=1) * sc_ref[...]


def _pool_sample(x, buf_t, g, w, sc):
    m = x.shape[0]
    return pl.pallas_call(
        _pool_sample_kernel,
        out_shape=[jax.ShapeDtypeStruct((m, D_MODEL), F32), jax.ShapeDtypeStruct(buf_t.shape, F32)],
        compiler_params=pltpu.CompilerParams(vmem_limit_bytes=VMEM_LIMIT),
        name="pool_sample",
    )(x, buf_t, g, w, sc)


def _seg_mask(rows):
    lane = lax.broadcasted_iota(jnp.int32, (rows, KV_DIM), 1)
    return [(lane >= kv * HEAD_DIM) & (lane < (kv + 1) * HEAD_DIM) for kv in range(N_KV)]


def _attn_rows(nprev, x_ref, g_ref, wq_ref, tabs, k_ref, v_ref, sink_ref, a_ref, kbuf, vbuf, tq):
    hrow = lax.broadcasted_iota(jnp.int32, (WINDOW, KV_DIM), 0)
    kbuf[0:WINDOW, :] = jnp.where(hrow < nprev, kbuf[tq:tq + WINDOW, :], 0.0)
    vbuf[0:WINDOW, :] = jnp.where(hrow < nprev, vbuf[tq:tq + WINDOW, :], 0.0)
    kbuf[WINDOW:WINDOW + tq, :] = k_ref[...]
    vbuf[WINDOW:WINDOW + tq, :] = v_ref[...]

    hn = _rms(x_ref[...], g_ref[...]).astype(BF16)
    q = jnp.dot(hn, wq_ref[...], preferred_element_type=F32)
    q = (_rope(q, *[t[...] for t in tabs]) * (HEAD_DIM ** -0.5)).astype(BF16)
    yield

    seg2 = _seg_mask(2 * WINDOW)
    seg1 = _seg_mask(WINDOW)
    qi = lax.broadcasted_iota(jnp.int32, (WINDOW, WINDOW), 0)
    kj = lax.broadcasted_iota(jnp.int32, (WINDOW, WINDOW), 1)
    causal = kj <= qi
    prev_ok = kj <= qi + nprev

    for i in range(tq // WINDOW):
        kk = kbuf[i * WINDOW:(i + 2) * WINDOW, :]
        vv = vbuf[i * WINDOW:(i + 2) * WINDOW, :]
        kbd = jnp.concatenate([jnp.where(m, kk, 0.0).astype(BF16) for m in seg2], axis=0)
        vbd = jnp.concatenate([jnp.where(m, vv, 0.0).astype(BF16) for m in seg2], axis=0)
        rows = slice(i * WINDOW, (i + 1) * WINDOW)
        qs = jnp.concatenate([q[rows, gm * KV_DIM:(gm + 1) * KV_DIM] for gm in range(GROUP)], axis=0)
        s = lax.dot_general(qs, kbd, (((1,), (1,)), ((), ())), preferred_element_type=F32)
        yield
        pg, rg = [], []
        for gm in range(GROUP):
            ps, rinv = [], jnp.zeros((WINDOW, KV_DIM), F32)
            for kv in range(N_KV):
                s_prev = s[gm * WINDOW:(gm + 1) * WINDOW, kv * 2 * WINDOW:kv * 2 * WINDOW + WINDOW]
                s_own = s[gm * WINDOW:(gm + 1) * WINDOW, kv * 2 * WINDOW + WINDOW:(kv + 1) * 2 * WINDOW]
                sc = jnp.where(causal, s_own, s_prev)
                if i == 0:
                    sc = jnp.where(prev_ok, sc, -jnp.inf)
                sink = sink_ref[gm, kv]
                mx = jnp.maximum(jnp.max(sc, axis=-1, keepdims=True), sink)
                p = jnp.exp(sc - mx)
                den = jnp.sum(p, axis=-1, keepdims=True) + jnp.exp(sink - mx)
                ps.append(jnp.where(causal, 0.0, p).astype(BF16))
                ps.append(jnp.where(causal, p, 0.0).astype(BF16))
                rinv = jnp.where(seg1[kv], 1.0 / den, rinv)
            pg.append(jnp.concatenate(ps, axis=1))
            rg.append(rinv)
            yield
        og = jnp.dot(jnp.concatenate(pg, axis=0), vbd, preferred_element_type=F32)
        for gm in range(GROUP):
            a_ref[rows, gm * KV_DIM:(gm + 1) * KV_DIM] = (
                og[gm * WINDOW:(gm + 1) * WINDOW] * rg[gm]).astype(a_ref.dtype)
        yield


def _attn_sample_kernel(q_ref, kt_ref, vt_ref, knt_ref, vnt_ref, sink_ref, o_ref, ko_ref, vo_ref, *, bb):
    i = pl.program_id(0)
    lane = lax.broadcasted_iota(jnp.int32, (HEAD_DIM, WINDOW), 1)
    col = lax.broadcasted_iota(jnp.int32, (bb * N_HEADS, N_KV * WINDOW), 1)
    row = lax.broadcasted_iota(jnp.int32, (bb * N_HEADS, N_KV * WINDOW), 0)
    own = (col // WINDOW) == (row % N_KV)
    sink = jnp.concatenate([sink_ref[...]] * bb, axis=0)

    def shifted(src_ref, new_t_ref, out_ref, j):
        new = pltpu.roll(new_t_ref[...], (WINDOW - 1) - (i * bb + j), axis=1)
        parts = []
        for kv in range(N_KV):
            hs = slice(kv * HEAD_DIM, (kv + 1) * HEAD_DIM)
            slab = jnp.where(lane == WINDOW - 1, new[hs], pltpu.roll(src_ref[j, kv], WINDOW - 1, axis=1))
            out_ref[j, kv] = slab
            parts.append(slab.astype(BF16))
        return jnp.concatenate(parts, axis=1)

    s = jnp.concatenate(
        [jnp.dot(q_ref[j].astype(BF16), shifted(kt_ref, knt_ref, ko_ref, j), preferred_element_type=F32)
         for j in range(bb)], axis=0)
    s = jnp.where(own, s, -jnp.inf)
    mx = jnp.maximum(jnp.max(s, axis=-1, keepdims=True), sink)
    p = jnp.exp(s - mx)
    den = jnp.sum(p, axis=-1, keepdims=True) + jnp.exp(sink - mx)
    p = p.astype(BF16)
    for j in range(bb):
        rs = slice(j * N_HEADS, (j + 1) * N_HEADS)
        vcat = shifted(vt_ref, vnt_ref, vo_ref, j)
        o = lax.dot_general(p[rs], vcat, (((1,), (1,)), ((), ())), preferred_element_type=F32)
        o_ref[j] = o / den[rs]


def _attn_sample(q3, kt, vt, knt, vnt, sinks, *, bb):
    b = q3.shape[0]
    blk4 = pl.BlockSpec((bb, N_KV, HEAD_DIM, WINDOW), lambda i: (i, 0, 0, 0))
    blk3 = pl.BlockSpec((bb, N_HEADS, HEAD_DIM), lambda i: (i, 0, 0))
    return pl.pallas_call(
        functools.partial(_attn_sample_kernel, bb=bb),
        grid=(b // bb,),
        in_specs=[blk3, blk4, blk4, _const_spec(knt.shape), _const_spec(vnt.shape), _const_spec(sinks.shape)],
        out_specs=[blk3, blk4, blk4],
        out_shape=[jax.ShapeDtypeStruct(q3.shape, F32), jax.ShapeDtypeStruct(kt.shape, F32),
                   jax.ShapeDtypeStruct(vt.shape, F32)],
        compiler_params=pltpu.CompilerParams(
            dimension_semantics=("arbitrary",), vmem_limit_bytes=VMEM_LIMIT),
        name="attn_sample",
    )(q3, kt, vt, knt, vnt, sinks)


def kernel(x_prompt, x_sample, state_pool, cache_k_win, cache_v_win, norm_g, ffn_w_gate, ffn_w_up,
           ffn_w_down, pool_w, pool_scale, kv_norm_g, w_kv, w_q, w_o, attn_sinks, final_norm_g):
    bp, seq, _ = x_prompt.shape
    bs = x_sample.shape[0]
    tm = 512
    g = lambda l, i: norm_g[l, i].reshape(1, D_MODEL)
    ffn_w = (ffn_w_gate.reshape(4, D_MODEL, D_FF), ffn_w_up.reshape(4, D_MODEL, D_FF),
             ffn_w_down.reshape(4, D_FF, D_MODEL))
    psc = pool_scale[0].reshape(1, D_MODEL)
    kvg = kv_norm_g.reshape(1, D_MODEL)
    sinks_gk = attn_sinks[0].reshape(N_KV, GROUP).T
    gfin = final_norm_g.reshape(1, D_MODEL)
    tabs_p = _rope_tables(np.arange(seq))
    tabs_s = _rope_tables(np.full((1,), PAST_LEN))

    x = x_prompt.reshape(bp * seq, D_MODEL)
    xs = x_sample.reshape(bs, D_MODEL)

    x, xs, *w_bf16 = _ffn(x, xs, g(0, 0), (*ffn_w, 0), tm=tm, seq=seq, cast_layers=(1, 2, 3))
    w1, w2, w3 = w_bf16[0:2], w_bf16[2:4], w_bf16[4:6]
    buf_t = jnp.transpose(state_pool[0], (1, 0, 2))
    xs, nbuf_t = _pool_sample(xs, buf_t, g(0, 1), pool_w[0], psc)
    x, xs, k_p, v_p, k_s, v_s, pool_p = _ffn(x, xs, g(0, 2), w1, tm=tm, seq=seq,
                                             pool=(g(0, 1), pool_w[0], psc),
                                             kv_g=kvg, wkv=w_kv, tabs_p=tabs_p, tabs_s=tabs_s)
    x, xs, wq, wo, q_s = _ffn(x, xs, g(1, 0), w2, tm=tm, seq=seq, tabs_s=tabs_s,
                              qprep=(w_q[0], w_o[0], g(1, 1)))
    kt = jnp.transpose(cache_k_win, (0, 2, 3, 1))
    vt = jnp.transpose(cache_v_win, (0, 2, 3, 1))
    o_s, kt_new, vt_new = _attn_sample(q_s.reshape(bs, N_HEADS, HEAD_DIM), kt, vt, k_s.T, v_s.T,
                                       sinks_gk.reshape(N_HEADS, 1), bb=16)
    attn_s = o_s.reshape(bs, D_MODEL).astype(BF16)
    y_p, y_s = _ffn(x, xs, g(1, 2), w3, tm=tm, seq=seq, tabs_p=tabs_p, final_g=gfin,
                    attn=(g(1, 1), wq, k_p, v_p, sinks_gk, attn_s, wo))

    y_prompt = y_p.reshape(bp, seq, D_MODEL)
    y_sample = y_s.reshape(bs, 1, D_MODEL)
    pool_prompt = pool_p[None]
    pool_sample = jnp.transpose(nbuf_t, (1, 0, 2))[None]
    k_win_prompt = k_p.reshape(bp, seq, KV_DIM)[:, seq - WINDOW:].reshape(bp, WINDOW, N_KV, HEAD_DIM)
    v_win_prompt = v_p.reshape(bp, seq, KV_DIM)[:, seq - WINDOW:].reshape(bp, WINDOW, N_KV, HEAD_DIM)
    k_win_sample = jnp.transpose(kt_new, (0, 3, 1, 2))
    v_win_sample = jnp.transpose(vt_new, (0, 3, 1, 2))
    return (y_prompt, y_sample, pool_prompt, pool_sample, k_win_prompt, v_win_prompt,
            k_win_sample, v_win_sample)
```

```python
import functools

import numpy as np
import jax
import jax.numpy as jnp
from jax import lax
from jax.experimental import pallas as pl
from jax.experimental.pallas import tpu as pltpu

F32 = jnp.float32
BF16 = jnp.bfloat16

D_MODEL = 1024
D_FF = 2816
HEAD_DIM = 64
N_HEADS = 16
N_KV = 4
GROUP = 4
KV_DIM = N_KV * HEAD_DIM
WINDOW = 128
ROT_DIM = 16
ROPE_THETA = 500000.0
EPS = 1e-5
POOL_WINDOWS = (2, 4, 8, 16)
POOL_GROUP = 256
POOL_BUF = 15
PAST_LEN = 16384

N_FFN = 4
LANES = 128
TILE_ROWS = 512
SAMPLE_ATTN_SEQS = 16
FF_CHUNK = 256
N_FF_CHUNKS = D_FF // FF_CHUNK
OUT_CHUNK = 256
N_OUT_CHUNKS = D_MODEL // OUT_CHUNK
VMEM_LIMIT = 60 * 1024 * 1024

GU_SLABS = 8
GU_ROWS = D_MODEL // GU_SLABS
DN_SLABS = 8
DN_ROWS = D_FF // DN_SLABS
STAGE_SLOTS = 4

CAST_GU_ROWS = 32
CAST_DN_ROWS = 128


def _const_spec(shape):
    nd = len(shape)
    return pl.BlockSpec(shape, lambda *_: (0,) * nd, pipeline_mode=pl.Buffered(1))


def _rms(x, g):
    ms = jnp.mean(x * x, axis=-1, keepdims=True)
    return x * lax.rsqrt(ms + EPS) * g


def _rope(x, c, s1, s2):
    pieces = []
    for j in range(x.shape[1] // LANES):
        xc = x[:, j * LANES:(j + 1) * LANES]
        up = pltpu.roll(xc, LANES - ROT_DIM // 2, axis=1)
        dn = pltpu.roll(xc, ROT_DIM // 2, axis=1)
        pieces.append(xc * c + up * s1 + dn * s2)
    return jnp.concatenate(pieces, axis=1) if len(pieces) > 1 else pieces[0]


def _rope_tables(pos):
    half = ROT_DIM // 2
    inv = np.power(ROPE_THETA, -np.arange(half, dtype=np.float64) * (2.0 / ROT_DIM))
    ang = np.asarray(pos, np.float64)[:, None] * inv[None, :]
    cos, sin = np.cos(ang), np.sin(ang)
    n = ang.shape[0]
    ones = np.ones((n, HEAD_DIM - ROT_DIM))
    zeros8 = np.zeros((n, half))
    zeros = np.zeros((n, HEAD_DIM - ROT_DIM))
    c = np.concatenate([cos, cos, ones], axis=1)
    s1 = np.concatenate([-sin, zeros8, zeros], axis=1)
    s2 = np.concatenate([zeros8, sin, zeros], axis=1)
    return [jnp.asarray(np.concatenate([t, t], axis=1), F32) for t in (c, s1, s2)]


def _stage_ffn_weights(wg_hbm, wu_hbm, wd_hbm, wgu_scr, wd_scr, gu_stage, dn_stage, gu_sem, dn_sem):
    chunks = []
    for src, col0 in ((wg_hbm, 0), (wu_hbm, FF_CHUNK)):
        for r in range(GU_SLABS):
            chunks.append(("gu", src, r, col0))
    for r in range(DN_SLABS):
        chunks.append(("dn", wd_hbm, r, 0))

    def copy(c):
        kind, src, r, _ = chunks[c]
        slot = c % STAGE_SLOTS
        if kind == "gu":
            return pltpu.make_async_copy(src.at[pl.ds(r * GU_ROWS, GU_ROWS), :], gu_stage.at[slot], gu_sem.at[slot])
        return pltpu.make_async_copy(src.at[pl.ds(r * DN_ROWS, DN_ROWS), :], dn_stage.at[slot], dn_sem.at[slot])

    for c in range(STAGE_SLOTS):
        copy(c).start()
    for c, (kind, _, r, col0) in enumerate(chunks):
        slot = c % STAGE_SLOTS
        copy(c).wait()
        if kind == "gu":
            for j in range(N_FF_CHUNKS):
                wgu_scr[j, r * GU_ROWS:(r + 1) * GU_ROWS, col0:col0 + FF_CHUNK] = (
                    gu_stage[slot, :, j * FF_CHUNK:(j + 1) * FF_CHUNK].astype(BF16))
        else:
            wd_scr[r * DN_ROWS:(r + 1) * DN_ROWS, :] = dn_stage[slot].astype(BF16)
        if c + STAGE_SLOTS < len(chunks):
            copy(c + STAGE_SLOTS).start()


def _run(*phase_generators):
    live = list(phase_generators)
    while live:
        for gen in list(live):
            try:
                next(gen)
            except StopIteration:
                live.remove(gen)


def _kv_rows(rows, src_ref, kvg_ref, wkv_scr, tabs, k_ref, v_ref):
    kvn = _rms(src_ref[0:rows, :], kvg_ref[...]).astype(BF16)
    yield
    kv = jnp.dot(kvn, wkv_scr[...], preferred_element_type=F32)
    yield
    tb = [t[...] for t in tabs]
    if tb[0].shape[0] != rows:
        tb = [jnp.broadcast_to(t, (rows, LANES)) for t in tb]
    k_ref[...] = _rope(kv[:, :KV_DIM], *tb)
    v_ref[...] = kv[:, KV_DIM:]
    yield


def _ffn_rows(rows, x_ref, a_ref, g_ref, gf_ref, o_ref, wgu_scr, wd_scr, wo_ref, xn_ref, h_ref, xr_ref):
    rs = slice(0, rows)
    if a_ref is not None:
        xr_ref[rs, :] = x_ref[...] + jnp.dot(a_ref[...], wo_ref[...], preferred_element_type=F32)
    elif xr_ref is not None:
        xr_ref[rs, :] = x_ref[...]
    res = (lambda sl: xr_ref[rs, sl]) if xr_ref is not None else (lambda sl: x_ref[:, sl])
    xn_ref[rs, :] = _rms(res(slice(None)), g_ref[...]).astype(BF16)
    yield
    for j in range(N_FF_CHUNKS):
        r = jnp.dot(xn_ref[rs, :], wgu_scr[j], preferred_element_type=F32)
        gate, up = r[:, :FF_CHUNK], r[:, FF_CHUNK:]
        h_ref[rs, j * FF_CHUNK:(j + 1) * FF_CHUNK] = (gate * jax.nn.sigmoid(gate) * up).astype(BF16)
        yield
    for n in range(N_OUT_CHUNKS):
        sl = slice(n * OUT_CHUNK, (n + 1) * OUT_CHUNK)
        y = jnp.dot(h_ref[rs, :], wd_scr[:, sl], preferred_element_type=F32)
        o_ref[:, sl] = res(sl) + 0.5 * y
        yield
    if gf_ref is not None:
        o_ref[...] = _rms(o_ref[...], gf_ref[...])


def _cast_slabs(wg_in, wu_in, wd_in, wgu_out, wd_out):
    for j in range(N_FF_CHUNKS):
        cs = slice(j * FF_CHUNK, (j + 1) * FF_CHUNK)
        wgu_out[j, :, 0:FF_CHUNK] = wg_in[:, cs].astype(BF16)
        wgu_out[j, :, FF_CHUNK:2 * FF_CHUNK] = wu_in[:, cs].astype(BF16)
    wd_out[...] = wd_in[...].astype(BF16)
    yield


def _permute_qo(wq_ref, wo_ref, wqp_ref, wop_ref):
    lane = lax.broadcasted_iota(jnp.int32, (D_MODEL, LANES), 1)
    src_head = lambda h: (h % N_KV) * GROUP + h // N_KV
    for c in range(D_MODEL // LANES):
        halves = []
        for half in (0, 1):
            hs = src_head(2 * c + half)
            col = wq_ref[:, (hs // 2) * LANES:(hs // 2 + 1) * LANES]
            halves.append(col if hs % 2 == half else pltpu.roll(col, HEAD_DIM, axis=1))
        wqp_ref[:, c * LANES:(c + 1) * LANES] = jnp.where(lane < HEAD_DIM, halves[0], halves[1]).astype(BF16)
    for h in range(N_HEADS):
        hs = src_head(h)
        wop_ref[h * HEAD_DIM:(h + 1) * HEAD_DIM, :] = wo_ref[hs * HEAD_DIM:(hs + 1) * HEAD_DIM, :].astype(BF16)


def _ffn_kernel(*refs, nt, tm, ms, per_seq, layer, n_cast, has_pool, has_attn, has_kv, has_final, has_qprep):
    it = iter(refs)
    x_ref, xs_ref = next(it), next(it)
    a_ref = as_ref = wo_ref = gf_ref = kvg_ref = wkv_ref = None
    tabs_p = tabs_s = None
    if has_pool:
        gp_ref, pw_ref, psc_ref, sbuf_ref = next(it), next(it), next(it), next(it)
    if has_attn:
        xq_ref, gq_ref, wq_ref = next(it), next(it), next(it)
        tabs_q = [next(it) for _ in range(3)]
        kq_ref, vq_ref, sink_ref, as_ref, wo_ref = [next(it) for _ in range(5)]
    g_ref = next(it)
    if layer is not None:
        wg_hbm, wu_hbm, wd_hbm = next(it), next(it), next(it)
    else:
        wgu_hbm, wdn_hbm = next(it), next(it)
    cast_in = [[next(it) for _ in range(3)] for _ in range(n_cast)]
    if has_kv:
        kvg_ref, wkv_ref = next(it), next(it)
        tabs_p = [next(it) for _ in range(3)]
        tabs_s = [next(it) for _ in range(3)]
    if has_final:
        gf_ref = next(it)
    if has_qprep:
        wqf_ref, wof_ref, gqs_ref = next(it), next(it), next(it)
        tabs_qs = [next(it) for _ in range(3)]
    o_ref, os_ref = next(it), next(it)
    k_ref = v_ref = ks_ref = vs_ref = None
    if has_kv:
        k_ref, v_ref, ks_ref, vs_ref = next(it), next(it), next(it), next(it)
    cast_out = [[next(it) for _ in range(2)] for _ in range(n_cast)]
    if has_pool:
        pbuf_ref, snbuf_ref = next(it), next(it)
    if has_qprep:
        wqp_ref, wop_ref, qs_ref = next(it), next(it), next(it)
    wgu_scr, wd_scr = next(it), next(it)
    if layer is not None:
        gu_stage, dn_stage, gu_sem, dn_sem = [next(it) for _ in range(4)]
    else:
        w_sem = next(it)
    xn_ref, h_ref = next(it), next(it)
    xr_ref = wkv_scr = None
    if has_attn:
        a_ref, xr_ref, kbuf, vbuf = [next(it) for _ in range(4)]
    if has_pool:
        x2_scr, xr_ref, hext_ref, xs2_scr = [next(it) for _ in range(4)]
    if has_kv:
        wkv_scr = next(it)

    i = pl.program_id(0)

    def pool(tile):
        return _pool_rows(tile % per_seq, x_ref, gp_ref, pw_ref, psc_ref, x2_scr, pbuf_ref, hext_ref, tm)

    def attention(tile):
        nprev = jnp.where(tile % per_seq == 0, 0, WINDOW)
        return _attn_rows(nprev, xq_ref, gq_ref, wq_ref, tabs_q, kq_ref, vq_ref, sink_ref, a_ref,
                          kbuf, vbuf, tm)

    @pl.when(i == 0)
    def _():
        if layer is not None:
            _stage_ffn_weights(wg_hbm.at[layer], wu_hbm.at[layer], wd_hbm.at[layer], wgu_scr, wd_scr,
                               gu_stage, dn_stage, gu_sem, dn_sem)
        else:
            copies = [pltpu.make_async_copy(wgu_hbm, wgu_scr, w_sem.at[0]),
                      pltpu.make_async_copy(wdn_hbm, wd_scr, w_sem.at[1])]
            for c in copies:
                c.start()
        if has_attn:
            kbuf[...] = jnp.zeros(kbuf.shape, F32)
            vbuf[...] = jnp.zeros(vbuf.shape, F32)
            _run(attention(0))
        if has_pool:
            hext_ref[...] = jnp.zeros(hext_ref.shape, F32)
            _run(pool(0))
        if has_kv:
            wkv_scr[...] = wkv_ref[...].astype(BF16)
        if has_qprep:
            _permute_qo(wqf_ref, wof_ref, wqp_ref, wop_ref)
        if layer is None:
            for c in copies:
                c.wait()

    common = (wgu_scr, wd_scr, wo_ref, xn_ref, h_ref, xr_ref)

    @pl.when((i >= 1) & (i <= nt))
    def _():
        side = [_cast_slabs(*cast_in[c], *cast_out[c]) for c in range(n_cast)]
        if has_kv:
            side.append(_kv_rows(tm, x_ref, kvg_ref, wkv_scr, tabs_p, k_ref, v_ref))
        if has_attn:
            side.append(attention(jnp.minimum(i, nt - 1)))
        if has_pool:
            side.append(pool(jnp.minimum(i, nt - 1)))
            blk = (i - 1) * (ms // POOL_S_ROWS) // nt
            side.append(_pool_sample_rows(blk * POOL_S_ROWS, xs_ref, sbuf_ref, gp_ref, pw_ref, psc_ref,
                                          xs2_scr, snbuf_ref))
        xin = x2_scr if has_pool else x_ref
        _run(_ffn_rows(tm, xin, a_ref, g_ref, gf_ref, o_ref, *common), *side)

    @pl.when(i == nt + 1)
    def _():
        side = [_kv_rows(ms, xs_ref, kvg_ref, wkv_scr, tabs_s, ks_ref, vs_ref)] if has_kv else []
        xsin = xs2_scr if has_pool else xs_ref
        _run(_ffn_rows(ms, xsin, as_ref, g_ref, gf_ref, os_ref, *common), *side)
        if has_qprep:
            qn = _rms(os_ref[...], gqs_ref[...]).astype(BF16)
            y = jnp.dot(qn, wqp_ref[...], preferred_element_type=F32)
            tb = [jnp.broadcast_to(t[...], (ms, LANES)) for t in tabs_qs]
            qs_ref[...] = _rope(y, *tb) * (HEAD_DIM ** -0.5)


def _ffn(x, xs, g, weights, *, tm, seq, cast_layers=(), pool=None, attn=None, kv_g=None, wkv=None,
         tabs_p=None, tabs_s=None, final_g=None, qprep=None):
    mp, ms = x.shape[0], xs.shape[0]
    nt = mp // tm
    per_seq = seq // tm
    layer = weights[3] if len(weights) == 4 else None
    n_cast = len(cast_layers)
    assert n_cast == 0 or (nt * CAST_GU_ROWS == D_MODEL and nt * CAST_DN_ROWS >= D_FF)
    has_attn, has_kv, has_final = attn is not None, kv_g is not None, final_g is not None
    has_pool = pool is not None
    assert not (has_kv and has_pool)
    tile = lambda i: jnp.clip(i - 1, 0, nt - 1)
    nxt = lambda i: jnp.minimum(i, nt - 1)
    row = lambda w: pl.BlockSpec((tm, w), lambda i: (tile(i), 0))
    nrow = lambda w: pl.BlockSpec((tm, w), lambda i: (nxt(i), 0))
    hbm = pl.BlockSpec(memory_space=pl.ANY)

    args, specs = [x, xs], [nrow(D_MODEL) if has_pool else row(D_MODEL), _const_spec(xs.shape)]
    sblk = pl.BlockSpec((POOL_BUF, POOL_S_ROWS, D_MODEL), lambda i: (0, tile(i) * (ms // POOL_S_ROWS) // nt, 0))
    if has_pool:
        args += list(pool)
        specs += [_const_spec(a.shape) for a in pool[:3]] + [sblk]
    if has_attn:
        gq, wq, k, v, sinks, attn_s, wo = attn
        args += [x, gq, wq, *tabs_p, k, v, sinks, attn_s, wo]
        specs += [nrow(D_MODEL), _const_spec(gq.shape), _const_spec(wq.shape)]
        specs += [pl.BlockSpec((tm, LANES), lambda i: (nxt(i) % per_seq, 0))] * 3
        specs += [nrow(KV_DIM), nrow(KV_DIM), pl.BlockSpec(memory_space=pltpu.SMEM),
                  _const_spec(attn_s.shape), _const_spec(wo.shape)]
    args += [g, *weights[:3]] if layer is not None else [g, *weights]
    specs += [_const_spec(g.shape)] + [hbm] * (3 if layer is not None else 2)
    dn_blocks = D_FF // CAST_DN_ROWS
    dn_tile = lambda i: jnp.minimum(tile(i), dn_blocks - 1)
    for cl in cast_layers:
        args += list(weights[:3])
        specs += [pl.BlockSpec((None, CAST_GU_ROWS, D_FF), lambda i, cl=cl: (cl, tile(i), 0)),
                  pl.BlockSpec((None, CAST_GU_ROWS, D_FF), lambda i, cl=cl: (cl, tile(i), 0)),
                  pl.BlockSpec((None, CAST_DN_ROWS, D_MODEL), lambda i, cl=cl: (cl, dn_tile(i), 0))]
    if has_kv:
        args += [kv_g, wkv, *tabs_p, *tabs_s]
        specs += [_const_spec(kv_g.shape), _const_spec(wkv.shape)]
        specs += [pl.BlockSpec((tm, LANES), lambda i: (tile(i) % per_seq, 0))] * 3
        specs += [_const_spec((1, LANES))] * 3
    if has_final:
        args.append(final_g)
        specs.append(_const_spec(final_g.shape))
    has_qprep = qprep is not None
    if has_qprep:
        args += [*qprep, *tabs_s]
        specs += [_const_spec(a.shape) for a in qprep] + [_const_spec((1, LANES))] * 3

    out_shape = [jax.ShapeDtypeStruct((mp, D_MODEL), F32), jax.ShapeDtypeStruct((ms, D_MODEL), F32)]
    out_specs = [row(D_MODEL), _const_spec((ms, D_MODEL))]
    if has_kv:
        out_shape += [jax.ShapeDtypeStruct((mp, KV_DIM), F32)] * 2 + [jax.ShapeDtypeStruct((ms, KV_DIM), F32)] * 2
        out_specs += [row(KV_DIM)] * 2 + [_const_spec((ms, KV_DIM))] * 2
    wgu_shape, wdn_shape = (N_FF_CHUNKS, D_MODEL, 2 * FF_CHUNK), (D_FF, D_MODEL)
    for _ in cast_layers:
        out_shape += [jax.ShapeDtypeStruct(wgu_shape, BF16), jax.ShapeDtypeStruct(wdn_shape, BF16)]
        out_specs += [pl.BlockSpec((N_FF_CHUNKS, CAST_GU_ROWS, 2 * FF_CHUNK), lambda i: (0, tile(i), 0)),
                      pl.BlockSpec((CAST_DN_ROWS, D_MODEL), lambda i: (dn_tile(i), 0))]
    if has_pool:
        out_shape += [jax.ShapeDtypeStruct((mp // seq, POOL_BUF, D_MODEL), F32),
                      jax.ShapeDtypeStruct(pool[3].shape, F32)]
        out_specs += [pl.BlockSpec((1, POOL_BUF, D_MODEL), lambda i: (nxt(i) // per_seq, 0, 0)), sblk]
    if has_qprep:
        out_shape += [jax.ShapeDtypeStruct((D_MODEL, D_MODEL), BF16)] * 2 + [jax.ShapeDtypeStruct((ms, D_MODEL), F32)]
        out_specs += [_const_spec((D_MODEL, D_MODEL))] * 2 + [_const_spec((ms, D_MODEL))]

    scratch = [pltpu.VMEM(wgu_shape, BF16), pltpu.VMEM(wdn_shape, BF16)]
    if layer is not None:
        scratch += [pltpu.VMEM((STAGE_SLOTS, GU_ROWS, D_FF), F32),
                    pltpu.VMEM((STAGE_SLOTS, DN_ROWS, D_MODEL), F32),
                    pltpu.SemaphoreType.DMA((STAGE_SLOTS,)),
                    pltpu.SemaphoreType.DMA((STAGE_SLOTS,))]
    else:
        scratch += [pltpu.SemaphoreType.DMA((2,))]
    scratch += [pltpu.VMEM((tm, D_MODEL), BF16),
                pltpu.VMEM((tm, D_FF), BF16)]
    if has_attn:
        scratch += [pltpu.VMEM((tm, D_MODEL), BF16), pltpu.VMEM((tm, D_MODEL), F32),
                    pltpu.VMEM((WINDOW + tm, KV_DIM), F32), pltpu.VMEM((WINDOW + tm, KV_DIM), F32)]
    if has_pool:
        scratch += [pltpu.VMEM((tm, D_MODEL), F32), pltpu.VMEM((tm, D_MODEL), F32),
                    pltpu.VMEM((HALO + tm, D_MODEL), F32), pltpu.VMEM((ms, D_MODEL), F32)]
    if has_kv:
        scratch += [pltpu.VMEM((D_MODEL, 2 * KV_DIM), BF16)]

    return pl.pallas_call(
        functools.partial(_ffn_kernel, nt=nt, tm=tm, ms=ms, per_seq=per_seq, layer=layer, n_cast=n_cast,
                          has_pool=has_pool, has_attn=has_attn, has_kv=has_kv, has_final=has_final,
                          has_qprep=has_qprep),
        grid=(nt + 2,),
        in_specs=specs,
        out_specs=out_specs,
        out_shape=out_shape,
        scratch_shapes=scratch,
        compiler_params=pltpu.CompilerParams(
            dimension_semantics=("arbitrary",), vmem_limit_bytes=VMEM_LIMIT),
        name="ffn",
    )(*args)


HALO = 16
assert POOL_WINDOWS == tuple(2 ** (i + 1) for i in range(len(POOL_WINDOWS))) and max(POOL_WINDOWS) <= HALO


def _pool_rows(t_seq, x_ref, g_ref, w_ref, sc_ref, o_ref, buf_ref, hext_ref, tp):
    keep = jnp.where(t_seq == 0, 0, HALO)
    hrow = lax.broadcasted_iota(jnp.int32, (HALO, D_MODEL), 0)
    hext_ref[0:HALO, :] = jnp.where(hrow < keep, hext_ref[tp:tp + HALO, :], 0.0)
    x = x_ref[...]
    h = _rms(x, g_ref[...])
    hext_ref[HALO:HALO + tp, :] = h
    buf_ref[0] = hext_ref[HALO + tp - POOL_BUF:HALO + tp, :]
    yield
    pos = (t_seq * tp + lax.broadcasted_iota(jnp.int32, (tp, 1), 0)).astype(F32)
    sums, cur, shift = [], hext_ref[...], 1
    for gi in range(len(POOL_WINDOWS)):
        cur = cur + pltpu.roll(cur, shift, axis=0)
        sums.append(cur[HALO:, :POOL_GROUP])
        if gi + 1 < len(POOL_WINDOWS):
            cur, shift = cur[:, POOL_GROUP:], 2 * shift
        yield
    for gi, w in enumerate(POOL_WINDOWS):
        cs = slice(gi * POOL_GROUP, (gi + 1) * POOL_GROUP)
        cnt = jnp.minimum(float(w), pos + 1.0)
        diff = (sums[gi] / cnt - hext_ref[HALO:HALO + tp, cs]).astype(BF16)
        mixed = jnp.dot(diff, w_ref[gi].astype(BF16), preferred_element_type=F32)
        o_ref[:, cs] = x_ref[:, cs] + mixed * sc_ref[:, cs]
        yield


POOL_S_ROWS = 8


def _pool_sample_rows(r0, x_ref, buf_ref, g_ref, w_ref, sc_ref, o_ref, nbuf_ref):
    rows = pl.ds(pl.multiple_of(r0, POOL_S_ROWS), POOL_S_ROWS)
    x = x_ref[rows, :]
    h = _rms(x, g_ref[...])
    nbuf_ref[0:POOL_BUF - 1] = buf_ref[1:POOL_BUF]
    nbuf_ref[POOL_BUF - 1] = h
    yield
    mixed = []
    for gi, w in enumerate(POOL_WINDOWS):
        cs = slice(gi * POOL_GROUP, (gi + 1) * POOL_GROUP)
        s = h[:, cs]
        for k in range(1, w):
            s = s + buf_ref[POOL_BUF - k, :, cs]
        diff = (s / float(w) - h[:, cs]).astype(BF16)
        mixed.append(jnp.dot(diff, w_ref[gi].astype(BF16), preferred_element_type=F32))
    o_ref[rows, :] = x + jnp.concatenate(mixed, axis=1) * sc_ref[...]
    yield


def _seg_mask(rows):
    lane = lax.broadcasted_iota(jnp.int32, (rows, KV_DIM), 1)
    return [(lane >= kv * HEAD_DIM) & (lane < (kv + 1) * HEAD_DIM) for kv in range(N_KV)]


def _attn_rows(nprev, x_ref, g_ref, wq_ref, tabs, k_ref, v_ref, sink_ref, a_ref, kbuf, vbuf, tq):
    hrow = lax.broadcasted_iota(jnp.int32, (WINDOW, KV_DIM), 0)
    kbuf[0:WINDOW, :] = jnp.where(hrow < nprev, kbuf[tq:tq + WINDOW, :], 0.0)
    vbuf[0:WINDOW, :] = jnp.where(hrow < nprev, vbuf[tq:tq + WINDOW, :], 0.0)
    kbuf[WINDOW:WINDOW + tq, :] = k_ref[...]
    vbuf[WINDOW:WINDOW + tq, :] = v_ref[...]

    hn = _rms(x_ref[...], g_ref[...]).astype(BF16)
    q = jnp.dot(hn, wq_ref[...], preferred_element_type=F32)
    q = (_rope(q, *[t[...] for t in tabs]) * (HEAD_DIM ** -0.5)).astype(BF16)
    yield

    seg2 = _seg_mask(2 * WINDOW)
    seg1 = _seg_mask(WINDOW)
    qi = lax.broadcasted_iota(jnp.int32, (WINDOW, WINDOW), 0)
    kj = lax.broadcasted_iota(jnp.int32, (WINDOW, WINDOW), 1)
    causal = kj <= qi
    prev_ok = kj <= qi + nprev

    for i in range(tq // WINDOW):
        kk = kbuf[i * WINDOW:(i + 2) * WINDOW, :]
        vv = vbuf[i * WINDOW:(i + 2) * WINDOW, :]
        kbd = jnp.concatenate([jnp.where(m, kk, 0.0).astype(BF16) for m in seg2], axis=0)
        vbd = jnp.concatenate([jnp.where(m, vv, 0.0).astype(BF16) for m in seg2], axis=0)
        rows = slice(i * WINDOW, (i + 1) * WINDOW)
        qs = jnp.concatenate([q[rows, gm * KV_DIM:(gm + 1) * KV_DIM] for gm in range(GROUP)], axis=0)
        s = lax.dot_general(qs, kbd, (((1,), (1,)), ((), ())), preferred_element_type=F32)
        yield
        pg, rg = [], []
        for gm in range(GROUP):
            ps, rinv = [], jnp.zeros((WINDOW, KV_DIM), F32)
            for kv in range(N_KV):
                s_prev = s[gm * WINDOW:(gm + 1) * WINDOW, kv * 2 * WINDOW:kv * 2 * WINDOW + WINDOW]
                s_own = s[gm * WINDOW:(gm + 1) * WINDOW, kv * 2 * WINDOW + WINDOW:(kv + 1) * 2 * WINDOW]
                sc = jnp.where(causal, s_own, s_prev)
                if i == 0:
                    sc = jnp.where(prev_ok, sc, -jnp.inf)
                sink = sink_ref[gm, kv]
                mx = jnp.maximum(jnp.max(sc, axis=-1, keepdims=True), sink)
                p = jnp.exp(sc - mx)
                den = jnp.sum(p, axis=-1, keepdims=True) + jnp.exp(sink - mx)
                ps.append(jnp.where(causal, 0.0, p).astype(BF16))
                ps.append(jnp.where(causal, p, 0.0).astype(BF16))
                rinv = jnp.where(seg1[kv], 1.0 / den, rinv)
            pg.append(jnp.concatenate(ps, axis=1))
            rg.append(rinv)
            if gm % 2 == 1:
                yield
        og = jnp.dot(jnp.concatenate(pg, axis=0), vbd, preferred_element_type=F32)
        for gm in range(GROUP):
            a_ref[rows, gm * KV_DIM:(gm + 1) * KV_DIM] = (
                og[gm * WINDOW:(gm + 1) * WINDOW] * rg[gm]).astype(a_ref.dtype)
        yield


def _attn_sample_kernel(q_ref, kt_ref, vt_ref, knt_ref, vnt_ref, sink_ref, o_ref, ko_ref, vo_ref, *, bb):
    i = pl.program_id(0)
    lane = lax.broadcasted_iota(jnp.int32, (HEAD_DIM, WINDOW), 1)
    col = lax.broadcasted_iota(jnp.int32, (bb * N_HEADS, N_KV * WINDOW), 1)
    row = lax.broadcasted_iota(jnp.int32, (bb * N_HEADS, N_KV * WINDOW), 0)
    own = (col // WINDOW) == (row % N_KV)
    sink = jnp.concatenate([sink_ref[...]] * bb, axis=0)

    def shifted(src_ref, new_t_ref, out_ref, j):
        new = pltpu.roll(new_t_ref[...], (WINDOW - 1) - (i * bb + j), axis=1)
        parts = []
        for kv in range(N_KV):
            hs = slice(kv * HEAD_DIM, (kv + 1) * HEAD_DIM)
            slab = jnp.where(lane == WINDOW - 1, new[hs], pltpu.roll(src_ref[j, kv], WINDOW - 1, axis=1))
            out_ref[j, kv] = slab
            parts.append(slab.astype(BF16))
        return jnp.concatenate(parts, axis=1)

    s = jnp.concatenate(
        [jnp.dot(q_ref[j].astype(BF16), shifted(kt_ref, knt_ref, ko_ref, j), preferred_element_type=F32)
         for j in range(bb)], axis=0)
    s = jnp.where(own, s, -jnp.inf)
    mx = jnp.maximum(jnp.max(s, axis=-1, keepdims=True), sink)
    p = jnp.exp(s - mx)
    den = jnp.sum(p, axis=-1, keepdims=True) + jnp.exp(sink - mx)
    p = p.astype(BF16)
    for j in range(bb):
        rs = slice(j * N_HEADS, (j + 1) * N_HEADS)
        vcat = shifted(vt_ref, vnt_ref, vo_ref, j)
        o = lax.dot_general(p[rs], vcat, (((1,), (1,)), ((), ())), preferred_element_type=F32)
        o_ref[j] = o / den[rs]


def _attn_sample(q3, kt, vt, knt, vnt, sinks, *, bb):
    b = q3.shape[0]
    blk4 = pl.BlockSpec((bb, N_KV, HEAD_DIM, WINDOW), lambda i: (i, 0, 0, 0))
    blk3 = pl.BlockSpec((bb, N_HEADS, HEAD_DIM), lambda i: (i, 0, 0))
    return pl.pallas_call(
        functools.partial(_attn_sample_kernel, bb=bb),
        grid=(b // bb,),
        in_specs=[blk3, blk4, blk4, _const_spec(knt.shape), _const_spec(vnt.shape), _const_spec(sinks.shape)],
        out_specs=[blk3, blk4, blk4],
        out_shape=[jax.ShapeDtypeStruct(q3.shape, F32), jax.ShapeDtypeStruct(kt.shape, F32),
                   jax.ShapeDtypeStruct(vt.shape, F32)],
        compiler_params=pltpu.CompilerParams(
            dimension_semantics=("arbitrary",), vmem_limit_bytes=VMEM_LIMIT),
        name="attn_sample",
    )(q3, kt, vt, knt, vnt, sinks)


def kernel(x_prompt, x_sample, state_pool, cache_k_win, cache_v_win, norm_g, ffn_w_gate, ffn_w_up,
           ffn_w_down, pool_w, pool_scale, kv_norm_g, w_kv, w_q, w_o, attn_sinks, final_norm_g):
    bp, seq, _ = x_prompt.shape
    bs = x_sample.shape[0]
    tm = TILE_ROWS
    assert seq % tm == 0 and tm % WINDOW == 0 and bs == LANES and bs % SAMPLE_ATTN_SEQS == 0
    assert x_prompt.shape[2] == D_MODEL and cache_k_win.shape[1:] == (WINDOW, N_KV, HEAD_DIM)
    g = lambda l, i: norm_g[l, i].reshape(1, D_MODEL)
    ffn_w = (ffn_w_gate.reshape(N_FFN, D_MODEL, D_FF), ffn_w_up.reshape(N_FFN, D_MODEL, D_FF),
             ffn_w_down.reshape(N_FFN, D_FF, D_MODEL))
    psc = pool_scale[0].reshape(1, D_MODEL)
    kvg = kv_norm_g.reshape(1, D_MODEL)
    sinks_gk = attn_sinks[0].reshape(N_KV, GROUP).T
    gfin = final_norm_g.reshape(1, D_MODEL)
    tabs_p = _rope_tables(np.arange(seq))
    tabs_s = _rope_tables(np.full((1,), PAST_LEN))

    x = x_prompt.reshape(bp * seq, D_MODEL)
    xs = x_sample.reshape(bs, D_MODEL)

    x, xs, *w_bf16 = _ffn(x, xs, g(0, 0), (*ffn_w, 0), tm=tm, seq=seq, cast_layers=tuple(range(1, N_FFN)))
    w1, w2, w3 = w_bf16[0:2], w_bf16[2:4], w_bf16[4:6]
    buf_t = jnp.transpose(state_pool[0], (1, 0, 2))
    x, xs, pool_p, nbuf_t = _ffn(x, xs, g(0, 2), w1, tm=tm, seq=seq, pool=(g(0, 1), pool_w[0], psc, buf_t))
    x, xs, k_p, v_p, k_s, v_s, wq, wo, q_s = _ffn(x, xs, g(1, 0), w2, tm=tm, seq=seq,
                                                  kv_g=kvg, wkv=w_kv, tabs_p=tabs_p, tabs_s=tabs_s,
                                                  qprep=(w_q[0], w_o[0], g(1, 1)))
    kt = jnp.transpose(cache_k_win, (0, 2, 3, 1))
    vt = jnp.transpose(cache_v_win, (0, 2, 3, 1))
    o_s, kt_new, vt_new = _attn_sample(q_s.reshape(bs, N_HEADS, HEAD_DIM), kt, vt, k_s.T, v_s.T,
                                       sinks_gk.reshape(N_HEADS, 1), bb=SAMPLE_ATTN_SEQS)
    attn_s = o_s.reshape(bs, D_MODEL).astype(BF16)
    y_p, y_s = _ffn(x, xs, g(1, 2), w3, tm=tm, seq=seq, tabs_p=tabs_p, final_g=gfin,
                    attn=(g(1, 1), wq, k_p, v_p, sinks_gk, attn_s, wo))

    y_prompt = y_p.reshape(bp, seq, D_MODEL)
    y_sample = y_s.reshape(bs, 1, D_MODEL)
    pool_prompt = pool_p[None]
    pool_sample = jnp.transpose(nbuf_t, (1, 0, 2))[None]
    k_win_prompt = k_p.reshape(bp, seq, KV_DIM)[:, seq - WINDOW:].reshape(bp, WINDOW, N_KV, HEAD_DIM)
    v_win_prompt = v_p.reshape(bp, seq, KV_DIM)[:, seq - WINDOW:].reshape(bp, WINDOW, N_KV, HEAD_DIM)
    k_win_sample = jnp.transpose(kt_new, (0, 3, 1, 2))
    v_win_sample = jnp.transpose(vt_new, (0, 3, 1, 2))
    return (y_prompt, y_sample, pool_prompt, pool_sample, k_win_prompt, v_win_prompt,
            k_win_sample, v_win_sample)
```

```python
import functools

import numpy as np
import jax
import jax.numpy as jnp
from jax import lax
from jax.experimental import pallas as pl
from jax.experimental.pallas import tpu as pltpu

F32 = jnp.float32
BF16 = jnp.bfloat16

D_MODEL = 1024
D_FF = 2816
HEAD_DIM = 64
N_HEADS = 16
N_KV = 4
GROUP = 4
KV_DIM = N_KV * HEAD_DIM
WINDOW = 128
ROT_DIM = 16
ROPE_THETA = 500000.0
EPS = 1e-5
POOL_WINDOWS = (2, 4, 8, 16)
POOL_GROUP = 256
POOL_BUF = 15
PAST_LEN = 16384

N_FFN = 4
LANES = 128
TILE_ROWS = 512
SAMPLE_ATTN_SEQS = 4
FF_CHUNK = 256
N_FF_CHUNKS = D_FF // FF_CHUNK
OUT_CHUNK = 256
N_OUT_CHUNKS = D_MODEL // OUT_CHUNK
VMEM_LIMIT = 60 * 1024 * 1024

GU_SLABS = 8
GU_ROWS = D_MODEL // GU_SLABS
DN_SLABS = 8
DN_ROWS = D_FF // DN_SLABS
STAGE_SLOTS = 4

CAST_GU_ROWS = 32
CAST_DN_ROWS = 128


def _const_spec(shape):
    nd = len(shape)
    return pl.BlockSpec(shape, lambda *_: (0,) * nd, pipeline_mode=pl.Buffered(1))


def _rms(x, g):
    ms = jnp.mean(x * x, axis=-1, keepdims=True)
    return x * lax.rsqrt(ms + EPS) * g


def _rope(x, c, s1, s2):
    pieces = []
    for j in range(x.shape[1] // LANES):
        xc = x[:, j * LANES:(j + 1) * LANES]
        up = pltpu.roll(xc, LANES - ROT_DIM // 2, axis=1)
        dn = pltpu.roll(xc, ROT_DIM // 2, axis=1)
        pieces.append(xc * c + up * s1 + dn * s2)
    return jnp.concatenate(pieces, axis=1) if len(pieces) > 1 else pieces[0]


def _rope_tables(pos):
    half = ROT_DIM // 2
    inv = np.power(ROPE_THETA, -np.arange(half, dtype=np.float64) * (2.0 / ROT_DIM))
    ang = np.asarray(pos, np.float64)[:, None] * inv[None, :]
    cos, sin = np.cos(ang), np.sin(ang)
    n = ang.shape[0]
    ones = np.ones((n, HEAD_DIM - ROT_DIM))
    zeros8 = np.zeros((n, half))
    zeros = np.zeros((n, HEAD_DIM - ROT_DIM))
    c = np.concatenate([cos, cos, ones], axis=1)
    s1 = np.concatenate([-sin, zeros8, zeros], axis=1)
    s2 = np.concatenate([zeros8, sin, zeros], axis=1)
    return [jnp.asarray(np.concatenate([t, t], axis=1), F32) for t in (c, s1, s2)]


def _stage_ffn_weights(wg_hbm, wu_hbm, wd_hbm, wgu_scr, wd_scr, gu_stage, dn_stage, gu_sem, dn_sem):
    chunks = []
    for src, col0 in ((wg_hbm, 0), (wu_hbm, FF_CHUNK)):
        for r in range(GU_SLABS):
            chunks.append(("gu", src, r, col0))
    for r in range(DN_SLABS):
        chunks.append(("dn", wd_hbm, r, 0))

    def copy(c):
        kind, src, r, _ = chunks[c]
        slot = c % STAGE_SLOTS
        if kind == "gu":
            return pltpu.make_async_copy(src.at[pl.ds(r * GU_ROWS, GU_ROWS), :], gu_stage.at[slot], gu_sem.at[slot])
        return pltpu.make_async_copy(src.at[pl.ds(r * DN_ROWS, DN_ROWS), :], dn_stage.at[slot], dn_sem.at[slot])

    for c in range(STAGE_SLOTS):
        copy(c).start()
    for c, (kind, _, r, col0) in enumerate(chunks):
        slot = c % STAGE_SLOTS
        copy(c).wait()
        if kind == "gu":
            for j in range(N_FF_CHUNKS):
                wgu_scr[j, r * GU_ROWS:(r + 1) * GU_ROWS, col0:col0 + FF_CHUNK] = (
                    gu_stage[slot, :, j * FF_CHUNK:(j + 1) * FF_CHUNK].astype(BF16))
        else:
            wd_scr[r * DN_ROWS:(r + 1) * DN_ROWS, :] = dn_stage[slot].astype(BF16)
        if c + STAGE_SLOTS < len(chunks):
            copy(c + STAGE_SLOTS).start()


def _run(*phase_generators):
    live = list(phase_generators)
    while live:
        for gen in list(live):
            try:
                next(gen)
            except StopIteration:
                live.remove(gen)


def _kv_rows(rows, src_ref, kvg_ref, wkv_scr, tabs, k_ref, v_ref):
    kvn = _rms(src_ref[0:rows, :], kvg_ref[...]).astype(BF16)
    yield
    kv = jnp.dot(kvn, wkv_scr[...], preferred_element_type=F32)
    yield
    tb = [t[...] for t in tabs]
    if tb[0].shape[0] != rows:
        tb = [jnp.broadcast_to(t, (rows, LANES)) for t in tb]
    k_ref[...] = _rope(kv[:, :KV_DIM], *tb)
    v_ref[...] = kv[:, KV_DIM:]
    yield


def _ffn_rows(rows, x_ref, a_ref, g_ref, gf_ref, o_ref, wgu_scr, wd_scr, wo_ref, xn_ref, h_ref, xr_ref):
    rs = slice(0, rows)
    if a_ref is not None:
        xr_ref[rs, :] = x_ref[...] + jnp.dot(a_ref[...].astype(BF16), wo_ref[...], preferred_element_type=F32)
    elif xr_ref is not None:
        xr_ref[rs, :] = x_ref[...]
    res = (lambda sl: xr_ref[rs, sl]) if xr_ref is not None else (lambda sl: x_ref[:, sl])
    xn_ref[rs, :] = _rms(res(slice(None)), g_ref[...]).astype(BF16)
    yield
    for j in range(N_FF_CHUNKS):
        r = jnp.dot(xn_ref[rs, :], wgu_scr[j], preferred_element_type=F32)
        gate, up = r[:, :FF_CHUNK], r[:, FF_CHUNK:]
        h_ref[rs, j * FF_CHUNK:(j + 1) * FF_CHUNK] = (gate * jax.nn.sigmoid(gate) * up).astype(BF16)
        yield
    for n in range(N_OUT_CHUNKS):
        sl = slice(n * OUT_CHUNK, (n + 1) * OUT_CHUNK)
        y = jnp.dot(h_ref[rs, :], wd_scr[:, sl], preferred_element_type=F32)
        o_ref[:, sl] = res(sl) + 0.5 * y
        yield
    if gf_ref is not None:
        o_ref[...] = _rms(o_ref[...], gf_ref[...])


def _cast_slabs(wg_in, wu_in, wd_in, wgu_out, wd_out):
    for j in range(N_FF_CHUNKS):
        cs = slice(j * FF_CHUNK, (j + 1) * FF_CHUNK)
        wgu_out[j, :, 0:FF_CHUNK] = wg_in[:, cs].astype(BF16)
        wgu_out[j, :, FF_CHUNK:2 * FF_CHUNK] = wu_in[:, cs].astype(BF16)
    wd_out[...] = wd_in[...].astype(BF16)
    yield


def _permute_qo(wq_ref, wo_ref, wqp_ref, wop_ref):
    lane = lax.broadcasted_iota(jnp.int32, (D_MODEL, LANES), 1)
    src_head = lambda h: (h % N_KV) * GROUP + h // N_KV
    for c in range(D_MODEL // LANES):
        halves = []
        for half in (0, 1):
            hs = src_head(2 * c + half)
            col = wq_ref[:, (hs // 2) * LANES:(hs // 2 + 1) * LANES]
            halves.append(col if hs % 2 == half else pltpu.roll(col, HEAD_DIM, axis=1))
        wqp_ref[:, c * LANES:(c + 1) * LANES] = jnp.where(lane < HEAD_DIM, halves[0], halves[1]).astype(BF16)
    for h in range(N_HEADS):
        hs = src_head(h)
        wop_ref[h * HEAD_DIM:(h + 1) * HEAD_DIM, :] = wo_ref[hs * HEAD_DIM:(hs + 1) * HEAD_DIM, :].astype(BF16)


def _ffn_kernel(*refs, nt, tm, ms, per_seq, layer, n_cast, has_pool, has_attn, has_kv, has_final, has_qprep):
    it = iter(refs)
    x_ref, xs_ref = next(it), next(it)
    a_ref = as_ref = wo_ref = gf_ref = kvg_ref = wkv_ref = None
    tabs_p = tabs_s = None
    if has_pool:
        gp_ref, pw_ref, psc_ref, sbuf_ref = next(it), next(it), next(it), next(it)
    if has_attn:
        xq_ref, gq_ref, wq_ref = next(it), next(it), next(it)
        tabs_q = [next(it) for _ in range(3)]
        kq_ref, vq_ref, sink_ref, wo_ref = [next(it) for _ in range(4)]
        q3_ref, kt_ref, vt_ref, knt_ref, vnt_ref, sinkc_ref = [next(it) for _ in range(6)]
    g_ref = next(it)
    if layer is not None:
        wg_hbm, wu_hbm, wd_hbm = next(it), next(it), next(it)
    else:
        wgu_hbm, wdn_hbm = next(it), next(it)
    cast_in = [[next(it) for _ in range(3)] for _ in range(n_cast)]
    if has_kv:
        kvg_ref, wkv_ref = next(it), next(it)
        tabs_p = [next(it) for _ in range(3)]
        tabs_s = [next(it) for _ in range(3)]
    if has_final:
        gf_ref = next(it)
    if has_qprep:
        wqf_ref, wof_ref, gqs_ref = next(it), next(it), next(it)
        tabs_qs = [next(it) for _ in range(3)]
    o_ref, os_ref = next(it), next(it)
    if has_attn:
        ko_ref, vo_ref = next(it), next(it)
    k_ref = v_ref = ks_ref = vs_ref = None
    if has_kv:
        k_ref, v_ref, ks_ref, vs_ref = next(it), next(it), next(it), next(it)
    cast_out = [[next(it) for _ in range(2)] for _ in range(n_cast)]
    if has_pool:
        pbuf_ref, snbuf_ref = next(it), next(it)
    if has_qprep:
        wqp_ref, wop_ref, qs_ref = next(it), next(it), next(it)
    wgu_scr, wd_scr = next(it), next(it)
    if layer is not None:
        gu_stage, dn_stage, gu_sem, dn_sem = [next(it) for _ in range(4)]
    else:
        w_sem = next(it)
    xn_ref, h_ref = next(it), next(it)
    xr_ref = wkv_scr = None
    if has_attn:
        a_ref, xr_ref, kbuf, vbuf, as_ref = [next(it) for _ in range(5)]
    if has_pool:
        x2_scr, xr_ref, hext_ref, xs2_scr = [next(it) for _ in range(4)]
    if has_kv:
        wkv_scr = next(it)

    i = pl.program_id(0)

    def pool(tile):
        return _pool_rows(tile % per_seq, x_ref, gp_ref, pw_ref, psc_ref, x2_scr, pbuf_ref, hext_ref, tm)

    def attention(tile):
        nprev = jnp.where(tile % per_seq == 0, 0, WINDOW)
        return _attn_rows(nprev, xq_ref, gq_ref, wq_ref, tabs_q, kq_ref, vq_ref, sink_ref, a_ref,
                          kbuf, vbuf, tm)

    @pl.when(i == 0)
    def _():
        if layer is not None:
            _stage_ffn_weights(wg_hbm.at[layer], wu_hbm.at[layer], wd_hbm.at[layer], wgu_scr, wd_scr,
                               gu_stage, dn_stage, gu_sem, dn_sem)
        else:
            copies = [pltpu.make_async_copy(wgu_hbm, wgu_scr, w_sem.at[0]),
                      pltpu.make_async_copy(wdn_hbm, wd_scr, w_sem.at[1])]
            for c in copies:
                c.start()
        if has_attn:
            kbuf[...] = jnp.zeros(kbuf.shape, F32)
            vbuf[...] = jnp.zeros(vbuf.shape, F32)
            _run(attention(0))
        if has_pool:
            hext_ref[...] = jnp.zeros(hext_ref.shape, F32)
            _run(pool(0))
        if has_kv:
            wkv_scr[...] = wkv_ref[...].astype(BF16)
        if has_qprep:
            _permute_qo(wqf_ref, wof_ref, wqp_ref, wop_ref)
        if layer is None:
            for c in copies:
                c.wait()

    common = (wgu_scr, wd_scr, wo_ref, xn_ref, h_ref, xr_ref)

    @pl.when((i >= 1) & (i <= nt))
    def _():
        side = [_cast_slabs(*cast_in[c], *cast_out[c]) for c in range(n_cast)]
        if has_kv:
            side.append(_kv_rows(tm, x_ref, kvg_ref, wkv_scr, tabs_p, k_ref, v_ref))
        if has_attn:
            side.append(attention(jnp.minimum(i, nt - 1)))
            side.append(_attn_sample_rows((i - 1) * SAMPLE_ATTN_SEQS, q3_ref, kt_ref, vt_ref, knt_ref, vnt_ref,
                                          sinkc_ref, ko_ref, vo_ref, as_ref))
        if has_pool:
            side.append(pool(jnp.minimum(i, nt - 1)))
            blk = (i - 1) * (ms // POOL_S_ROWS) // nt
            side.append(_pool_sample_rows(blk * POOL_S_ROWS, xs_ref, sbuf_ref, gp_ref, pw_ref, psc_ref,
                                          xs2_scr, snbuf_ref))
        xin = x2_scr if has_pool else x_ref
        _run(_ffn_rows(tm, xin, a_ref, g_ref, gf_ref, o_ref, *common), *side)

    @pl.when(i == nt + 1)
    def _():
        side = [_kv_rows(ms, xs_ref, kvg_ref, wkv_scr, tabs_s, ks_ref, vs_ref)] if has_kv else []
        xsin = xs2_scr if has_pool else xs_ref
        _run(_ffn_rows(ms, xsin, as_ref, g_ref, gf_ref, os_ref, *common), *side)
        if has_qprep:
            qn = _rms(os_ref[...], gqs_ref[...]).astype(BF16)
            y = jnp.dot(qn, wqp_ref[...], preferred_element_type=F32)
            tb = [jnp.broadcast_to(t[...], (ms, LANES)) for t in tabs_qs]
            qs_ref[...] = _rope(y, *tb) * (HEAD_DIM ** -0.5)


def _ffn(x, xs, g, weights, *, tm, seq, cast_layers=(), pool=None, attn=None, kv_g=None, wkv=None,
         tabs_p=None, tabs_s=None, final_g=None, qprep=None):
    mp, ms = x.shape[0], xs.shape[0]
    nt = mp // tm
    per_seq = seq // tm
    layer = weights[3] if len(weights) == 4 else None
    n_cast = len(cast_layers)
    assert n_cast == 0 or (nt * CAST_GU_ROWS == D_MODEL and nt * CAST_DN_ROWS >= D_FF)
    has_attn, has_kv, has_final = attn is not None, kv_g is not None, final_g is not None
    has_pool = pool is not None
    assert not (has_kv and has_pool)
    tile = lambda i: jnp.clip(i - 1, 0, nt - 1)
    nxt = lambda i: jnp.minimum(i, nt - 1)
    row = lambda w: pl.BlockSpec((tm, w), lambda i: (tile(i), 0))
    nrow = lambda w: pl.BlockSpec((tm, w), lambda i: (nxt(i), 0))
    hbm = pl.BlockSpec(memory_space=pl.ANY)

    args, specs = [x, xs], [nrow(D_MODEL) if has_pool else row(D_MODEL), _const_spec(xs.shape)]
    sblk = pl.BlockSpec((POOL_BUF, POOL_S_ROWS, D_MODEL), lambda i: (0, tile(i) * (ms // POOL_S_ROWS) // nt, 0))
    if has_pool:
        args += list(pool)
        specs += [_const_spec(a.shape) for a in pool[:3]] + [sblk]
    if has_attn:
        gq, wq, k, v, sinks, wo, q3, kt, vt, knt, vnt = attn
        assert ms == nt * SAMPLE_ATTN_SEQS
        sinks_col = sinks.reshape(N_HEADS, 1)
        cblk = pl.BlockSpec((SAMPLE_ATTN_SEQS, N_KV, HEAD_DIM, WINDOW), lambda i: (tile(i), 0, 0, 0))
        args += [x, gq, wq, *tabs_p, k, v, sinks, wo, q3, kt, vt, knt, vnt, sinks_col]
        specs += [nrow(D_MODEL), _const_spec(gq.shape), _const_spec(wq.shape)]
        specs += [pl.BlockSpec((tm, LANES), lambda i: (nxt(i) % per_seq, 0))] * 3
        specs += [nrow(KV_DIM), nrow(KV_DIM), pl.BlockSpec(memory_space=pltpu.SMEM), _const_spec(wo.shape)]
        specs += [_const_spec(q3.shape), cblk, cblk, _const_spec(knt.shape), _const_spec(vnt.shape),
                  _const_spec(sinks_col.shape)]
    args += [g, *weights[:3]] if layer is not None else [g, *weights]
    specs += [_const_spec(g.shape)] + [hbm] * (3 if layer is not None else 2)
    dn_blocks = D_FF // CAST_DN_ROWS
    dn_tile = lambda i: jnp.minimum(tile(i), dn_blocks - 1)
    for cl in cast_layers:
        args += list(weights[:3])
        specs += [pl.BlockSpec((None, CAST_GU_ROWS, D_FF), lambda i, cl=cl: (cl, tile(i), 0)),
                  pl.BlockSpec((None, CAST_GU_ROWS, D_FF), lambda i, cl=cl: (cl, tile(i), 0)),
                  pl.BlockSpec((None, CAST_DN_ROWS, D_MODEL), lambda i, cl=cl: (cl, dn_tile(i), 0))]
    if has_kv:
        args += [kv_g, wkv, *tabs_p, *tabs_s]
        specs += [_const_spec(kv_g.shape), _const_spec(wkv.shape)]
        specs += [pl.BlockSpec((tm, LANES), lambda i: (tile(i) % per_seq, 0))] * 3
        specs += [_const_spec((1, LANES))] * 3
    if has_final:
        args.append(final_g)
        specs.append(_const_spec(final_g.shape))
    has_qprep = qprep is not None
    if has_qprep:
        args += [*qprep, *tabs_s]
        specs += [_const_spec(a.shape) for a in qprep] + [_const_spec((1, LANES))] * 3

    out_shape = [jax.ShapeDtypeStruct((mp, D_MODEL), F32), jax.ShapeDtypeStruct((ms, D_MODEL), F32)]
    out_specs = [row(D_MODEL), _const_spec((ms, D_MODEL))]
    if has_attn:
        out_shape += [jax.ShapeDtypeStruct(kt.shape, F32), jax.ShapeDtypeStruct(vt.shape, F32)]
        out_specs += [cblk, cblk]
    if has_kv:
        out_shape += [jax.ShapeDtypeStruct((mp, KV_DIM), F32)] * 2 + [jax.ShapeDtypeStruct((ms, KV_DIM), F32)] * 2
        out_specs += [row(KV_DIM)] * 2 + [_const_spec((ms, KV_DIM))] * 2
    wgu_shape, wdn_shape = (N_FF_CHUNKS, D_MODEL, 2 * FF_CHUNK), (D_FF, D_MODEL)
    for _ in cast_layers:
        out_shape += [jax.ShapeDtypeStruct(wgu_shape, BF16), jax.ShapeDtypeStruct(wdn_shape, BF16)]
        out_specs += [pl.BlockSpec((N_FF_CHUNKS, CAST_GU_ROWS, 2 * FF_CHUNK), lambda i: (0, tile(i), 0)),
                      pl.BlockSpec((CAST_DN_ROWS, D_MODEL), lambda i: (dn_tile(i), 0))]
    if has_pool:
        out_shape += [jax.ShapeDtypeStruct((mp // seq, POOL_BUF, D_MODEL), F32),
                      jax.ShapeDtypeStruct(pool[3].shape, F32)]
        out_specs += [pl.BlockSpec((1, POOL_BUF, D_MODEL), lambda i: (nxt(i) // per_seq, 0, 0)), sblk]
    if has_qprep:
        out_shape += [jax.ShapeDtypeStruct((D_MODEL, D_MODEL), BF16)] * 2 + [jax.ShapeDtypeStruct((ms, D_MODEL), F32)]
        out_specs += [_const_spec((D_MODEL, D_MODEL))] * 2 + [_const_spec((ms, D_MODEL))]

    scratch = [pltpu.VMEM(wgu_shape, BF16), pltpu.VMEM(wdn_shape, BF16)]
    if layer is not None:
        scratch += [pltpu.VMEM((STAGE_SLOTS, GU_ROWS, D_FF), F32),
                    pltpu.VMEM((STAGE_SLOTS, DN_ROWS, D_MODEL), F32),
                    pltpu.SemaphoreType.DMA((STAGE_SLOTS,)),
                    pltpu.SemaphoreType.DMA((STAGE_SLOTS,))]
    else:
        scratch += [pltpu.SemaphoreType.DMA((2,))]
    scratch += [pltpu.VMEM((tm, D_MODEL), BF16),
                pltpu.VMEM((tm, D_FF), BF16)]
    if has_attn:
        scratch += [pltpu.VMEM((tm, D_MODEL), BF16), pltpu.VMEM((tm, D_MODEL), F32),
                    pltpu.VMEM((WINDOW + tm, KV_DIM), F32), pltpu.VMEM((WINDOW + tm, KV_DIM), F32),
                    pltpu.VMEM((ms, D_MODEL), F32)]
    if has_pool:
        scratch += [pltpu.VMEM((tm, D_MODEL), F32), pltpu.VMEM((tm, D_MODEL), F32),
                    pltpu.VMEM((HALO + tm, D_MODEL), F32), pltpu.VMEM((ms, D_MODEL), F32)]
    if has_kv:
        scratch += [pltpu.VMEM((D_MODEL, 2 * KV_DIM), BF16)]

    return pl.pallas_call(
        functools.partial(_ffn_kernel, nt=nt, tm=tm, ms=ms, per_seq=per_seq, layer=layer, n_cast=n_cast,
                          has_pool=has_pool, has_attn=has_attn, has_kv=has_kv, has_final=has_final,
                          has_qprep=has_qprep),
        grid=(nt + 2,),
        in_specs=specs,
        out_specs=out_specs,
        out_shape=out_shape,
        scratch_shapes=scratch,
        compiler_params=pltpu.CompilerParams(
            dimension_semantics=("arbitrary",), vmem_limit_bytes=VMEM_LIMIT),
        name="ffn",
    )(*args)


HALO = 16
assert POOL_WINDOWS == tuple(2 ** (i + 1) for i in range(len(POOL_WINDOWS))) and max(POOL_WINDOWS) <= HALO


def _pool_rows(t_seq, x_ref, g_ref, w_ref, sc_ref, o_ref, buf_ref, hext_ref, tp):
    keep = jnp.where(t_seq == 0, 0, HALO)
    hrow = lax.broadcasted_iota(jnp.int32, (HALO, D_MODEL), 0)
    hext_ref[0:HALO, :] = jnp.where(hrow < keep, hext_ref[tp:tp + HALO, :], 0.0)
    x = x_ref[...]
    h = _rms(x, g_ref[...])
    hext_ref[HALO:HALO + tp, :] = h
    buf_ref[0] = hext_ref[HALO + tp - POOL_BUF:HALO + tp, :]
    yield
    pos = (t_seq * tp + lax.broadcasted_iota(jnp.int32, (tp, 1), 0)).astype(F32)
    sums, cur, shift = [], hext_ref[...], 1
    for gi in range(len(POOL_WINDOWS)):
        cur = cur + pltpu.roll(cur, shift, axis=0)
        sums.append(cur[HALO:, :POOL_GROUP])
        if gi + 1 < len(POOL_WINDOWS):
            cur, shift = cur[:, POOL_GROUP:], 2 * shift
        yield
    for gi, w in enumerate(POOL_WINDOWS):
        cs = slice(gi * POOL_GROUP, (gi + 1) * POOL_GROUP)
        cnt = jnp.minimum(float(w), pos + 1.0)
        diff = (sums[gi] / cnt - hext_ref[HALO:HALO + tp, cs]).astype(BF16)
        mixed = jnp.dot(diff, w_ref[gi].astype(BF16), preferred_element_type=F32)
        o_ref[:, cs] = x_ref[:, cs] + mixed * sc_ref[:, cs]
        yield


POOL_S_ROWS = 8


def _pool_sample_rows(r0, x_ref, buf_ref, g_ref, w_ref, sc_ref, o_ref, nbuf_ref):
    rows = pl.ds(pl.multiple_of(r0, POOL_S_ROWS), POOL_S_ROWS)
    x = x_ref[rows, :]
    h = _rms(x, g_ref[...])
    nbuf_ref[0:POOL_BUF - 1] = buf_ref[1:POOL_BUF]
    nbuf_ref[POOL_BUF - 1] = h
    yield
    mixed = []
    for gi, w in enumerate(POOL_WINDOWS):
        cs = slice(gi * POOL_GROUP, (gi + 1) * POOL_GROUP)
        s = h[:, cs]
        for k in range(1, w):
            s = s + buf_ref[POOL_BUF - k, :, cs]
        diff = (s / float(w) - h[:, cs]).astype(BF16)
        mixed.append(jnp.dot(diff, w_ref[gi].astype(BF16), preferred_element_type=F32))
    o_ref[rows, :] = x + jnp.concatenate(mixed, axis=1) * sc_ref[...]
    yield


def _seg_mask(rows):
    lane = lax.broadcasted_iota(jnp.int32, (rows, KV_DIM), 1)
    return [(lane >= kv * HEAD_DIM) & (lane < (kv + 1) * HEAD_DIM) for kv in range(N_KV)]


def _attn_rows(nprev, x_ref, g_ref, wq_ref, tabs, k_ref, v_ref, sink_ref, a_ref, kbuf, vbuf, tq):
    hrow = lax.broadcasted_iota(jnp.int32, (WINDOW, KV_DIM), 0)
    kbuf[0:WINDOW, :] = jnp.where(hrow < nprev, kbuf[tq:tq + WINDOW, :], 0.0)
    vbuf[0:WINDOW, :] = jnp.where(hrow < nprev, vbuf[tq:tq + WINDOW, :], 0.0)
    kbuf[WINDOW:WINDOW + tq, :] = k_ref[...]
    vbuf[WINDOW:WINDOW + tq, :] = v_ref[...]

    hn = _rms(x_ref[...], g_ref[...]).astype(BF16)
    q = jnp.dot(hn, wq_ref[...], preferred_element_type=F32)
    q = (_rope(q, *[t[...] for t in tabs]) * (HEAD_DIM ** -0.5)).astype(BF16)
    yield

    seg2 = _seg_mask(2 * WINDOW)
    seg1 = _seg_mask(WINDOW)
    qi = lax.broadcasted_iota(jnp.int32, (WINDOW, WINDOW), 0)
    kj = lax.broadcasted_iota(jnp.int32, (WINDOW, WINDOW), 1)
    causal = kj <= qi
    prev_ok = kj <= qi + nprev

    for i in range(tq // WINDOW):
        kk = kbuf[i * WINDOW:(i + 2) * WINDOW, :]
        vv = vbuf[i * WINDOW:(i + 2) * WINDOW, :]
        kbd = jnp.concatenate([jnp.where(m, kk, 0.0).astype(BF16) for m in seg2], axis=0)
        vbd = jnp.concatenate([jnp.where(m, vv, 0.0).astype(BF16) for m in seg2], axis=0)
        rows = slice(i * WINDOW, (i + 1) * WINDOW)
        qs = jnp.concatenate([q[rows, gm * KV_DIM:(gm + 1) * KV_DIM] for gm in range(GROUP)], axis=0)
        s = lax.dot_general(qs, kbd, (((1,), (1,)), ((), ())), preferred_element_type=F32)
        yield
        pg, rg = [], []
        for gm in range(GROUP):
            ps, rinv = [], jnp.zeros((WINDOW, KV_DIM), F32)
            for kv in range(N_KV):
                s_prev = s[gm * WINDOW:(gm + 1) * WINDOW, kv * 2 * WINDOW:kv * 2 * WINDOW + WINDOW]
                s_own = s[gm * WINDOW:(gm + 1) * WINDOW, kv * 2 * WINDOW + WINDOW:(kv + 1) * 2 * WINDOW]
                sc = jnp.where(causal, s_own, s_prev)
                if i == 0:
                    sc = jnp.where(prev_ok, sc, -jnp.inf)
                sink = sink_ref[gm, kv]
                mx = jnp.maximum(jnp.max(sc, axis=-1, keepdims=True), sink)
                p = jnp.exp(sc - mx)
                den = jnp.sum(p, axis=-1, keepdims=True) + jnp.exp(sink - mx)
                ps.append(jnp.where(causal, 0.0, p).astype(BF16))
                ps.append(jnp.where(causal, p, 0.0).astype(BF16))
                rinv = jnp.where(seg1[kv], 1.0 / den, rinv)
            pg.append(jnp.concatenate(ps, axis=1))
            rg.append(rinv)
            if gm % 2 == 1:
                yield
        og = jnp.dot(jnp.concatenate(pg, axis=0), vbd, preferred_element_type=F32)
        for gm in range(GROUP):
            a_ref[rows, gm * KV_DIM:(gm + 1) * KV_DIM] = (
                og[gm * WINDOW:(gm + 1) * WINDOW] * rg[gm]).astype(a_ref.dtype)
        yield


def _attn_sample_rows(first, q_ref, kt_ref, vt_ref, knt_ref, vnt_ref, sink_ref, ko_ref, vo_ref, a_ref):
    bb = SAMPLE_ATTN_SEQS
    lane = lax.broadcasted_iota(jnp.int32, (HEAD_DIM, WINDOW), 1)
    col = lax.broadcasted_iota(jnp.int32, (bb * N_HEADS, N_KV * WINDOW), 1)
    row = lax.broadcasted_iota(jnp.int32, (bb * N_HEADS, N_KV * WINDOW), 0)
    own = (col // WINDOW) == (row % N_KV)
    sink = jnp.concatenate([sink_ref[...]] * bb, axis=0)

    def shifted(src_ref, new_t_ref, out_ref, j):
        new = pltpu.roll(new_t_ref[...], (WINDOW - 1) - (first + j), axis=1)
        parts = []
        for kv in range(N_KV):
            hs = slice(kv * HEAD_DIM, (kv + 1) * HEAD_DIM)
            slab = jnp.where(lane == WINDOW - 1, new[hs], pltpu.roll(src_ref[j, kv], WINDOW - 1, axis=1))
            out_ref[j, kv] = slab
            parts.append(slab.astype(BF16))
        return jnp.concatenate(parts, axis=1)

    s = jnp.concatenate(
        [jnp.dot(q_ref[first + j].astype(BF16), shifted(kt_ref, knt_ref, ko_ref, j),
                 preferred_element_type=F32) for j in range(bb)], axis=0)
    yield
    s = jnp.where(own, s, -jnp.inf)
    mx = jnp.maximum(jnp.max(s, axis=-1, keepdims=True), sink)
    p = jnp.exp(s - mx)
    den = jnp.sum(p, axis=-1, keepdims=True) + jnp.exp(sink - mx)
    p = p.astype(BF16)
    yield
    hrow = lax.broadcasted_iota(jnp.int32, (N_HEADS, D_MODEL), 0)
    hcol = lax.broadcasted_iota(jnp.int32, (N_HEADS, D_MODEL), 1)
    pick = jnp.ones((8, N_HEADS), BF16)
    for j in range(bb):
        rs = slice(j * N_HEADS, (j + 1) * N_HEADS)
        vcat = shifted(vt_ref, vnt_ref, vo_ref, j)
        o2 = lax.dot_general(p[rs], jnp.concatenate([vcat, vcat], axis=0), (((1,), (1,)), ((), ())),
                             preferred_element_type=F32) / den[rs]
        obd = jnp.where(hcol // HEAD_DIM == hrow, jnp.concatenate([o2] * (D_MODEL // LANES), axis=1), 0.0)
        a_ref[pl.ds(first + j, 1), :] = jnp.dot(pick, obd.astype(BF16), preferred_element_type=F32)[0:1]
    yield


def kernel(x_prompt, x_sample, state_pool, cache_k_win, cache_v_win, norm_g, ffn_w_gate, ffn_w_up,
           ffn_w_down, pool_w, pool_scale, kv_norm_g, w_kv, w_q, w_o, attn_sinks, final_norm_g):
    bp, seq, _ = x_prompt.shape
    bs = x_sample.shape[0]
    tm = TILE_ROWS
    assert seq % tm == 0 and tm % WINDOW == 0 and bs == LANES and bs % SAMPLE_ATTN_SEQS == 0
    assert x_prompt.shape[2] == D_MODEL and cache_k_win.shape[1:] == (WINDOW, N_KV, HEAD_DIM)
    g = lambda l, i: norm_g[l, i].reshape(1, D_MODEL)
    ffn_w = (ffn_w_gate.reshape(N_FFN, D_MODEL, D_FF), ffn_w_up.reshape(N_FFN, D_MODEL, D_FF),
             ffn_w_down.reshape(N_FFN, D_FF, D_MODEL))
    psc = pool_scale[0].reshape(1, D_MODEL)
    kvg = kv_norm_g.reshape(1, D_MODEL)
    sinks_gk = attn_sinks[0].reshape(N_KV, GROUP).T
    gfin = final_norm_g.reshape(1, D_MODEL)
    tabs_p = _rope_tables(np.arange(seq))
    tabs_s = _rope_tables(np.full((1,), PAST_LEN))

    x = x_prompt.reshape(bp * seq, D_MODEL)
    xs = x_sample.reshape(bs, D_MODEL)

    x, xs, *w_bf16 = _ffn(x, xs, g(0, 0), (*ffn_w, 0), tm=tm, seq=seq, cast_layers=tuple(range(1, N_FFN)))
    w1, w2, w3 = w_bf16[0:2], w_bf16[2:4], w_bf16[4:6]
    buf_t = jnp.transpose(state_pool[0], (1, 0, 2))
    x, xs, pool_p, nbuf_t = _ffn(x, xs, g(0, 2), w1, tm=tm, seq=seq, pool=(g(0, 1), pool_w[0], psc, buf_t))
    x, xs, k_p, v_p, k_s, v_s, wq, wo, q_s = _ffn(x, xs, g(1, 0), w2, tm=tm, seq=seq,
                                                  kv_g=kvg, wkv=w_kv, tabs_p=tabs_p, tabs_s=tabs_s,
                                                  qprep=(w_q[0], w_o[0], g(1, 1)))
    kt = jnp.transpose(cache_k_win, (0, 2, 3, 1))
    vt = jnp.transpose(cache_v_win, (0, 2, 3, 1))
    y_p, y_s, kt_new, vt_new = _ffn(
        x, xs, g(1, 2), w3, tm=tm, seq=seq, tabs_p=tabs_p, final_g=gfin,
        attn=(g(1, 1), wq, k_p, v_p, sinks_gk, wo, q_s.reshape(bs, N_HEADS, HEAD_DIM), kt, vt, k_s.T, v_s.T))

    y_prompt = y_p.reshape(bp, seq, D_MODEL)
    y_sample = y_s.reshape(bs, 1, D_MODEL)
    pool_prompt = pool_p[None]
    pool_sample = jnp.transpose(nbuf_t, (1, 0, 2))[None]
    k_win_prompt = k_p.reshape(bp, seq, KV_DIM)[:, seq - WINDOW:].reshape(bp, WINDOW, N_KV, HEAD_DIM)
    v_win_prompt = v_p.reshape(bp, seq, KV_DIM)[:, seq - WINDOW:].reshape(bp, WINDOW, N_KV, HEAD_DIM)
    k_win_sample = jnp.transpose(kt_new, (0, 3, 1, 2))
    v_win_sample = jnp.transpose(vt_new, (0, 3, 1, 2))
    return (y_prompt, y_sample, pool_prompt, pool_sample, k_win_prompt, v_win_prompt,
            k_win_sample, v_win_sample)
```

```python
import functools

import numpy as np
import jax
import jax.numpy as jnp
from jax import lax
from jax.experimental import pallas as pl
from jax.experimental.pallas import tpu as pltpu

F32 = jnp.float32
BF16 = jnp.bfloat16

D_MODEL = 1024
D_FF = 2816
HEAD_DIM = 64
N_HEADS = 16
N_KV = 4
GROUP = 4
KV_DIM = N_KV * HEAD_DIM
WINDOW = 128
ROT_DIM = 16
ROPE_THETA = 500000.0
EPS = 1e-5
POOL_WINDOWS = (2, 4, 8, 16)
POOL_GROUP = 256
POOL_BUF = 15
PAST_LEN = 16384

N_FFN = 4
LANES = 128
TILE_ROWS = 512
SAMPLE_ATTN_SEQS = 4
FF_CHUNK = 256
N_FF_CHUNKS = D_FF // FF_CHUNK
OUT_CHUNK = 256
N_OUT_CHUNKS = D_MODEL // OUT_CHUNK
VMEM_LIMIT = 60 * 1024 * 1024

GU_SLABS = 8
GU_ROWS = D_MODEL // GU_SLABS
DN_SLABS = 8
DN_ROWS = D_FF // DN_SLABS
STAGE_SLOTS = 4

CAST_GU_ROWS = 32
CAST_DN_ROWS = 128


def _const_spec(shape):
    nd = len(shape)
    return pl.BlockSpec(shape, lambda *_: (0,) * nd, pipeline_mode=pl.Buffered(1))


def _rms(x, g):
    ms = jnp.mean(x * x, axis=-1, keepdims=True)
    return x * lax.rsqrt(ms + EPS) * g


def _rope(x, c, s1, s2):
    pieces = []
    for j in range(x.shape[1] // LANES):
        xc = x[:, j * LANES:(j + 1) * LANES]
        up = pltpu.roll(xc, LANES - ROT_DIM // 2, axis=1)
        dn = pltpu.roll(xc, ROT_DIM // 2, axis=1)
        pieces.append(xc * c + up * s1 + dn * s2)
    return jnp.concatenate(pieces, axis=1) if len(pieces) > 1 else pieces[0]


def _rope_tables(pos):
    half = ROT_DIM // 2
    inv = np.power(ROPE_THETA, -np.arange(half, dtype=np.float64) * (2.0 / ROT_DIM))
    ang = np.asarray(pos, np.float64)[:, None] * inv[None, :]
    cos, sin = np.cos(ang), np.sin(ang)
    n = ang.shape[0]
    ones = np.ones((n, HEAD_DIM - ROT_DIM))
    zeros8 = np.zeros((n, half))
    zeros = np.zeros((n, HEAD_DIM - ROT_DIM))
    c = np.concatenate([cos, cos, ones], axis=1)
    s1 = np.concatenate([-sin, zeros8, zeros], axis=1)
    s2 = np.concatenate([zeros8, sin, zeros], axis=1)
    return [jnp.asarray(np.concatenate([t, t], axis=1), F32) for t in (c, s1, s2)]


def _stage_ffn_weights(wg_hbm, wu_hbm, wd_hbm, wgu_scr, wd_scr, gu_stage, dn_stage, gu_sem, dn_sem):
    chunks = []
    for src, col0 in ((wg_hbm, 0), (wu_hbm, FF_CHUNK)):
        for r in range(GU_SLABS):
            chunks.append(("gu", src, r, col0))
    for r in range(DN_SLABS):
        chunks.append(("dn", wd_hbm, r, 0))

    def copy(c):
        kind, src, r, _ = chunks[c]
        slot = c % STAGE_SLOTS
        if kind == "gu":
            return pltpu.make_async_copy(src.at[pl.ds(r * GU_ROWS, GU_ROWS), :], gu_stage.at[slot], gu_sem.at[slot])
        return pltpu.make_async_copy(src.at[pl.ds(r * DN_ROWS, DN_ROWS), :], dn_stage.at[slot], dn_sem.at[slot])

    for c in range(STAGE_SLOTS):
        copy(c).start()
    for c, (kind, _, r, col0) in enumerate(chunks):
        slot = c % STAGE_SLOTS
        copy(c).wait()
        if kind == "gu":
            for j in range(N_FF_CHUNKS):
                wgu_scr[j, r * GU_ROWS:(r + 1) * GU_ROWS, col0:col0 + FF_CHUNK] = (
                    gu_stage[slot, :, j * FF_CHUNK:(j + 1) * FF_CHUNK].astype(BF16))
        else:
            wd_scr[r * DN_ROWS:(r + 1) * DN_ROWS, :] = dn_stage[slot].astype(BF16)
        if c + STAGE_SLOTS < len(chunks):
            copy(c + STAGE_SLOTS).start()


def _run(*phase_generators):
    live = list(phase_generators)
    while live:
        for gen in list(live):
            try:
                next(gen)
            except StopIteration:
                live.remove(gen)


def _kv_rows(rows, src_ref, kvg_ref, wkv_scr, tabs, k_ref, v_ref):
    kvn = _rms(src_ref[0:rows, :], kvg_ref[...]).astype(BF16)
    yield
    kv = jnp.dot(kvn, wkv_scr[...], preferred_element_type=F32)
    yield
    tb = [t[...] for t in tabs]
    if tb[0].shape[0] != rows:
        tb = [jnp.broadcast_to(t, (rows, LANES)) for t in tb]
    k_ref[...] = _rope(kv[:, :KV_DIM], *tb)
    v_ref[...] = kv[:, KV_DIM:]
    yield


def _ffn_rows(rows, x_ref, a_ref, g_ref, gf_ref, o_ref, wgu_scr, wd_scr, wo_ref, xn_ref, h_ref, xr_ref):
    rs = slice(0, rows)
    if a_ref is not None:
        xr_ref[rs, :] = x_ref[...] + jnp.dot(a_ref[...].astype(BF16), wo_ref[...], preferred_element_type=F32)
    elif xr_ref is not None:
        xr_ref[rs, :] = x_ref[...]
    res = (lambda sl: xr_ref[rs, sl]) if xr_ref is not None else (lambda sl: x_ref[:, sl])
    xn_ref[rs, :] = _rms(res(slice(None)), g_ref[...]).astype(BF16)
    yield
    for j in range(N_FF_CHUNKS):
        r = jnp.dot(xn_ref[rs, :], wgu_scr[j], preferred_element_type=F32)
        gate, up = r[:, :FF_CHUNK], r[:, FF_CHUNK:]
        h_ref[rs, j * FF_CHUNK:(j + 1) * FF_CHUNK] = (gate * jax.nn.sigmoid(gate) * up).astype(BF16)
        yield
    for n in range(N_OUT_CHUNKS):
        sl = slice(n * OUT_CHUNK, (n + 1) * OUT_CHUNK)
        y = jnp.dot(h_ref[rs, :], wd_scr[:, sl], preferred_element_type=F32)
        o_ref[:, sl] = res(sl) + 0.5 * y
        yield
    if gf_ref is not None:
        o_ref[...] = _rms(o_ref[...], gf_ref[...])


def _cast_slabs(wg_in, wu_in, wd_in, wgu_out, wd_out):
    for j in range(N_FF_CHUNKS):
        cs = slice(j * FF_CHUNK, (j + 1) * FF_CHUNK)
        wgu_out[j, :, 0:FF_CHUNK] = wg_in[:, cs].astype(BF16)
        wgu_out[j, :, FF_CHUNK:2 * FF_CHUNK] = wu_in[:, cs].astype(BF16)
    wd_out[...] = wd_in[...].astype(BF16)
    yield


def _permute_qo(wq_ref, wo_ref, wqp_ref, wop_ref):
    lane = lax.broadcasted_iota(jnp.int32, (D_MODEL, LANES), 1)
    src_head = lambda h: (h % N_KV) * GROUP + h // N_KV
    for c in range(D_MODEL // LANES):
        halves = []
        for half in (0, 1):
            hs = src_head(2 * c + half)
            col = wq_ref[:, (hs // 2) * LANES:(hs // 2 + 1) * LANES]
            halves.append(col if hs % 2 == half else pltpu.roll(col, HEAD_DIM, axis=1))
        wqp_ref[:, c * LANES:(c + 1) * LANES] = jnp.where(lane < HEAD_DIM, halves[0], halves[1]).astype(BF16)
    for h in range(N_HEADS):
        hs = src_head(h)
        wop_ref[h * HEAD_DIM:(h + 1) * HEAD_DIM, :] = wo_ref[hs * HEAD_DIM:(hs + 1) * HEAD_DIM, :].astype(BF16)


def _ffn_kernel(*refs, nt, tm, ms, per_seq, layer, n_cast, has_pool, has_attn, has_kv, has_final, has_qprep):
    it = iter(refs)
    x_ref, xs_ref = next(it), next(it)
    a_ref = as_ref = wo_ref = gf_ref = kvg_ref = wkv_ref = None
    tabs_p = tabs_s = None
    if has_pool:
        gp_ref, pw_ref, psc_ref, sbuf_ref = next(it), next(it), next(it), next(it)
    if has_attn:
        xq_ref, gq_ref, wq_ref = next(it), next(it), next(it)
        tabs_q = [next(it) for _ in range(3)]
        kq_ref, vq_ref, sink_ref, wo_ref = [next(it) for _ in range(4)]
        q3_ref, kt_ref, vt_ref, knt_ref, vnt_ref, sinkc_ref = [next(it) for _ in range(6)]
    g_ref = next(it)
    if layer is not None:
        wg_hbm, wu_hbm, wd_hbm = next(it), next(it), next(it)
    else:
        wgu_hbm, wdn_hbm = next(it), next(it)
    cast_in = [[next(it) for _ in range(3)] for _ in range(n_cast)]
    if has_kv:
        kvg_ref, wkv_ref = next(it), next(it)
        tabs_p = [next(it) for _ in range(3)]
        tabs_s = [next(it) for _ in range(3)]
    if has_final:
        gf_ref = next(it)
    if has_qprep:
        wqf_ref, wof_ref, gqs_ref = next(it), next(it), next(it)
        tabs_qs = [next(it) for _ in range(3)]
    o_ref, os_ref = next(it), next(it)
    if has_attn:
        ko_ref, vo_ref = next(it), next(it)
    k_ref = v_ref = ks_ref = vs_ref = None
    if has_kv:
        k_ref, v_ref, ks_ref, vs_ref = next(it), next(it), next(it), next(it)
    cast_out = [[next(it) for _ in range(2)] for _ in range(n_cast)]
    if has_pool:
        pbuf_ref, snbuf_ref = next(it), next(it)
    if has_qprep:
        wqp_ref, wop_ref, qs_ref = next(it), next(it), next(it)
    wgu_scr, wd_scr = next(it), next(it)
    if layer is not None:
        gu_stage, dn_stage, gu_sem, dn_sem = [next(it) for _ in range(4)]
    else:
        w_sem = next(it)
    xn_ref, h_ref = next(it), next(it)
    xr_ref = wkv_scr = None
    if has_attn:
        a_ref, xr_ref, kbuf, vbuf, as_ref = [next(it) for _ in range(5)]
    if has_pool:
        x2_scr, xr_ref, hext_ref, xs2_scr = [next(it) for _ in range(4)]
    if has_kv:
        wkv_scr = next(it)

    i = pl.program_id(0)

    def pool(tile):
        return _pool_rows(tile % per_seq, x_ref, gp_ref, pw_ref, psc_ref, x2_scr, pbuf_ref, hext_ref, tm)

    def attention(tile):
        nprev = jnp.where(tile % per_seq == 0, 0, WINDOW)
        return _attn_rows(nprev, xq_ref, gq_ref, wq_ref, tabs_q, kq_ref, vq_ref, sink_ref, a_ref,
                          kbuf, vbuf, tm)

    @pl.when(i == 0)
    def _():
        if layer is not None:
            _stage_ffn_weights(wg_hbm.at[layer], wu_hbm.at[layer], wd_hbm.at[layer], wgu_scr, wd_scr,
                               gu_stage, dn_stage, gu_sem, dn_sem)
        else:
            copies = [pltpu.make_async_copy(wgu_hbm, wgu_scr, w_sem.at[0]),
                      pltpu.make_async_copy(wdn_hbm, wd_scr, w_sem.at[1])]
            for c in copies:
                c.start()
        if has_attn:
            kbuf[...] = jnp.zeros(kbuf.shape, F32)
            vbuf[...] = jnp.zeros(vbuf.shape, F32)
            _run(attention(0))
        if has_pool:
            hext_ref[...] = jnp.zeros(hext_ref.shape, F32)
            _run(pool(0))
        if has_kv:
            wkv_scr[...] = wkv_ref[...].astype(BF16)
        if has_qprep:
            _permute_qo(wqf_ref, wof_ref, wqp_ref, wop_ref)
        if layer is None:
            for c in copies:
                c.wait()

    common = (wgu_scr, wd_scr, wo_ref, xn_ref, h_ref, xr_ref)

    @pl.when((i >= 1) & (i <= nt))
    def _():
        side = [_cast_slabs(*cast_in[c], *cast_out[c]) for c in range(n_cast)]
        if has_kv:
            side.append(_kv_rows(tm, x_ref, kvg_ref, wkv_scr, tabs_p, k_ref, v_ref))
        if has_attn:
            side.append(attention(jnp.minimum(i, nt - 1)))
            side.append(_attn_sample_rows((i - 1) * SAMPLE_ATTN_SEQS, q3_ref, kt_ref, vt_ref, knt_ref, vnt_ref,
                                          sinkc_ref, ko_ref, vo_ref, as_ref))
        if has_pool:
            side.append(pool(jnp.minimum(i, nt - 1)))
            blk = (i - 1) * (ms // POOL_S_ROWS) // nt
            side.append(_pool_sample_rows(blk * POOL_S_ROWS, xs_ref, sbuf_ref, gp_ref, pw_ref, psc_ref,
                                          xs2_scr, snbuf_ref))
        xin = x2_scr if has_pool else x_ref
        _run(_ffn_rows(tm, xin, a_ref, g_ref, gf_ref, o_ref, *common), *side)

    @pl.when(i == nt + 1)
    def _():
        side = [_kv_rows(ms, xs_ref, kvg_ref, wkv_scr, tabs_s, ks_ref, vs_ref)] if has_kv else []
        xsin = xs2_scr if has_pool else xs_ref
        _run(_ffn_rows(ms, xsin, as_ref, g_ref, gf_ref, os_ref, *common), *side)
        if has_qprep:
            qn = _rms(os_ref[...], gqs_ref[...]).astype(BF16)
            y = jnp.dot(qn, wqp_ref[...], preferred_element_type=F32)
            tb = [jnp.broadcast_to(t[...], (ms, LANES)) for t in tabs_qs]
            qs_ref[...] = _rope(y, *tb) * (HEAD_DIM ** -0.5)


def _ffn(x, xs, g, weights, *, tm, seq, cast_layers=(), pool=None, attn=None, kv_g=None, wkv=None,
         tabs_p=None, tabs_s=None, final_g=None, qprep=None):
    mp, ms = x.shape[0], xs.shape[0]
    nt = mp // tm
    per_seq = seq // tm
    layer = weights[3] if len(weights) == 4 else None
    n_cast = len(cast_layers)
    assert n_cast == 0 or (nt * CAST_GU_ROWS == D_MODEL and nt * CAST_DN_ROWS >= D_FF)
    has_attn, has_kv, has_final = attn is not None, kv_g is not None, final_g is not None
    has_pool = pool is not None
    assert not (has_kv and has_pool)
    tile = lambda i: jnp.clip(i - 1, 0, nt - 1)
    nxt = lambda i: jnp.minimum(i, nt - 1)
    row = lambda w: pl.BlockSpec((tm, w), lambda i: (tile(i), 0))
    nrow = lambda w: pl.BlockSpec((tm, w), lambda i: (nxt(i), 0))
    hbm = pl.BlockSpec(memory_space=pl.ANY)

    args, specs = [x, xs], [nrow(D_MODEL) if has_pool else row(D_MODEL), _const_spec(xs.shape)]
    sblk = pl.BlockSpec((POOL_BUF, POOL_S_ROWS, D_MODEL), lambda i: (0, tile(i) * (ms // POOL_S_ROWS) // nt, 0))
    if has_pool:
        args += list(pool)
        specs += [_const_spec(a.shape) for a in pool[:3]] + [sblk]
    if has_attn:
        gq, wq, k, v, sinks, wo, q3, kt, vt, knt, vnt = attn
        assert ms == nt * SAMPLE_ATTN_SEQS
        sinks_col = sinks.reshape(N_HEADS, 1)
        cblk = pl.BlockSpec((SAMPLE_ATTN_SEQS, N_KV, HEAD_DIM, WINDOW), lambda i: (tile(i), 0, 0, 0))
        args += [x, gq, wq, *tabs_p, k, v, sinks, wo, q3, kt, vt, knt, vnt, sinks_col]
        specs += [nrow(D_MODEL), _const_spec(gq.shape), _const_spec(wq.shape)]
        specs += [pl.BlockSpec((tm, LANES), lambda i: (nxt(i) % per_seq, 0))] * 3
        specs += [nrow(KV_DIM), nrow(KV_DIM), pl.BlockSpec(memory_space=pltpu.SMEM), _const_spec(wo.shape)]
        specs += [_const_spec(q3.shape), cblk, cblk, _const_spec(knt.shape), _const_spec(vnt.shape),
                  _const_spec(sinks_col.shape)]
    args += [g, *weights[:3]] if layer is not None else [g, *weights]
    specs += [_const_spec(g.shape)] + [hbm] * (3 if layer is not None else 2)
    dn_blocks = D_FF // CAST_DN_ROWS
    dn_tile = lambda i: jnp.minimum(tile(i), dn_blocks - 1)
    for cl in cast_layers:
        args += list(weights[:3])
        specs += [pl.BlockSpec((None, CAST_GU_ROWS, D_FF), lambda i, cl=cl: (cl, tile(i), 0)),
                  pl.BlockSpec((None, CAST_GU_ROWS, D_FF), lambda i, cl=cl: (cl, tile(i), 0)),
                  pl.BlockSpec((None, CAST_DN_ROWS, D_MODEL), lambda i, cl=cl: (cl, dn_tile(i), 0))]
    if has_kv:
        args += [kv_g, wkv, *tabs_p, *tabs_s]
        specs += [_const_spec(kv_g.shape), _const_spec(wkv.shape)]
        specs += [pl.BlockSpec((tm, LANES), lambda i: (tile(i) % per_seq, 0))] * 3
        specs += [_const_spec((1, LANES))] * 3
    if has_final:
        args.append(final_g)
        specs.append(_const_spec(final_g.shape))
    has_qprep = qprep is not None
    if has_qprep:
        args += [*qprep, *tabs_s]
        specs += [_const_spec(a.shape) for a in qprep] + [_const_spec((1, LANES))] * 3

    out_shape = [jax.ShapeDtypeStruct((mp, D_MODEL), F32), jax.ShapeDtypeStruct((ms, D_MODEL), F32)]
    out_specs = [row(D_MODEL), _const_spec((ms, D_MODEL))]
    if has_attn:
        out_shape += [jax.ShapeDtypeStruct(kt.shape, F32), jax.ShapeDtypeStruct(vt.shape, F32)]
        out_specs += [cblk, cblk]
    if has_kv:
        out_shape += [jax.ShapeDtypeStruct((mp, KV_DIM), F32)] * 2 + [jax.ShapeDtypeStruct((ms, KV_DIM), F32)] * 2
        out_specs += [row(KV_DIM)] * 2 + [_const_spec((ms, KV_DIM))] * 2
    wgu_shape, wdn_shape = (N_FF_CHUNKS, D_MODEL, 2 * FF_CHUNK), (D_FF, D_MODEL)
    for _ in cast_layers:
        out_shape += [jax.ShapeDtypeStruct(wgu_shape, BF16), jax.ShapeDtypeStruct(wdn_shape, BF16)]
        out_specs += [pl.BlockSpec((N_FF_CHUNKS, CAST_GU_ROWS, 2 * FF_CHUNK), lambda i: (0, tile(i), 0)),
                      pl.BlockSpec((CAST_DN_ROWS, D_MODEL), lambda i: (dn_tile(i), 0))]
    if has_pool:
        out_shape += [jax.ShapeDtypeStruct((mp // seq, POOL_BUF, D_MODEL), F32),
                      jax.ShapeDtypeStruct(pool[3].shape, F32)]
        out_specs += [pl.BlockSpec((1, POOL_BUF, D_MODEL), lambda i: (nxt(i) // per_seq, 0, 0)), sblk]
    if has_qprep:
        out_shape += [jax.ShapeDtypeStruct((D_MODEL, D_MODEL), BF16)] * 2 + [jax.ShapeDtypeStruct((ms, D_MODEL), F32)]
        out_specs += [_const_spec((D_MODEL, D_MODEL))] * 2 + [_const_spec((ms, D_MODEL))]

    scratch = [pltpu.VMEM(wgu_shape, BF16), pltpu.VMEM(wdn_shape, BF16)]
    if layer is not None:
        scratch += [pltpu.VMEM((STAGE_SLOTS, GU_ROWS, D_FF), F32),
                    pltpu.VMEM((STAGE_SLOTS, DN_ROWS, D_MODEL), F32),
                    pltpu.SemaphoreType.DMA((STAGE_SLOTS,)),
                    pltpu.SemaphoreType.DMA((STAGE_SLOTS,))]
    else:
        scratch += [pltpu.SemaphoreType.DMA((2,))]
    scratch += [pltpu.VMEM((tm, D_MODEL), BF16),
                pltpu.VMEM((tm, D_FF), BF16)]
    if has_attn:
        scratch += [pltpu.VMEM((tm, D_MODEL), BF16), pltpu.VMEM((tm, D_MODEL), F32),
                    pltpu.VMEM((WINDOW + tm, KV_DIM), F32), pltpu.VMEM((WINDOW + tm, KV_DIM), F32),
                    pltpu.VMEM((ms, D_MODEL), F32)]
    if has_pool:
        scratch += [pltpu.VMEM((tm, D_MODEL), F32), pltpu.VMEM((tm, D_MODEL), F32),
                    pltpu.VMEM((HALO + tm, D_MODEL), F32), pltpu.VMEM((ms, D_MODEL), F32)]
    if has_kv:
        scratch += [pltpu.VMEM((D_MODEL, 2 * KV_DIM), BF16)]

    return pl.pallas_call(
        functools.partial(_ffn_kernel, nt=nt, tm=tm, ms=ms, per_seq=per_seq, layer=layer, n_cast=n_cast,
                          has_pool=has_pool, has_attn=has_attn, has_kv=has_kv, has_final=has_final,
                          has_qprep=has_qprep),
        grid=(nt + 2,),
        in_specs=specs,
        out_specs=out_specs,
        out_shape=out_shape,
        scratch_shapes=scratch,
        compiler_params=pltpu.CompilerParams(
            dimension_semantics=("arbitrary",), vmem_limit_bytes=VMEM_LIMIT),
        name="ffn",
    )(*args)


HALO = 16
assert POOL_WINDOWS == tuple(2 ** (i + 1) for i in range(len(POOL_WINDOWS))) and max(POOL_WINDOWS) <= HALO


def _pool_rows(t_seq, x_ref, g_ref, w_ref, sc_ref, o_ref, buf_ref, hext_ref, tp):
    keep = jnp.where(t_seq == 0, 0, HALO)
    hrow = lax.broadcasted_iota(jnp.int32, (HALO, D_MODEL), 0)
    hext_ref[0:HALO, :] = jnp.where(hrow < keep, hext_ref[tp:tp + HALO, :], 0.0)
    x = x_ref[...]
    h = _rms(x, g_ref[...])
    hext_ref[HALO:HALO + tp, :] = h
    buf_ref[0] = hext_ref[HALO + tp - POOL_BUF:HALO + tp, :]
    yield
    pos = (t_seq * tp + lax.broadcasted_iota(jnp.int32, (tp, 1), 0)).astype(F32)
    sums, cur, shift = [], hext_ref[...], 1
    for gi in range(len(POOL_WINDOWS)):
        cur = cur + pltpu.roll(cur, shift, axis=0)
        sums.append(cur[HALO:, :POOL_GROUP])
        if gi + 1 < len(POOL_WINDOWS):
            cur, shift = cur[:, POOL_GROUP:], 2 * shift
        yield
    for gi, w in enumerate(POOL_WINDOWS):
        cs = slice(gi * POOL_GROUP, (gi + 1) * POOL_GROUP)
        cnt = jnp.minimum(float(w), pos + 1.0)
        diff = (sums[gi] / cnt - hext_ref[HALO:HALO + tp, cs]).astype(BF16)
        mixed = jnp.dot(diff, w_ref[gi].astype(BF16), preferred_element_type=F32)
        o_ref[:, cs] = x_ref[:, cs] + mixed * sc_ref[:, cs]
        yield


POOL_S_ROWS = 8


def _pool_sample_rows(r0, x_ref, buf_ref, g_ref, w_ref, sc_ref, o_ref, nbuf_ref):
    rows = pl.ds(pl.multiple_of(r0, POOL_S_ROWS), POOL_S_ROWS)
    x = x_ref[rows, :]
    h = _rms(x, g_ref[...])
    nbuf_ref[0:POOL_BUF - 1] = buf_ref[1:POOL_BUF]
    nbuf_ref[POOL_BUF - 1] = h
    yield
    mixed = []
    for gi, w in enumerate(POOL_WINDOWS):
        cs = slice(gi * POOL_GROUP, (gi + 1) * POOL_GROUP)
        s = h[:, cs]
        for k in range(1, w):
            s = s + buf_ref[POOL_BUF - k, :, cs]
        diff = (s / float(w) - h[:, cs]).astype(BF16)
        mixed.append(jnp.dot(diff, w_ref[gi].astype(BF16), preferred_element_type=F32))
    o_ref[rows, :] = x + jnp.concatenate(mixed, axis=1) * sc_ref[...]
    yield


def _seg_mask(rows):
    lane = lax.broadcasted_iota(jnp.int32, (rows, KV_DIM), 1)
    return [(lane >= kv * HEAD_DIM) & (lane < (kv + 1) * HEAD_DIM) for kv in range(N_KV)]


def _attn_rows(nprev, x_ref, g_ref, wq_ref, tabs, k_ref, v_ref, sink_ref, a_ref, kbuf, vbuf, tq):
    hrow = lax.broadcasted_iota(jnp.int32, (WINDOW, KV_DIM), 0)
    kbuf[0:WINDOW, :] = jnp.where(hrow < nprev, kbuf[tq:tq + WINDOW, :], 0.0)
    vbuf[0:WINDOW, :] = jnp.where(hrow < nprev, vbuf[tq:tq + WINDOW, :], 0.0)
    kbuf[WINDOW:WINDOW + tq, :] = k_ref[...]
    vbuf[WINDOW:WINDOW + tq, :] = v_ref[...]

    hn = _rms(x_ref[...], g_ref[...]).astype(BF16)
    q = jnp.dot(hn, wq_ref[...], preferred_element_type=F32)
    q = (_rope(q, *[t[...] for t in tabs]) * (HEAD_DIM ** -0.5)).astype(BF16)
    yield

    seg2 = _seg_mask(2 * WINDOW)
    seg1 = _seg_mask(WINDOW)
    qi = lax.broadcasted_iota(jnp.int32, (WINDOW, WINDOW), 0)
    kj = lax.broadcasted_iota(jnp.int32, (WINDOW, WINDOW), 1)
    causal = kj <= qi
    prev_ok = kj <= qi + nprev

    for i in range(tq // WINDOW):
        kk = kbuf[i * WINDOW:(i + 2) * WINDOW, :]
        vv = vbuf[i * WINDOW:(i + 2) * WINDOW, :]
        kbd = jnp.concatenate([jnp.where(m, kk, 0.0).astype(BF16) for m in seg2], axis=0)
        vbd = jnp.concatenate([jnp.where(m, vv, 0.0).astype(BF16) for m in seg2], axis=0)
        rows = slice(i * WINDOW, (i + 1) * WINDOW)
        qs = jnp.concatenate([q[rows, gm * KV_DIM:(gm + 1) * KV_DIM] for gm in range(GROUP)], axis=0)
        s = lax.dot_general(qs, kbd, (((1,), (1,)), ((), ())), preferred_element_type=F32)
        yield
        pg, rg = [], []
        for gm in range(GROUP):
            ps, rinv = [], jnp.zeros((WINDOW, KV_DIM), F32)
            for kv in range(N_KV):
                s_prev = s[gm * WINDOW:(gm + 1) * WINDOW, kv * 2 * WINDOW:kv * 2 * WINDOW + WINDOW]
                s_own = s[gm * WINDOW:(gm + 1) * WINDOW, kv * 2 * WINDOW + WINDOW:(kv + 1) * 2 * WINDOW]
                sc = jnp.where(causal, s_own, s_prev)
                if i == 0:
                    sc = jnp.where(prev_ok, sc, -jnp.inf)
                sink = sink_ref[gm, kv]
                mx = jnp.maximum(jnp.max(sc, axis=-1, keepdims=True), sink)
                p = jnp.exp(sc - mx)
                den = jnp.sum(p, axis=-1, keepdims=True) + jnp.exp(sink - mx)
                ps.append(jnp.where(causal, 0.0, p).astype(BF16))
                ps.append(jnp.where(causal, p, 0.0).astype(BF16))
                rinv = jnp.where(seg1[kv], 1.0 / den, rinv)
            pg.append(jnp.concatenate(ps, axis=1))
            rg.append(rinv)
            if gm % 2 == 1:
                yield
        og = jnp.dot(jnp.concatenate(pg, axis=0), vbd, preferred_element_type=F32)
        for gm in range(GROUP):
            a_ref[rows, gm * KV_DIM:(gm + 1) * KV_DIM] = (
                og[gm * WINDOW:(gm + 1) * WINDOW] * rg[gm]).astype(a_ref.dtype)
        yield


def _attn_sample_rows(first, q_ref, kt_ref, vt_ref, knt_ref, vnt_ref, sink_ref, ko_ref, vo_ref, a_ref):
    bb = SAMPLE_ATTN_SEQS
    lane = lax.broadcasted_iota(jnp.int32, (HEAD_DIM, WINDOW), 1)
    col = lax.broadcasted_iota(jnp.int32, (bb * N_HEADS, N_KV * WINDOW), 1)
    row = lax.broadcasted_iota(jnp.int32, (bb * N_HEADS, N_KV * WINDOW), 0)
    own = (col // WINDOW) == (row % N_KV)
    sink = jnp.concatenate([sink_ref[...]] * bb, axis=0)

    def shifted(src_ref, new_t_ref, out_ref, j):
        new = pltpu.roll(new_t_ref[...], (WINDOW - 1) - (first + j), axis=1)
        parts = []
        for kv in range(N_KV):
            hs = slice(kv * HEAD_DIM, (kv + 1) * HEAD_DIM)
            slab = jnp.where(lane == WINDOW - 1, new[hs], pltpu.roll(src_ref[j, kv], WINDOW - 1, axis=1))
            out_ref[j, kv] = slab
            parts.append(slab.astype(BF16))
        return jnp.concatenate(parts, axis=1)

    s = jnp.concatenate(
        [jnp.dot(q_ref[first + j].astype(BF16), shifted(kt_ref, knt_ref, ko_ref, j),
                 preferred_element_type=F32) for j in range(bb)], axis=0)
    yield
    s = jnp.where(own, s, -jnp.inf)
    mx = jnp.maximum(jnp.max(s, axis=-1, keepdims=True), sink)
    p = jnp.exp(s - mx)
    den = jnp.sum(p, axis=-1, keepdims=True) + jnp.exp(sink - mx)
    p = p.astype(BF16)
    yield
    hrow = lax.broadcasted_iota(jnp.int32, (N_HEADS, D_MODEL), 0)
    hcol = lax.broadcasted_iota(jnp.int32, (N_HEADS, D_MODEL), 1)
    obds = []
    for j in range(bb):
        rs = slice(j * N_HEADS, (j + 1) * N_HEADS)
        vcat = shifted(vt_ref, vnt_ref, vo_ref, j)
        o2 = lax.dot_general(p[rs], jnp.concatenate([vcat, vcat], axis=0), (((1,), (1,)), ((), ())),
                             preferred_element_type=F32) / den[rs]
        obds.append(jnp.where(hcol // HEAD_DIM == hrow,
                              jnp.concatenate([o2] * (D_MODEL // LANES), axis=1), 0.0).astype(BF16))
    pr = lax.broadcasted_iota(jnp.int32, (8, bb * N_HEADS), 0)
    pc = lax.broadcasted_iota(jnp.int32, (8, bb * N_HEADS), 1)
    pick = jnp.where(pc // N_HEADS == pr, 1.0, 0.0).astype(BF16)
    rows = jnp.dot(pick, jnp.concatenate(obds, axis=0), preferred_element_type=F32)
    for j in range(bb):
        a_ref[pl.ds(first + j, 1), :] = rows[j:j + 1]
    yield


def kernel(x_prompt, x_sample, state_pool, cache_k_win, cache_v_win, norm_g, ffn_w_gate, ffn_w_up,
           ffn_w_down, pool_w, pool_scale, kv_norm_g, w_kv, w_q, w_o, attn_sinks, final_norm_g):
    bp, seq, _ = x_prompt.shape
    bs = x_sample.shape[0]
    tm = TILE_ROWS
    assert seq % tm == 0 and tm % WINDOW == 0 and bs == LANES and bs % SAMPLE_ATTN_SEQS == 0
    assert x_prompt.shape[2] == D_MODEL and cache_k_win.shape[1:] == (WINDOW, N_KV, HEAD_DIM)
    g = lambda l, i: norm_g[l, i].reshape(1, D_MODEL)
    ffn_w = (ffn_w_gate.reshape(N_FFN, D_MODEL, D_FF), ffn_w_up.reshape(N_FFN, D_MODEL, D_FF),
             ffn_w_down.reshape(N_FFN, D_FF, D_MODEL))
    psc = pool_scale[0].reshape(1, D_MODEL)
    kvg = kv_norm_g.reshape(1, D_MODEL)
    sinks_gk = attn_sinks[0].reshape(N_KV, GROUP).T
    gfin = final_norm_g.reshape(1, D_MODEL)
    tabs_p = _rope_tables(np.arange(seq))
    tabs_s = _rope_tables(np.full((1,), PAST_LEN))

    x = x_prompt.reshape(bp * seq, D_MODEL)
    xs = x_sample.reshape(bs, D_MODEL)

    x, xs, *w_bf16 = _ffn(x, xs, g(0, 0), (*ffn_w, 0), tm=tm, seq=seq, cast_layers=tuple(range(1, N_FFN)))
    w1, w2, w3 = w_bf16[0:2], w_bf16[2:4], w_bf16[4:6]
    buf_t = jnp.transpose(state_pool[0], (1, 0, 2))
    x, xs, pool_p, nbuf_t = _ffn(x, xs, g(0, 2), w1, tm=tm, seq=seq, pool=(g(0, 1), pool_w[0], psc, buf_t))
    x, xs, k_p, v_p, k_s, v_s, wq, wo, q_s = _ffn(x, xs, g(1, 0), w2, tm=tm, seq=seq,
                                                  kv_g=kvg, wkv=w_kv, tabs_p=tabs_p, tabs_s=tabs_s,
                                                  qprep=(w_q[0], w_o[0], g(1, 1)))
    kt = jnp.transpose(cache_k_win, (0, 2, 3, 1))
    vt = jnp.transpose(cache_v_win, (0, 2, 3, 1))
    y_p, y_s, kt_new, vt_new = _ffn(
        x, xs, g(1, 2), w3, tm=tm, seq=seq, tabs_p=tabs_p, final_g=gfin,
        attn=(g(1, 1), wq, k_p, v_p, sinks_gk, wo, q_s.reshape(bs, N_HEADS, HEAD_DIM), kt, vt, k_s.T, v_s.T))

    y_prompt = y_p.reshape(bp, seq, D_MODEL)
    y_sample = y_s.reshape(bs, 1, D_MODEL)
    pool_prompt = pool_p[None]
    pool_sample = jnp.transpose(nbuf_t, (1, 0, 2))[None]
    k_win_prompt = k_p.reshape(bp, seq, KV_DIM)[:, seq - WINDOW:].reshape(bp, WINDOW, N_KV, HEAD_DIM)
    v_win_prompt = v_p.reshape(bp, seq, KV_DIM)[:, seq - WINDOW:].reshape(bp, WINDOW, N_KV, HEAD_DIM)
    k_win_sample = jnp.transpose(kt_new, (0, 3, 1, 2))
    v_win_sample = jnp.transpose(vt_new, (0, 3, 1, 2))
    return (y_prompt, y_sample, pool_prompt, pool_sample, k_win_prompt, v_win_prompt,
            k_win_sample, v_win_sample)
```

```python
import functools

import numpy as np
import jax
import jax.numpy as jnp
from jax import lax
from jax.experimental import pallas as pl
from jax.experimental.pallas import tpu as pltpu

F32 = jnp.float32
BF16 = jnp.bfloat16

D_MODEL = 1024
D_FF = 2816
HEAD_DIM = 64
N_HEADS = 16
N_KV = 4
GROUP = 4
KV_DIM = N_KV * HEAD_DIM
WINDOW = 128
ROT_DIM = 16
ROPE_THETA = 500000.0
EPS = 1e-5
POOL_WINDOWS = (2, 4, 8, 16)
POOL_GROUP = 256
POOL_BUF = 15
PAST_LEN = 16384

N_FFN = 4
LANES = 128
TILE_ROWS = 512
SAMPLE_ATTN_SEQS = 4
FF_CHUNK = 256
N_FF_CHUNKS = D_FF // FF_CHUNK
OUT_CHUNK = 256
N_OUT_CHUNKS = D_MODEL // OUT_CHUNK
VMEM_LIMIT = 60 * 1024 * 1024

GU_SLABS = 8
GU_ROWS = D_MODEL // GU_SLABS
DN_SLABS = 8
DN_ROWS = D_FF // DN_SLABS
STAGE_SLOTS = 4

CAST_GU_ROWS = 32
CAST_DN_ROWS = 128


def _const_spec(shape):
    nd = len(shape)
    return pl.BlockSpec(shape, lambda *_: (0,) * nd, pipeline_mode=pl.Buffered(1))


def _rms(x, g):
    ms = jnp.mean(x * x, axis=-1, keepdims=True)
    return x * lax.rsqrt(ms + EPS) * g


def _rope(x, c, s1, s2):
    pieces = []
    for j in range(x.shape[1] // LANES):
        xc = x[:, j * LANES:(j + 1) * LANES]
        up = pltpu.roll(xc, LANES - ROT_DIM // 2, axis=1)
        dn = pltpu.roll(xc, ROT_DIM // 2, axis=1)
        pieces.append(xc * c + up * s1 + dn * s2)
    return jnp.concatenate(pieces, axis=1) if len(pieces) > 1 else pieces[0]


def _rope_tables(pos):
    half = ROT_DIM // 2
    inv = np.power(ROPE_THETA, -np.arange(half, dtype=np.float64) * (2.0 / ROT_DIM))
    ang = np.asarray(pos, np.float64)[:, None] * inv[None, :]
    cos, sin = np.cos(ang), np.sin(ang)
    n = ang.shape[0]
    ones = np.ones((n, HEAD_DIM - ROT_DIM))
    zeros8 = np.zeros((n, half))
    zeros = np.zeros((n, HEAD_DIM - ROT_DIM))
    c = np.concatenate([cos, cos, ones], axis=1)
    s1 = np.concatenate([-sin, zeros8, zeros], axis=1)
    s2 = np.concatenate([zeros8, sin, zeros], axis=1)
    return [jnp.asarray(np.concatenate([t, t], axis=1), F32) for t in (c, s1, s2)]


def _stage_ffn_weights(wg_hbm, wu_hbm, wd_hbm, wgu_scr, wd_scr, gu_stage, dn_stage, gu_sem, dn_sem):
    chunks = []
    for src, col0 in ((wg_hbm, 0), (wu_hbm, FF_CHUNK)):
        for r in range(GU_SLABS):
            chunks.append(("gu", src, r, col0))
    for r in range(DN_SLABS):
        chunks.append(("dn", wd_hbm, r, 0))

    def copy(c):
        kind, src, r, _ = chunks[c]
        slot = c % STAGE_SLOTS
        if kind == "gu":
            return pltpu.make_async_copy(src.at[pl.ds(r * GU_ROWS, GU_ROWS), :], gu_stage.at[slot], gu_sem.at[slot])
        return pltpu.make_async_copy(src.at[pl.ds(r * DN_ROWS, DN_ROWS), :], dn_stage.at[slot], dn_sem.at[slot])

    for c in range(STAGE_SLOTS):
        copy(c).start()
    for c, (kind, _, r, col0) in enumerate(chunks):
        slot = c % STAGE_SLOTS
        copy(c).wait()
        if kind == "gu":
            for j in range(N_FF_CHUNKS):
                wgu_scr[j, r * GU_ROWS:(r + 1) * GU_ROWS, col0:col0 + FF_CHUNK] = (
                    gu_stage[slot, :, j * FF_CHUNK:(j + 1) * FF_CHUNK].astype(BF16))
        else:
            wd_scr[r * DN_ROWS:(r + 1) * DN_ROWS, :] = dn_stage[slot].astype(BF16)
        if c + STAGE_SLOTS < len(chunks):
            copy(c + STAGE_SLOTS).start()


def _run(*phase_generators):
    live = list(phase_generators)
    while live:
        for gen in list(live):
            try:
                next(gen)
            except StopIteration:
                live.remove(gen)


def _kv_rows(rows, src_ref, kvg_ref, wkv_scr, tabs, k_ref, v_ref):
    kvn = _rms(src_ref[0:rows, :], kvg_ref[...]).astype(BF16)
    yield
    kv = jnp.dot(kvn, wkv_scr[...], preferred_element_type=F32)
    yield
    tb = [t[...] for t in tabs]
    if tb[0].shape[0] != rows:
        tb = [jnp.broadcast_to(t, (rows, LANES)) for t in tb]
    k_ref[...] = _rope(kv[:, :KV_DIM], *tb)
    v_ref[...] = kv[:, KV_DIM:]
    yield


def _ffn_rows(rows, x_ref, a_ref, g_ref, gf_ref, o_ref, wgu_scr, wd_scr, wo_ref, xn_ref, h_ref, xr_ref):
    rs = slice(0, rows)
    if a_ref is not None:
        xr_ref[rs, :] = x_ref[...] + jnp.dot(a_ref[...].astype(BF16), wo_ref[...], preferred_element_type=F32)
    elif xr_ref is not None:
        xr_ref[rs, :] = x_ref[...]
    res = (lambda sl: xr_ref[rs, sl]) if xr_ref is not None else (lambda sl: x_ref[:, sl])
    xn_ref[rs, :] = _rms(res(slice(None)), g_ref[...]).astype(BF16)
    yield
    for j in range(N_FF_CHUNKS):
        r = jnp.dot(xn_ref[rs, :], wgu_scr[j], preferred_element_type=F32)
        gate, up = r[:, :FF_CHUNK], r[:, FF_CHUNK:]
        h_ref[rs, j * FF_CHUNK:(j + 1) * FF_CHUNK] = (gate * jax.nn.sigmoid(gate) * up).astype(BF16)
        yield
    for n in range(N_OUT_CHUNKS):
        sl = slice(n * OUT_CHUNK, (n + 1) * OUT_CHUNK)
        y = jnp.dot(h_ref[rs, :], wd_scr[:, sl], preferred_element_type=F32)
        o_ref[:, sl] = res(sl) + 0.5 * y
        yield
    if gf_ref is not None:
        o_ref[...] = _rms(o_ref[...], gf_ref[...])


def _cast_slabs(wg_in, wu_in, wd_in, wgu_out, wd_out):
    for j in range(N_FF_CHUNKS):
        cs = slice(j * FF_CHUNK, (j + 1) * FF_CHUNK)
        wgu_out[j, :, 0:FF_CHUNK] = wg_in[:, cs].astype(BF16)
        wgu_out[j, :, FF_CHUNK:2 * FF_CHUNK] = wu_in[:, cs].astype(BF16)
    wd_out[...] = wd_in[...].astype(BF16)
    yield


def _permute_qo(wq_ref, wo_ref, wqp_ref, wop_ref):
    lane = lax.broadcasted_iota(jnp.int32, (D_MODEL, LANES), 1)
    src_head = lambda h: (h % N_KV) * GROUP + h // N_KV
    for c in range(D_MODEL // LANES):
        halves = []
        for half in (0, 1):
            hs = src_head(2 * c + half)
            col = wq_ref[:, (hs // 2) * LANES:(hs // 2 + 1) * LANES]
            halves.append(col if hs % 2 == half else pltpu.roll(col, HEAD_DIM, axis=1))
        wqp_ref[:, c * LANES:(c + 1) * LANES] = jnp.where(lane < HEAD_DIM, halves[0], halves[1]).astype(BF16)
    for h in range(N_HEADS):
        hs = src_head(h)
        wop_ref[h * HEAD_DIM:(h + 1) * HEAD_DIM, :] = wo_ref[hs * HEAD_DIM:(hs + 1) * HEAD_DIM, :].astype(BF16)


def _ffn_kernel(*refs, nt, tm, ms, per_seq, layer, n_cast, has_pool, has_attn, has_kv, has_final, has_qprep):
    it = iter(refs)
    x_ref, xs_ref = next(it), next(it)
    a_ref = as_ref = wo_ref = gf_ref = kvg_ref = wkv_ref = None
    tabs_p = tabs_s = None
    if has_pool:
        gp_ref, pw_ref, psc_ref, sbuf_ref = next(it), next(it), next(it), next(it)
    if has_attn:
        xq_ref, gq_ref, wq_ref = next(it), next(it), next(it)
        tabs_q = [next(it) for _ in range(3)]
        kq_ref, vq_ref, sink_ref, wo_ref = [next(it) for _ in range(4)]
        q3_ref, kt_ref, vt_ref, knt_ref, vnt_ref, sinkc_ref = [next(it) for _ in range(6)]
    g_ref = next(it)
    if layer is not None:
        wg_hbm, wu_hbm, wd_hbm = next(it), next(it), next(it)
    else:
        wgu_hbm, wdn_hbm = next(it), next(it)
    cast_in = [[next(it) for _ in range(3)] for _ in range(n_cast)]
    if has_kv:
        kvg_ref, wkv_ref = next(it), next(it)
        tabs_p = [next(it) for _ in range(3)]
        tabs_s = [next(it) for _ in range(3)]
    if has_final:
        gf_ref = next(it)
    if has_qprep:
        wqf_ref, wof_ref, gqs_ref = next(it), next(it), next(it)
        tabs_qs = [next(it) for _ in range(3)]
    o_ref, os_ref = next(it), next(it)
    if has_attn:
        ko_ref, vo_ref = next(it), next(it)
    k_ref = v_ref = ks_ref = vs_ref = None
    if has_kv:
        k_ref, v_ref, ks_ref, vs_ref = next(it), next(it), next(it), next(it)
    cast_out = [[next(it) for _ in range(2)] for _ in range(n_cast)]
    if has_pool:
        pbuf_ref, snbuf_ref = next(it), next(it)
    if has_qprep:
        wqp_ref, wop_ref, qs_ref = next(it), next(it), next(it)
    wgu_scr, wd_scr = next(it), next(it)
    if layer is not None:
        gu_stage, dn_stage, gu_sem, dn_sem = [next(it) for _ in range(4)]
    else:
        w_sem = next(it)
    xn_ref, h_ref = next(it), next(it)
    xr_ref = wkv_scr = None
    if has_attn:
        a_ref, xr_ref, kbuf, vbuf, as_ref = [next(it) for _ in range(5)]
    if has_pool:
        x2_scr, xr_ref, hext_ref, xs2_scr = [next(it) for _ in range(4)]
    if has_kv:
        wkv_scr = next(it)

    i = pl.program_id(0)

    def pool(tile):
        return _pool_rows(tile % per_seq, x_ref, gp_ref, pw_ref, psc_ref, x2_scr, pbuf_ref, hext_ref, tm)

    def attention(tile):
        nprev = jnp.where(tile % per_seq == 0, 0, WINDOW)
        return _attn_rows(nprev, xq_ref, gq_ref, wq_ref, tabs_q, kq_ref, vq_ref, sink_ref, a_ref,
                          kbuf, vbuf, tm)

    @pl.when(i == 0)
    def _():
        if layer is not None:
            _stage_ffn_weights(wg_hbm.at[layer], wu_hbm.at[layer], wd_hbm.at[layer], wgu_scr, wd_scr,
                               gu_stage, dn_stage, gu_sem, dn_sem)
        else:
            copies = [pltpu.make_async_copy(wgu_hbm, wgu_scr, w_sem.at[0]),
                      pltpu.make_async_copy(wdn_hbm, wd_scr, w_sem.at[1])]
            for c in copies:
                c.start()
        if has_attn:
            kbuf[...] = jnp.zeros(kbuf.shape, F32)
            vbuf[...] = jnp.zeros(vbuf.shape, F32)
            _run(attention(0))
        if has_pool:
            hext_ref[...] = jnp.zeros(hext_ref.shape, F32)
            _run(pool(0))
        if has_kv:
            wkv_scr[...] = wkv_ref[...].astype(BF16)
        if has_qprep:
            _permute_qo(wqf_ref, wof_ref, wqp_ref, wop_ref)
        if layer is None:
            for c in copies:
                c.wait()

    common = (wgu_scr, wd_scr, wo_ref, xn_ref, h_ref, xr_ref)

    @pl.when((i >= 1) & (i <= nt))
    def _():
        side = [_cast_slabs(*cast_in[c], *cast_out[c]) for c in range(n_cast)]
        if has_kv:
            side.append(_kv_rows(tm, x_ref, kvg_ref, wkv_scr, tabs_p, k_ref, v_ref))
        if has_attn:
            side.append(attention(jnp.minimum(i, nt - 1)))
            side.append(_attn_sample_rows((i - 1) * SAMPLE_ATTN_SEQS, q3_ref, kt_ref, vt_ref, knt_ref, vnt_ref,
                                          sinkc_ref, ko_ref, vo_ref, as_ref))
        if has_pool:
            side.append(pool(jnp.minimum(i, nt - 1)))
            blk = (i - 1) * (ms // POOL_S_ROWS) // nt
            side.append(_pool_sample_rows(blk * POOL_S_ROWS, xs_ref, sbuf_ref, gp_ref, pw_ref, psc_ref,
                                          xs2_scr, snbuf_ref))
        xin = x2_scr if has_pool else x_ref
        _run(_ffn_rows(tm, xin, a_ref, g_ref, gf_ref, o_ref, *common), *side)

    @pl.when(i == nt + 1)
    def _():
        side = [_kv_rows(ms, xs_ref, kvg_ref, wkv_scr, tabs_s, ks_ref, vs_ref)] if has_kv else []
        xsin = xs2_scr if has_pool else xs_ref
        _run(_ffn_rows(ms, xsin, as_ref, g_ref, gf_ref, os_ref, *common), *side)
        if has_qprep:
            qn = _rms(os_ref[...], gqs_ref[...]).astype(BF16)
            y = jnp.dot(qn, wqp_ref[...], preferred_element_type=F32)
            tb = [jnp.broadcast_to(t[...], (ms, LANES)) for t in tabs_qs]
            qs_ref[...] = _rope(y, *tb) * (HEAD_DIM ** -0.5)


def _ffn(x, xs, g, weights, *, tm, seq, cast_layers=(), pool=None, attn=None, kv_g=None, wkv=None,
         tabs_p=None, tabs_s=None, final_g=None, qprep=None):
    mp, ms = x.shape[0], xs.shape[0]
    nt = mp // tm
    per_seq = seq // tm
    layer = weights[3] if len(weights) == 4 else None
    n_cast = len(cast_layers)
    assert n_cast == 0 or (nt * CAST_GU_ROWS == D_MODEL and nt * CAST_DN_ROWS >= D_FF)
    has_attn, has_kv, has_final = attn is not None, kv_g is not None, final_g is not None
    has_pool = pool is not None
    assert not (has_kv and has_pool)
    tile = lambda i: jnp.clip(i - 1, 0, nt - 1)
    nxt = lambda i: jnp.minimum(i, nt - 1)
    row = lambda w: pl.BlockSpec((tm, w), lambda i: (tile(i), 0))
    nrow = lambda w: pl.BlockSpec((tm, w), lambda i: (nxt(i), 0))
    hbm = pl.BlockSpec(memory_space=pl.ANY)

    args, specs = [x, xs], [nrow(D_MODEL) if has_pool else row(D_MODEL), _const_spec(xs.shape)]
    sblk = pl.BlockSpec((POOL_BUF, POOL_S_ROWS, D_MODEL), lambda i: (0, tile(i) * (ms // POOL_S_ROWS) // nt, 0))
    if has_pool:
        args += list(pool)
        specs += [_const_spec(a.shape) for a in pool[:3]] + [sblk]
    if has_attn:
        gq, wq, k, v, sinks, wo, q3, kt, vt, knt, vnt = attn
        assert ms == nt * SAMPLE_ATTN_SEQS
        sinks_col = sinks.reshape(N_HEADS, 1)
        cblk = pl.BlockSpec((SAMPLE_ATTN_SEQS, N_KV, HEAD_DIM, WINDOW), lambda i: (tile(i), 0, 0, 0))
        args += [x, gq, wq, *tabs_p, k, v, sinks, wo, q3, kt, vt, knt, vnt, sinks_col]
        specs += [nrow(D_MODEL), _const_spec(gq.shape), _const_spec(wq.shape)]
        specs += [pl.BlockSpec((tm, LANES), lambda i: (nxt(i) % per_seq, 0))] * 3
        specs += [nrow(KV_DIM), nrow(KV_DIM), pl.BlockSpec(memory_space=pltpu.SMEM), _const_spec(wo.shape)]
        specs += [_const_spec(q3.shape), cblk, cblk, _const_spec(knt.shape), _const_spec(vnt.shape),
                  _const_spec(sinks_col.shape)]
    args += [g, *weights[:3]] if layer is not None else [g, *weights]
    specs += [_const_spec(g.shape)] + [hbm] * (3 if layer is not None else 2)
    dn_blocks = D_FF // CAST_DN_ROWS
    dn_tile = lambda i: jnp.minimum(tile(i), dn_blocks - 1)
    for cl in cast_layers:
        args += list(weights[:3])
        specs += [pl.BlockSpec((None, CAST_GU_ROWS, D_FF), lambda i, cl=cl: (cl, tile(i), 0)),
                  pl.BlockSpec((None, CAST_GU_ROWS, D_FF), lambda i, cl=cl: (cl, tile(i), 0)),
                  pl.BlockSpec((None, CAST_DN_ROWS, D_MODEL), lambda i, cl=cl: (cl, dn_tile(i), 0))]
    if has_kv:
        args += [kv_g, wkv, *tabs_p, *tabs_s]
        specs += [_const_spec(kv_g.shape), _const_spec(wkv.shape)]
        specs += [pl.BlockSpec((tm, LANES), lambda i: (tile(i) % per_seq, 0))] * 3
        specs += [_const_spec((1, LANES))] * 3
    if has_final:
        args.append(final_g)
        specs.append(_const_spec(final_g.shape))
    has_qprep = qprep is not None
    if has_qprep:
        args += [*qprep, *tabs_s]
        specs += [_const_spec(a.shape) for a in qprep] + [_const_spec((1, LANES))] * 3

    out_shape = [jax.ShapeDtypeStruct((mp, D_MODEL), F32), jax.ShapeDtypeStruct((ms, D_MODEL), F32)]
    out_specs = [row(D_MODEL), _const_spec((ms, D_MODEL))]
    if has_attn:
        out_shape += [jax.ShapeDtypeStruct(kt.shape, F32), jax.ShapeDtypeStruct(vt.shape, F32)]
        out_specs += [cblk, cblk]
    if has_kv:
        out_shape += [jax.ShapeDtypeStruct((mp, KV_DIM), F32)] * 2 + [jax.ShapeDtypeStruct((ms, KV_DIM), F32)] * 2
        out_specs += [row(KV_DIM)] * 2 + [_const_spec((ms, KV_DIM))] * 2
    wgu_shape, wdn_shape = (N_FF_CHUNKS, D_MODEL, 2 * FF_CHUNK), (D_FF, D_MODEL)
    for _ in cast_layers:
        out_shape += [jax.ShapeDtypeStruct(wgu_shape, BF16), jax.ShapeDtypeStruct(wdn_shape, BF16)]
        out_specs += [pl.BlockSpec((N_FF_CHUNKS, CAST_GU_ROWS, 2 * FF_CHUNK), lambda i: (0, tile(i), 0)),
                      pl.BlockSpec((CAST_DN_ROWS, D_MODEL), lambda i: (dn_tile(i), 0))]
    if has_pool:
        out_shape += [jax.ShapeDtypeStruct((mp // seq, POOL_BUF, D_MODEL), F32),
                      jax.ShapeDtypeStruct(pool[3].shape, F32)]
        out_specs += [pl.BlockSpec((1, POOL_BUF, D_MODEL), lambda i: (nxt(i) // per_seq, 0, 0)), sblk]
    if has_qprep:
        out_shape += [jax.ShapeDtypeStruct((D_MODEL, D_MODEL), BF16)] * 2 + [jax.ShapeDtypeStruct((ms, D_MODEL), F32)]
        out_specs += [_const_spec((D_MODEL, D_MODEL))] * 2 + [_const_spec((ms, D_MODEL))]

    scratch = [pltpu.VMEM(wgu_shape, BF16), pltpu.VMEM(wdn_shape, BF16)]
    if layer is not None:
        scratch += [pltpu.VMEM((STAGE_SLOTS, GU_ROWS, D_FF), F32),
                    pltpu.VMEM((STAGE_SLOTS, DN_ROWS, D_MODEL), F32),
                    pltpu.SemaphoreType.DMA((STAGE_SLOTS,)),
                    pltpu.SemaphoreType.DMA((STAGE_SLOTS,))]
    else:
        scratch += [pltpu.SemaphoreType.DMA((2,))]
    scratch += [pltpu.VMEM((tm, D_MODEL), BF16),
                pltpu.VMEM((tm, D_FF), BF16)]
    if has_attn:
        scratch += [pltpu.VMEM((tm, D_MODEL), BF16), pltpu.VMEM((tm, D_MODEL), F32),
                    pltpu.VMEM((WINDOW + tm, KV_DIM), F32), pltpu.VMEM((WINDOW + tm, KV_DIM), F32),
                    pltpu.VMEM((ms, D_MODEL), F32)]
    if has_pool:
        scratch += [pltpu.VMEM((tm, D_MODEL), F32), pltpu.VMEM((tm, D_MODEL), F32),
                    pltpu.VMEM((HALO + tm, D_MODEL), F32), pltpu.VMEM((ms, D_MODEL), F32)]
    if has_kv:
        scratch += [pltpu.VMEM((D_MODEL, 2 * KV_DIM), BF16)]

    return pl.pallas_call(
        functools.partial(_ffn_kernel, nt=nt, tm=tm, ms=ms, per_seq=per_seq, layer=layer, n_cast=n_cast,
                          has_pool=has_pool, has_attn=has_attn, has_kv=has_kv, has_final=has_final,
                          has_qprep=has_qprep),
        grid=(nt + 2,),
        in_specs=specs,
        out_specs=out_specs,
        out_shape=out_shape,
        scratch_shapes=scratch,
        compiler_params=pltpu.CompilerParams(
            dimension_semantics=("arbitrary",), vmem_limit_bytes=VMEM_LIMIT),
        name="ffn",
    )(*args)


HALO = 16
assert POOL_WINDOWS == tuple(2 ** (i + 1) for i in range(len(POOL_WINDOWS))) and max(POOL_WINDOWS) <= HALO


def _pool_rows(t_seq, x_ref, g_ref, w_ref, sc_ref, o_ref, buf_ref, hext_ref, tp):
    keep = jnp.where(t_seq == 0, 0, HALO)
    hrow = lax.broadcasted_iota(jnp.int32, (HALO, D_MODEL), 0)
    hext_ref[0:HALO, :] = jnp.where(hrow < keep, hext_ref[tp:tp + HALO, :], 0.0)
    x = x_ref[...]
    h = _rms(x, g_ref[...])
    hext_ref[HALO:HALO + tp, :] = h
    buf_ref[0] = hext_ref[HALO + tp - POOL_BUF:HALO + tp, :]
    yield
    pos = (t_seq * tp + lax.broadcasted_iota(jnp.int32, (tp, 1), 0)).astype(F32)
    sums, cur, shift = [], hext_ref[...], 1
    for gi in range(len(POOL_WINDOWS)):
        cur = cur + pltpu.roll(cur, shift, axis=0)
        sums.append(cur[HALO:, :POOL_GROUP])
        if gi + 1 < len(POOL_WINDOWS):
            cur, shift = cur[:, POOL_GROUP:], 2 * shift
        yield
    for gi, w in enumerate(POOL_WINDOWS):
        cs = slice(gi * POOL_GROUP, (gi + 1) * POOL_GROUP)
        cnt = jnp.minimum(float(w), pos + 1.0)
        diff = (sums[gi] / cnt - hext_ref[HALO:HALO + tp, cs]).astype(BF16)
        mixed = jnp.dot(diff, w_ref[gi].astype(BF16), preferred_element_type=F32)
        o_ref[:, cs] = x_ref[:, cs] + mixed * sc_ref[:, cs]
        yield


POOL_S_ROWS = 8


def _pool_sample_rows(r0, x_ref, buf_ref, g_ref, w_ref, sc_ref, o_ref, nbuf_ref):
    rows = pl.ds(pl.multiple_of(r0, POOL_S_ROWS), POOL_S_ROWS)
    x = x_ref[rows, :]
    h = _rms(x, g_ref[...])
    nbuf_ref[0:POOL_BUF - 1] = buf_ref[1:POOL_BUF]
    nbuf_ref[POOL_BUF - 1] = h
    yield
    mixed = []
    for gi, w in enumerate(POOL_WINDOWS):
        cs = slice(gi * POOL_GROUP, (gi + 1) * POOL_GROUP)
        s = h[:, cs]
        for k in range(1, w):
            s = s + buf_ref[POOL_BUF - k, :, cs]
        diff = (s / float(w) - h[:, cs]).astype(BF16)
        mixed.append(jnp.dot(diff, w_ref[gi].astype(BF16), preferred_element_type=F32))
    o_ref[rows, :] = x + jnp.concatenate(mixed, axis=1) * sc_ref[...]
    yield


def _seg_mask(rows):
    lane = lax.broadcasted_iota(jnp.int32, (rows, KV_DIM), 1)
    return [(lane >= kv * HEAD_DIM) & (lane < (kv + 1) * HEAD_DIM) for kv in range(N_KV)]


def _attn_rows(nprev, x_ref, g_ref, wq_ref, tabs, k_ref, v_ref, sink_ref, a_ref, kbuf, vbuf, tq):
    hrow = lax.broadcasted_iota(jnp.int32, (WINDOW, KV_DIM), 0)
    kbuf[0:WINDOW, :] = jnp.where(hrow < nprev, kbuf[tq:tq + WINDOW, :], 0.0)
    vbuf[0:WINDOW, :] = jnp.where(hrow < nprev, vbuf[tq:tq + WINDOW, :], 0.0)
    kbuf[WINDOW:WINDOW + tq, :] = k_ref[...]
    vbuf[WINDOW:WINDOW + tq, :] = v_ref[...]

    hn = _rms(x_ref[...], g_ref[...]).astype(BF16)
    q = jnp.dot(hn, wq_ref[...], preferred_element_type=F32)
    q = (_rope(q, *[t[...] for t in tabs]) * (HEAD_DIM ** -0.5)).astype(BF16)
    yield

    seg2 = _seg_mask(2 * WINDOW)
    seg1 = _seg_mask(WINDOW)
    qi = lax.broadcasted_iota(jnp.int32, (WINDOW, WINDOW), 0)
    kj = lax.broadcasted_iota(jnp.int32, (WINDOW, WINDOW), 1)
    causal = kj <= qi
    prev_ok = kj <= qi + nprev

    for i in range(tq // WINDOW):
        kk = kbuf[i * WINDOW:(i + 2) * WINDOW, :]
        vv = vbuf[i * WINDOW:(i + 2) * WINDOW, :]
        kbd = jnp.concatenate([jnp.where(m, kk, 0.0).astype(BF16) for m in seg2], axis=0)
        vbd = jnp.concatenate([jnp.where(m, vv, 0.0).astype(BF16) for m in seg2], axis=0)
        rows = slice(i * WINDOW, (i + 1) * WINDOW)
        qs = jnp.concatenate([q[rows, gm * KV_DIM:(gm + 1) * KV_DIM] for gm in range(GROUP)], axis=0)
        s = lax.dot_general(qs, kbd, (((1,), (1,)), ((), ())), preferred_element_type=F32)
        yield
        pg, rg = [], []
        for gm in range(GROUP):
            ps, rinv = [], jnp.zeros((WINDOW, KV_DIM), F32)
            for kv in range(N_KV):
                s_prev = s[gm * WINDOW:(gm + 1) * WINDOW, kv * 2 * WINDOW:kv * 2 * WINDOW + WINDOW]
                s_own = s[gm * WINDOW:(gm + 1) * WINDOW, kv * 2 * WINDOW + WINDOW:(kv + 1) * 2 * WINDOW]
                sc = jnp.where(causal, s_own, s_prev)
                if i == 0:
                    sc = jnp.where(prev_ok, sc, -jnp.inf)
                sink = sink_ref[gm, kv]
                mx = jnp.maximum(jnp.max(sc, axis=-1, keepdims=True), sink)
                p = jnp.exp(sc - mx)
                den = jnp.sum(p, axis=-1, keepdims=True) + jnp.exp(sink - mx)
                ps.append(jnp.where(causal, 0.0, p).astype(BF16))
                ps.append(jnp.where(causal, p, 0.0).astype(BF16))
                rinv = jnp.where(seg1[kv], 1.0 / den, rinv)
            pg.append(jnp.concatenate(ps, axis=1))
            rg.append(rinv)
            if gm % 2 == 1:
                yield
        og = jnp.dot(jnp.concatenate(pg, axis=0), vbd, preferred_element_type=F32)
        for gm in range(GROUP):
            a_ref[rows, gm * KV_DIM:(gm + 1) * KV_DIM] = (
                og[gm * WINDOW:(gm + 1) * WINDOW] * rg[gm]).astype(a_ref.dtype)
        yield


def _attn_sample_rows(first, q_ref, kt_ref, vt_ref, knt_ref, vnt_ref, sink_ref, ko_ref, vo_ref, a_ref):
    bb = SAMPLE_ATTN_SEQS
    lane = lax.broadcasted_iota(jnp.int32, (HEAD_DIM, WINDOW), 1)
    col = lax.broadcasted_iota(jnp.int32, (bb * N_HEADS, N_KV * WINDOW), 1)
    row = lax.broadcasted_iota(jnp.int32, (bb * N_HEADS, N_KV * WINDOW), 0)
    own = (col // WINDOW) == (row % N_KV)
    sink = jnp.concatenate([sink_ref[...]] * bb, axis=0)

    def shifted(src_ref, new_t_ref, out_ref, j):
        new = pltpu.roll(new_t_ref[...], (WINDOW - 1) - (first + j), axis=1)
        parts = []
        for kv in range(N_KV):
            hs = slice(kv * HEAD_DIM, (kv + 1) * HEAD_DIM)
            slab = jnp.where(lane == WINDOW - 1, new[hs], pltpu.roll(src_ref[j, kv], WINDOW - 1, axis=1))
            out_ref[j, kv] = slab
            parts.append(slab.astype(BF16))
        return jnp.concatenate(parts, axis=1)

    s = jnp.concatenate(
        [jnp.dot(q_ref[first + j].astype(BF16), shifted(kt_ref, knt_ref, ko_ref, j),
                 preferred_element_type=F32) for j in range(bb)], axis=0)
    yield
    s = jnp.where(own, s, -jnp.inf)
    mx = jnp.maximum(jnp.max(s, axis=-1, keepdims=True), sink)
    p = jnp.exp(s - mx)
    den = jnp.sum(p, axis=-1, keepdims=True) + jnp.exp(sink - mx)
    p = p.astype(BF16)
    yield
    hrow = lax.broadcasted_iota(jnp.int32, (N_HEADS, D_MODEL), 0)
    hcol = lax.broadcasted_iota(jnp.int32, (N_HEADS, D_MODEL), 1)
    obds = []
    for j in range(bb):
        rs = slice(j * N_HEADS, (j + 1) * N_HEADS)
        vcat = shifted(vt_ref, vnt_ref, vo_ref, j)
        o2 = lax.dot_general(p[rs], jnp.concatenate([vcat, vcat], axis=0), (((1,), (1,)), ((), ())),
                             preferred_element_type=F32) / den[rs]
        obds.append(jnp.where(hcol // HEAD_DIM == hrow,
                              jnp.concatenate([o2] * (D_MODEL // LANES), axis=1), 0.0).astype(BF16))
    yield
    pr = lax.broadcasted_iota(jnp.int32, (8, bb * N_HEADS), 0)
    pc = lax.broadcasted_iota(jnp.int32, (8, bb * N_HEADS), 1)
    pick = jnp.where(pc // N_HEADS == pr, 1.0, 0.0).astype(BF16)
    rows = jnp.dot(pick, jnp.concatenate(obds, axis=0), preferred_element_type=F32)
    for j in range(bb):
        a_ref[pl.ds(first + j, 1), :] = rows[j:j + 1]
    yield


def kernel(x_prompt, x_sample, state_pool, cache_k_win, cache_v_win, norm_g, ffn_w_gate, ffn_w_up,
           ffn_w_down, pool_w, pool_scale, kv_norm_g, w_kv, w_q, w_o, attn_sinks, final_norm_g):
    bp, seq, _ = x_prompt.shape
    bs = x_sample.shape[0]
    tm = TILE_ROWS
    assert seq % tm == 0 and tm % WINDOW == 0 and bs == LANES and bs % SAMPLE_ATTN_SEQS == 0
    assert x_prompt.shape[2] == D_MODEL and cache_k_win.shape[1:] == (WINDOW, N_KV, HEAD_DIM)
    g = lambda l, i: norm_g[l, i].reshape(1, D_MODEL)
    ffn_w = (ffn_w_gate.reshape(N_FFN, D_MODEL, D_FF), ffn_w_up.reshape(N_FFN, D_MODEL, D_FF),
             ffn_w_down.reshape(N_FFN, D_FF, D_MODEL))
    psc = pool_scale[0].reshape(1, D_MODEL)
    kvg = kv_norm_g.reshape(1, D_MODEL)
    sinks_gk = attn_sinks[0].reshape(N_KV, GROUP).T
    gfin = final_norm_g.reshape(1, D_MODEL)
    tabs_p = _rope_tables(np.arange(seq))
    tabs_s = _rope_tables(np.full((1,), PAST_LEN))

    x = x_prompt.reshape(bp * seq, D_MODEL)
    xs = x_sample.reshape(bs, D_MODEL)

    x, xs, *w_bf16 = _ffn(x, xs, g(0, 0), (*ffn_w, 0), tm=tm, seq=seq, cast_layers=tuple(range(1, N_FFN)))
    w1, w2, w3 = w_bf16[0:2], w_bf16[2:4], w_bf16[4:6]
    buf_t = jnp.transpose(state_pool[0], (1, 0, 2))
    x, xs, pool_p, nbuf_t = _ffn(x, xs, g(0, 2), w1, tm=tm, seq=seq, pool=(g(0, 1), pool_w[0], psc, buf_t))
    x, xs, k_p, v_p, k_s, v_s, wq, wo, q_s = _ffn(x, xs, g(1, 0), w2, tm=tm, seq=seq,
                                                  kv_g=kvg, wkv=w_kv, tabs_p=tabs_p, tabs_s=tabs_s,
                                                  qprep=(w_q[0], w_o[0], g(1, 1)))
    kt = jnp.transpose(cache_k_win, (0, 2, 3, 1))
    vt = jnp.transpose(cache_v_win, (0, 2, 3, 1))
    y_p, y_s, kt_new, vt_new = _ffn(
        x, xs, g(1, 2), w3, tm=tm, seq=seq, tabs_p=tabs_p, final_g=gfin,
        attn=(g(1, 1), wq, k_p, v_p, sinks_gk, wo, q_s.reshape(bs, N_HEADS, HEAD_DIM), kt, vt, k_s.T, v_s.T))

    y_prompt = y_p.reshape(bp, seq, D_MODEL)
    y_sample = y_s.reshape(bs, 1, D_MODEL)
    pool_prompt = pool_p[None]
    pool_sample = jnp.transpose(nbuf_t, (1, 0, 2))[None]
    k_win_prompt = k_p.reshape(bp, seq, KV_DIM)[:, seq - WINDOW:].reshape(bp, WINDOW, N_KV, HEAD_DIM)
    v_win_prompt = v_p.reshape(bp, seq, KV_DIM)[:, seq - WINDOW:].reshape(bp, WINDOW, N_KV, HEAD_DIM)
    k_win_sample = jnp.transpose(kt_new, (0, 3, 1, 2))
    v_win_sample = jnp.transpose(vt_new, (0, 3, 1, 2))
    return (y_prompt, y_sample, pool_prompt, pool_sample, k_win_prompt, v_win_prompt,
            k_win_sample, v_win_sample)
```

```python
import functools

import numpy as np
import jax
import jax.numpy as jnp
from jax import lax
from jax.experimental import pallas as pl
from jax.experimental.pallas import tpu as pltpu

F32 = jnp.float32
BF16 = jnp.bfloat16

D_MODEL = 1024
D_FF = 2816
HEAD_DIM = 64
N_HEADS = 16
N_KV = 4
GROUP = 4
KV_DIM = N_KV * HEAD_DIM
WINDOW = 128
ROT_DIM = 16
ROPE_THETA = 500000.0
EPS = 1e-5
POOL_WINDOWS = (2, 4, 8, 16)
POOL_GROUP = 256
POOL_BUF = 15
PAST_LEN = 16384

N_FFN = 4
LANES = 128
TILE_ROWS = 512
SAMPLE_ATTN_SEQS = 4
FF_CHUNK = 256
N_FF_CHUNKS = D_FF // FF_CHUNK
OUT_CHUNK = 256
N_OUT_CHUNKS = D_MODEL // OUT_CHUNK
VMEM_LIMIT = 60 * 1024 * 1024

GU_SLABS = 8
GU_ROWS = D_MODEL // GU_SLABS
DN_SLABS = 8
DN_ROWS = D_FF // DN_SLABS
STAGE_SLOTS = 6

CAST_GU_ROWS = 32
CAST_DN_ROWS = 128


def _const_spec(shape):
    nd = len(shape)
    return pl.BlockSpec(shape, lambda *_: (0,) * nd, pipeline_mode=pl.Buffered(1))


def _rms(x, g):
    ms = jnp.mean(x * x, axis=-1, keepdims=True)
    return x * lax.rsqrt(ms + EPS) * g


def _rope(x, c, s1, s2):
    pieces = []
    for j in range(x.shape[1] // LANES):
        xc = x[:, j * LANES:(j + 1) * LANES]
        up = pltpu.roll(xc, LANES - ROT_DIM // 2, axis=1)
        dn = pltpu.roll(xc, ROT_DIM // 2, axis=1)
        pieces.append(xc * c + up * s1 + dn * s2)
    return jnp.concatenate(pieces, axis=1) if len(pieces) > 1 else pieces[0]


def _rope_tables(pos):
    half = ROT_DIM // 2
    inv = np.power(ROPE_THETA, -np.arange(half, dtype=np.float64) * (2.0 / ROT_DIM))
    ang = np.asarray(pos, np.float64)[:, None] * inv[None, :]
    cos, sin = np.cos(ang), np.sin(ang)
    n = ang.shape[0]
    ones = np.ones((n, HEAD_DIM - ROT_DIM))
    zeros8 = np.zeros((n, half))
    zeros = np.zeros((n, HEAD_DIM - ROT_DIM))
    c = np.concatenate([cos, cos, ones], axis=1)
    s1 = np.concatenate([-sin, zeros8, zeros], axis=1)
    s2 = np.concatenate([zeros8, sin, zeros], axis=1)
    return [jnp.asarray(np.concatenate([t, t], axis=1), F32) for t in (c, s1, s2)]


def _stage_ffn_weights(wg_hbm, wu_hbm, wd_hbm, wgu_scr, wd_scr, gu_stage, dn_stage, gu_sem, dn_sem):
    chunks = []
    for src, col0 in ((wg_hbm, 0), (wu_hbm, FF_CHUNK)):
        for r in range(GU_SLABS):
            chunks.append(("gu", src, r, col0))
    for r in range(DN_SLABS):
        chunks.append(("dn", wd_hbm, r, 0))

    def copy(c):
        kind, src, r, _ = chunks[c]
        slot = c % STAGE_SLOTS
        if kind == "gu":
            return pltpu.make_async_copy(src.at[pl.ds(r * GU_ROWS, GU_ROWS), :], gu_stage.at[slot], gu_sem.at[slot])
        return pltpu.make_async_copy(src.at[pl.ds(r * DN_ROWS, DN_ROWS), :], dn_stage.at[slot], dn_sem.at[slot])

    for c in range(STAGE_SLOTS):
        copy(c).start()
    for c, (kind, _, r, col0) in enumerate(chunks):
        slot = c % STAGE_SLOTS
        copy(c).wait()
        if kind == "gu":
            for j in range(N_FF_CHUNKS):
                wgu_scr[j, r * GU_ROWS:(r + 1) * GU_ROWS, col0:col0 + FF_CHUNK] = (
                    gu_stage[slot, :, j * FF_CHUNK:(j + 1) * FF_CHUNK].astype(BF16))
        else:
            wd_scr[r * DN_ROWS:(r + 1) * DN_ROWS, :] = dn_stage[slot].astype(BF16)
        if c + STAGE_SLOTS < len(chunks):
            copy(c + STAGE_SLOTS).start()


def _run(*phase_generators):
    live = list(phase_generators)
    while live:
        for gen in list(live):
            try:
                next(gen)
            except StopIteration:
                live.remove(gen)


def _kv_rows(rows, src_ref, kvg_ref, wkv_scr, tabs, k_ref, v_ref):
    kvn = _rms(src_ref[0:rows, :], kvg_ref[...]).astype(BF16)
    yield
    kv = jnp.dot(kvn, wkv_scr[...], preferred_element_type=F32)
    yield
    tb = [t[...] for t in tabs]
    if tb[0].shape[0] != rows:
        tb = [jnp.broadcast_to(t, (rows, LANES)) for t in tb]
    k_ref[...] = _rope(kv[:, :KV_DIM], *tb)
    v_ref[...] = kv[:, KV_DIM:]
    yield


def _ffn_rows(rows, x_ref, a_ref, g_ref, gf_ref, o_ref, wgu_scr, wd_scr, wo_ref, xn_ref, h_ref, xr_ref):
    rs = slice(0, rows)
    if a_ref is not None:
        xr_ref[rs, :] = x_ref[...] + jnp.dot(a_ref[...].astype(BF16), wo_ref[...], preferred_element_type=F32)
    elif xr_ref is not None:
        xr_ref[rs, :] = x_ref[...]
    res = (lambda sl: xr_ref[rs, sl]) if xr_ref is not None else (lambda sl: x_ref[:, sl])
    xn_ref[rs, :] = _rms(res(slice(None)), g_ref[...]).astype(BF16)
    yield
    for j in range(N_FF_CHUNKS):
        r = jnp.dot(xn_ref[rs, :], wgu_scr[j], preferred_element_type=F32)
        gate, up = r[:, :FF_CHUNK], r[:, FF_CHUNK:]
        h_ref[rs, j * FF_CHUNK:(j + 1) * FF_CHUNK] = (gate * jax.nn.sigmoid(gate) * up).astype(BF16)
        yield
    for n in range(N_OUT_CHUNKS):
        sl = slice(n * OUT_CHUNK, (n + 1) * OUT_CHUNK)
        y = jnp.dot(h_ref[rs, :], wd_scr[:, sl], preferred_element_type=F32)
        o_ref[:, sl] = res(sl) + 0.5 * y
        yield
    if gf_ref is not None:
        o_ref[...] = _rms(o_ref[...], gf_ref[...])


def _cast_slabs(wg_in, wu_in, wd_in, wgu_out, wd_out):
    for j in range(N_FF_CHUNKS):
        cs = slice(j * FF_CHUNK, (j + 1) * FF_CHUNK)
        wgu_out[j, :, 0:FF_CHUNK] = wg_in[:, cs].astype(BF16)
        wgu_out[j, :, FF_CHUNK:2 * FF_CHUNK] = wu_in[:, cs].astype(BF16)
    wd_out[...] = wd_in[...].astype(BF16)
    yield


def _permute_qo(wq_ref, wo_ref, wqp_ref, wop_ref):
    lane = lax.broadcasted_iota(jnp.int32, (D_MODEL, LANES), 1)
    src_head = lambda h: (h % N_KV) * GROUP + h // N_KV
    for c in range(D_MODEL // LANES):
        halves = []
        for half in (0, 1):
            hs = src_head(2 * c + half)
            col = wq_ref[:, (hs // 2) * LANES:(hs // 2 + 1) * LANES]
            halves.append(col if hs % 2 == half else pltpu.roll(col, HEAD_DIM, axis=1))
        wqp_ref[:, c * LANES:(c + 1) * LANES] = jnp.where(lane < HEAD_DIM, halves[0], halves[1]).astype(BF16)
    for h in range(N_HEADS):
        hs = src_head(h)
        wop_ref[h * HEAD_DIM:(h + 1) * HEAD_DIM, :] = wo_ref[hs * HEAD_DIM:(hs + 1) * HEAD_DIM, :].astype(BF16)


def _ffn_kernel(*refs, nt, tm, ms, per_seq, layer, n_cast, has_pool, has_attn, has_kv, has_final, has_qprep):
    it = iter(refs)
    x_ref, xs_ref = next(it), next(it)
    a_ref = as_ref = wo_ref = gf_ref = kvg_ref = wkv_ref = None
    tabs_p = tabs_s = None
    if has_pool:
        gp_ref, pw_ref, psc_ref, sbuf_ref = next(it), next(it), next(it), next(it)
    if has_attn:
        xq_ref, gq_ref, wq_ref = next(it), next(it), next(it)
        tabs_q = [next(it) for _ in range(3)]
        kq_ref, vq_ref, sink_ref, wo_ref = [next(it) for _ in range(4)]
        q3_ref, kt_ref, vt_ref, knt_ref, vnt_ref, sinkc_ref = [next(it) for _ in range(6)]
    g_ref = next(it)
    if layer is not None:
        wg_hbm, wu_hbm, wd_hbm = next(it), next(it), next(it)
    else:
        wgu_hbm, wdn_hbm = next(it), next(it)
    cast_in = [[next(it) for _ in range(3)] for _ in range(n_cast)]
    if has_kv:
        kvg_ref, wkv_ref = next(it), next(it)
        tabs_p = [next(it) for _ in range(3)]
        tabs_s = [next(it) for _ in range(3)]
    if has_final:
        gf_ref = next(it)
    if has_qprep:
        wqf_ref, wof_ref, gqs_ref = next(it), next(it), next(it)
        tabs_qs = [next(it) for _ in range(3)]
    o_ref, os_ref = next(it), next(it)
    if has_attn:
        ko_ref, vo_ref = next(it), next(it)
    k_ref = v_ref = ks_ref = vs_ref = None
    if has_kv:
        k_ref, v_ref, ks_ref, vs_ref = next(it), next(it), next(it), next(it)
    cast_out = [[next(it) for _ in range(2)] for _ in range(n_cast)]
    if has_pool:
        pbuf_ref, snbuf_ref = next(it), next(it)
    if has_qprep:
        wqp_ref, wop_ref, qs_ref = next(it), next(it), next(it)
    wgu_scr, wd_scr = next(it), next(it)
    if layer is not None:
        gu_stage, dn_stage, gu_sem, dn_sem = [next(it) for _ in range(4)]
    else:
        w_sem = next(it)
    xn_ref, h_ref = next(it), next(it)
    xr_ref = wkv_scr = None
    if has_attn:
        a_ref, xr_ref, kbuf, vbuf, as_ref = [next(it) for _ in range(5)]
    if has_pool:
        x2_scr, xr_ref, hext_ref, xs2_scr = [next(it) for _ in range(4)]
    if has_kv:
        wkv_scr = next(it)

    i = pl.program_id(0)

    def pool(tile):
        return _pool_rows(tile % per_seq, x_ref, gp_ref, pw_ref, psc_ref, x2_scr, pbuf_ref, hext_ref, tm)

    def attention(tile):
        nprev = jnp.where(tile % per_seq == 0, 0, WINDOW)
        return _attn_rows(nprev, xq_ref, gq_ref, wq_ref, tabs_q, kq_ref, vq_ref, sink_ref, a_ref,
                          kbuf, vbuf, tm)

    @pl.when(i == 0)
    def _():
        if layer is not None:
            _stage_ffn_weights(wg_hbm.at[layer], wu_hbm.at[layer], wd_hbm.at[layer], wgu_scr, wd_scr,
                               gu_stage, dn_stage, gu_sem, dn_sem)
        else:
            copies = [pltpu.make_async_copy(wgu_hbm, wgu_scr, w_sem.at[0]),
                      pltpu.make_async_copy(wdn_hbm, wd_scr, w_sem.at[1])]
            for c in copies:
                c.start()
        if has_attn:
            kbuf[...] = jnp.zeros(kbuf.shape, F32)
            vbuf[...] = jnp.zeros(vbuf.shape, F32)
            _run(attention(0))
        if has_pool:
            hext_ref[...] = jnp.zeros(hext_ref.shape, F32)
            _run(pool(0))
        if has_kv:
            wkv_scr[...] = wkv_ref[...].astype(BF16)
        if has_qprep:
            _permute_qo(wqf_ref, wof_ref, wqp_ref, wop_ref)
        if layer is None:
            for c in copies:
                c.wait()

    common = (wgu_scr, wd_scr, wo_ref, xn_ref, h_ref, xr_ref)

    @pl.when((i >= 1) & (i <= nt))
    def _():
        side = [_cast_slabs(*cast_in[c], *cast_out[c]) for c in range(n_cast)]
        if has_kv:
            side.append(_kv_rows(tm, x_ref, kvg_ref, wkv_scr, tabs_p, k_ref, v_ref))
        if has_attn:
            side.append(attention(jnp.minimum(i, nt - 1)))
            side.append(_attn_sample_rows((i - 1) * SAMPLE_ATTN_SEQS, q3_ref, kt_ref, vt_ref, knt_ref, vnt_ref,
                                          sinkc_ref, ko_ref, vo_ref, as_ref))
        if has_pool:
            side.append(pool(jnp.minimum(i, nt - 1)))
            blk = (i - 1) * (ms // POOL_S_ROWS) // nt
            side.append(_pool_sample_rows(blk * POOL_S_ROWS, xs_ref, sbuf_ref, gp_ref, pw_ref, psc_ref,
                                          xs2_scr, snbuf_ref))
        xin = x2_scr if has_pool else x_ref
        _run(_ffn_rows(tm, xin, a_ref, g_ref, gf_ref, o_ref, *common), *side)

    @pl.when(i == nt + 1)
    def _():
        side = [_kv_rows(ms, xs_ref, kvg_ref, wkv_scr, tabs_s, ks_ref, vs_ref)] if has_kv else []
        xsin = xs2_scr if has_pool else xs_ref
        _run(_ffn_rows(ms, xsin, as_ref, g_ref, gf_ref, os_ref, *common), *side)
        if has_qprep:
            qn = _rms(os_ref[...], gqs_ref[...]).astype(BF16)
            y = jnp.dot(qn, wqp_ref[...], preferred_element_type=F32)
            tb = [jnp.broadcast_to(t[...], (ms, LANES)) for t in tabs_qs]
            qs_ref[...] = _rope(y, *tb) * (HEAD_DIM ** -0.5)


def _ffn(x, xs, g, weights, *, tm, seq, cast_layers=(), pool=None, attn=None, kv_g=None, wkv=None,
         tabs_p=None, tabs_s=None, final_g=None, qprep=None):
    mp, ms = x.shape[0], xs.shape[0]
    nt = mp // tm
    per_seq = seq // tm
    layer = weights[3] if len(weights) == 4 else None
    n_cast = len(cast_layers)
    assert n_cast == 0 or (nt * CAST_GU_ROWS == D_MODEL and nt * CAST_DN_ROWS >= D_FF)
    has_attn, has_kv, has_final = attn is not None, kv_g is not None, final_g is not None
    has_pool = pool is not None
    assert not (has_kv and has_pool)
    tile = lambda i: jnp.clip(i - 1, 0, nt - 1)
    nxt = lambda i: jnp.minimum(i, nt - 1)
    row = lambda w: pl.BlockSpec((tm, w), lambda i: (tile(i), 0))
    nrow = lambda w: pl.BlockSpec((tm, w), lambda i: (nxt(i), 0))
    hbm = pl.BlockSpec(memory_space=pl.ANY)

    args, specs = [x, xs], [nrow(D_MODEL) if has_pool else row(D_MODEL), _const_spec(xs.shape)]
    sblk = pl.BlockSpec((POOL_BUF, POOL_S_ROWS, D_MODEL), lambda i: (0, tile(i) * (ms // POOL_S_ROWS) // nt, 0))
    if has_pool:
        args += list(pool)
        specs += [_const_spec(a.shape) for a in pool[:3]] + [sblk]
    if has_attn:
        gq, wq, k, v, sinks, wo, q3, kt, vt, knt, vnt = attn
        assert ms == nt * SAMPLE_ATTN_SEQS
        sinks_col = sinks.reshape(N_HEADS, 1)
        cblk = pl.BlockSpec((SAMPLE_ATTN_SEQS, N_KV, HEAD_DIM, WINDOW), lambda i: (tile(i), 0, 0, 0))
        args += [x, gq, wq, *tabs_p, k, v, sinks, wo, q3, kt, vt, knt, vnt, sinks_col]
        specs += [nrow(D_MODEL), _const_spec(gq.shape), _const_spec(wq.shape)]
        specs += [pl.BlockSpec((tm, LANES), lambda i: (nxt(i) % per_seq, 0))] * 3
        specs += [nrow(KV_DIM), nrow(KV_DIM), pl.BlockSpec(memory_space=pltpu.SMEM), _const_spec(wo.shape)]
        specs += [_const_spec(q3.shape), cblk, cblk, _const_spec(knt.shape), _const_spec(vnt.shape),
                  _const_spec(sinks_col.shape)]
    args += [g, *weights[:3]] if layer is not None else [g, *weights]
    specs += [_const_spec(g.shape)] + [hbm] * (3 if layer is not None else 2)
    dn_blocks = D_FF // CAST_DN_ROWS
    dn_tile = lambda i: jnp.minimum(tile(i), dn_blocks - 1)
    for cl in cast_layers:
        args += list(weights[:3])
        specs += [pl.BlockSpec((None, CAST_GU_ROWS, D_FF), lambda i, cl=cl: (cl, tile(i), 0)),
                  pl.BlockSpec((None, CAST_GU_ROWS, D_FF), lambda i, cl=cl: (cl, tile(i), 0)),
                  pl.BlockSpec((None, CAST_DN_ROWS, D_MODEL), lambda i, cl=cl: (cl, dn_tile(i), 0))]
    if has_kv:
        args += [kv_g, wkv, *tabs_p, *tabs_s]
        specs += [_const_spec(kv_g.shape), _const_spec(wkv.shape)]
        specs += [pl.BlockSpec((tm, LANES), lambda i: (tile(i) % per_seq, 0))] * 3
        specs += [_const_spec((1, LANES))] * 3
    if has_final:
        args.append(final_g)
        specs.append(_const_spec(final_g.shape))
    has_qprep = qprep is not None
    if has_qprep:
        args += [*qprep, *tabs_s]
        specs += [_const_spec(a.shape) for a in qprep] + [_const_spec((1, LANES))] * 3

    out_shape = [jax.ShapeDtypeStruct((mp, D_MODEL), F32), jax.ShapeDtypeStruct((ms, D_MODEL), F32)]
    out_specs = [row(D_MODEL), _const_spec((ms, D_MODEL))]
    if has_attn:
        out_shape += [jax.ShapeDtypeStruct(kt.shape, F32), jax.ShapeDtypeStruct(vt.shape, F32)]
        out_specs += [cblk, cblk]
    if has_kv:
        out_shape += [jax.ShapeDtypeStruct((mp, KV_DIM), F32)] * 2 + [jax.ShapeDtypeStruct((ms, KV_DIM), F32)] * 2
        out_specs += [row(KV_DIM)] * 2 + [_const_spec((ms, KV_DIM))] * 2
    wgu_shape, wdn_shape = (N_FF_CHUNKS, D_MODEL, 2 * FF_CHUNK), (D_FF, D_MODEL)
    for _ in cast_layers:
        out_shape += [jax.ShapeDtypeStruct(wgu_shape, BF16), jax.ShapeDtypeStruct(wdn_shape, BF16)]
        out_specs += [pl.BlockSpec((N_FF_CHUNKS, CAST_GU_ROWS, 2 * FF_CHUNK), lambda i: (0, tile(i), 0)),
                      pl.BlockSpec((CAST_DN_ROWS, D_MODEL), lambda i: (dn_tile(i), 0))]
    if has_pool:
        out_shape += [jax.ShapeDtypeStruct((mp // seq, POOL_BUF, D_MODEL), F32),
                      jax.ShapeDtypeStruct(pool[3].shape, F32)]
        out_specs += [pl.BlockSpec((1, POOL_BUF, D_MODEL), lambda i: (nxt(i) // per_seq, 0, 0)), sblk]
    if has_qprep:
        out_shape += [jax.ShapeDtypeStruct((D_MODEL, D_MODEL), BF16)] * 2 + [jax.ShapeDtypeStruct((ms, D_MODEL), F32)]
        out_specs += [_const_spec((D_MODEL, D_MODEL))] * 2 + [_const_spec((ms, D_MODEL))]

    scratch = [pltpu.VMEM(wgu_shape, BF16), pltpu.VMEM(wdn_shape, BF16)]
    if layer is not None:
        scratch += [pltpu.VMEM((STAGE_SLOTS, GU_ROWS, D_FF), F32),
                    pltpu.VMEM((STAGE_SLOTS, DN_ROWS, D_MODEL), F32),
                    pltpu.SemaphoreType.DMA((STAGE_SLOTS,)),
                    pltpu.SemaphoreType.DMA((STAGE_SLOTS,))]
    else:
        scratch += [pltpu.SemaphoreType.DMA((2,))]
    scratch += [pltpu.VMEM((tm, D_MODEL), BF16),
                pltpu.VMEM((tm, D_FF), BF16)]
    if has_attn:
        scratch += [pltpu.VMEM((tm, D_MODEL), BF16), pltpu.VMEM((tm, D_MODEL), F32),
                    pltpu.VMEM((WINDOW + tm, KV_DIM), F32), pltpu.VMEM((WINDOW + tm, KV_DIM), F32),
                    pltpu.VMEM((ms, D_MODEL), F32)]
    if has_pool:
        scratch += [pltpu.VMEM((tm, D_MODEL), F32), pltpu.VMEM((tm, D_MODEL), F32),
                    pltpu.VMEM((HALO + tm, D_MODEL), F32), pltpu.VMEM((ms, D_MODEL), F32)]
    if has_kv:
        scratch += [pltpu.VMEM((D_MODEL, 2 * KV_DIM), BF16)]

    return pl.pallas_call(
        functools.partial(_ffn_kernel, nt=nt, tm=tm, ms=ms, per_seq=per_seq, layer=layer, n_cast=n_cast,
                          has_pool=has_pool, has_attn=has_attn, has_kv=has_kv, has_final=has_final,
                          has_qprep=has_qprep),
        grid=(nt + 2,),
        in_specs=specs,
        out_specs=out_specs,
        out_shape=out_shape,
        scratch_shapes=scratch,
        compiler_params=pltpu.CompilerParams(
            dimension_semantics=("arbitrary",), vmem_limit_bytes=VMEM_LIMIT),
        name="ffn",
    )(*args)


HALO = 16
assert POOL_WINDOWS == tuple(2 ** (i + 1) for i in range(len(POOL_WINDOWS))) and max(POOL_WINDOWS) <= HALO


def _pool_rows(t_seq, x_ref, g_ref, w_ref, sc_ref, o_ref, buf_ref, hext_ref, tp):
    keep = jnp.where(t_seq == 0, 0, HALO)
    hrow = lax.broadcasted_iota(jnp.int32, (HALO, D_MODEL), 0)
    hext_ref[0:HALO, :] = jnp.where(hrow < keep, hext_ref[tp:tp + HALO, :], 0.0)
    x = x_ref[...]
    h = _rms(x, g_ref[...])
    hext_ref[HALO:HALO + tp, :] = h
    buf_ref[0] = hext_ref[HALO + tp - POOL_BUF:HALO + tp, :]
    yield
    pos = (t_seq * tp + lax.broadcasted_iota(jnp.int32, (tp, 1), 0)).astype(F32)
    sums, cur, shift = [], hext_ref[...], 1
    for gi in range(len(POOL_WINDOWS)):
        cur = cur + pltpu.roll(cur, shift, axis=0)
        sums.append(cur[HALO:, :POOL_GROUP])
        if gi + 1 < len(POOL_WINDOWS):
            cur, shift = cur[:, POOL_GROUP:], 2 * shift
        yield
    for gi, w in enumerate(POOL_WINDOWS):
        cs = slice(gi * POOL_GROUP, (gi + 1) * POOL_GROUP)
        cnt = jnp.minimum(float(w), pos + 1.0)
        diff = (sums[gi] / cnt - hext_ref[HALO:HALO + tp, cs]).astype(BF16)
        mixed = jnp.dot(diff, w_ref[gi].astype(BF16), preferred_element_type=F32)
        o_ref[:, cs] = x_ref[:, cs] + mixed * sc_ref[:, cs]
        yield


POOL_S_ROWS = 8


def _pool_sample_rows(r0, x_ref, buf_ref, g_ref, w_ref, sc_ref, o_ref, nbuf_ref):
    rows = pl.ds(pl.multiple_of(r0, POOL_S_ROWS), POOL_S_ROWS)
    x = x_ref[rows, :]
    h = _rms(x, g_ref[...])
    nbuf_ref[0:POOL_BUF - 1] = buf_ref[1:POOL_BUF]
    nbuf_ref[POOL_BUF - 1] = h
    yield
    mixed = []
    for gi, w in enumerate(POOL_WINDOWS):
        cs = slice(gi * POOL_GROUP, (gi + 1) * POOL_GROUP)
        s = h[:, cs]
        for k in range(1, w):
            s = s + buf_ref[POOL_BUF - k, :, cs]
        diff = (s / float(w) - h[:, cs]).astype(BF16)
        mixed.append(jnp.dot(diff, w_ref[gi].astype(BF16), preferred_element_type=F32))
    o_ref[rows, :] = x + jnp.concatenate(mixed, axis=1) * sc_ref[...]
    yield


def _seg_mask(rows):
    lane = lax.broadcasted_iota(jnp.int32, (rows, KV_DIM), 1)
    return [(lane >= kv * HEAD_DIM) & (lane < (kv + 1) * HEAD_DIM) for kv in range(N_KV)]


def _attn_rows(nprev, x_ref, g_ref, wq_ref, tabs, k_ref, v_ref, sink_ref, a_ref, kbuf, vbuf, tq):
    hrow = lax.broadcasted_iota(jnp.int32, (WINDOW, KV_DIM), 0)
    kbuf[0:WINDOW, :] = jnp.where(hrow < nprev, kbuf[tq:tq + WINDOW, :], 0.0)
    vbuf[0:WINDOW, :] = jnp.where(hrow < nprev, vbuf[tq:tq + WINDOW, :], 0.0)
    kbuf[WINDOW:WINDOW + tq, :] = k_ref[...]
    vbuf[WINDOW:WINDOW + tq, :] = v_ref[...]

    hn = _rms(x_ref[...], g_ref[...]).astype(BF16)
    q = jnp.dot(hn, wq_ref[...], preferred_element_type=F32)
    q = (_rope(q, *[t[...] for t in tabs]) * (HEAD_DIM ** -0.5)).astype(BF16)
    yield

    seg2 = _seg_mask(2 * WINDOW)
    seg1 = _seg_mask(WINDOW)
    qi = lax.broadcasted_iota(jnp.int32, (WINDOW, WINDOW), 0)
    kj = lax.broadcasted_iota(jnp.int32, (WINDOW, WINDOW), 1)
    causal = kj <= qi
    prev_ok = kj <= qi + nprev

    for i in range(tq // WINDOW):
        kk = kbuf[i * WINDOW:(i + 2) * WINDOW, :]
        vv = vbuf[i * WINDOW:(i + 2) * WINDOW, :]
        kbd = jnp.concatenate([jnp.where(m, kk, 0.0).astype(BF16) for m in seg2], axis=0)
        vbd = jnp.concatenate([jnp.where(m, vv, 0.0).astype(BF16) for m in seg2], axis=0)
        rows = slice(i * WINDOW, (i + 1) * WINDOW)
        qs = jnp.concatenate([q[rows, gm * KV_DIM:(gm + 1) * KV_DIM] for gm in range(GROUP)], axis=0)
        s = lax.dot_general(qs, kbd, (((1,), (1,)), ((), ())), preferred_element_type=F32)
        yield
        pg, rg = [], []
        for gm in range(GROUP):
            ps, rinv = [], jnp.zeros((WINDOW, KV_DIM), F32)
            for kv in range(N_KV):
                s_prev = s[gm * WINDOW:(gm + 1) * WINDOW, kv * 2 * WINDOW:kv * 2 * WINDOW + WINDOW]
                s_own = s[gm * WINDOW:(gm + 1) * WINDOW, kv * 2 * WINDOW + WINDOW:(kv + 1) * 2 * WINDOW]
                sc = jnp.where(causal, s_own, s_prev)
                if i == 0:
                    sc = jnp.where(prev_ok, sc, -jnp.inf)
                sink = sink_ref[gm, kv]
                mx = jnp.maximum(jnp.max(sc, axis=-1, keepdims=True), sink)
                p = jnp.exp(sc - mx)
                den = jnp.sum(p, axis=-1, keepdims=True) + jnp.exp(sink - mx)
                ps.append(jnp.where(causal, 0.0, p).astype(BF16))
                ps.append(jnp.where(causal, p, 0.0).astype(BF16))
                rinv = jnp.where(seg1[kv], 1.0 / den, rinv)
            pg.append(jnp.concatenate(ps, axis=1))
            rg.append(rinv)
            if gm % 2 == 1:
                yield
        og = jnp.dot(jnp.concatenate(pg, axis=0), vbd, preferred_element_type=F32)
        for gm in range(GROUP):
            a_ref[rows, gm * KV_DIM:(gm + 1) * KV_DIM] = (
                og[gm * WINDOW:(gm + 1) * WINDOW] * rg[gm]).astype(a_ref.dtype)
        yield


def _attn_sample_rows(first, q_ref, kt_ref, vt_ref, knt_ref, vnt_ref, sink_ref, ko_ref, vo_ref, a_ref):
    bb = SAMPLE_ATTN_SEQS
    lane = lax.broadcasted_iota(jnp.int32, (HEAD_DIM, WINDOW), 1)
    col = lax.broadcasted_iota(jnp.int32, (bb * N_HEADS, N_KV * WINDOW), 1)
    row = lax.broadcasted_iota(jnp.int32, (bb * N_HEADS, N_KV * WINDOW), 0)
    own = (col // WINDOW) == (row % N_KV)
    sink = jnp.concatenate([sink_ref[...]] * bb, axis=0)

    def shifted(src_ref, new_t_ref, out_ref, j):
        new = pltpu.roll(new_t_ref[...], (WINDOW - 1) - (first + j), axis=1)
        parts = []
        for kv in range(N_KV):
            hs = slice(kv * HEAD_DIM, (kv + 1) * HEAD_DIM)
            slab = jnp.where(lane == WINDOW - 1, new[hs], pltpu.roll(src_ref[j, kv], WINDOW - 1, axis=1))
            out_ref[j, kv] = slab
            parts.append(slab.astype(BF16))
        return jnp.concatenate(parts, axis=1)

    s = jnp.concatenate(
        [jnp.dot(q_ref[first + j].astype(BF16), shifted(kt_ref, knt_ref, ko_ref, j),
                 preferred_element_type=F32) for j in range(bb)], axis=0)
    yield
    s = jnp.where(own, s, -jnp.inf)
    mx = jnp.maximum(jnp.max(s, axis=-1, keepdims=True), sink)
    p = jnp.exp(s - mx)
    den = jnp.sum(p, axis=-1, keepdims=True) + jnp.exp(sink - mx)
    p = p.astype(BF16)
    yield
    hrow = lax.broadcasted_iota(jnp.int32, (N_HEADS, D_MODEL), 0)
    hcol = lax.broadcasted_iota(jnp.int32, (N_HEADS, D_MODEL), 1)
    obds = []
    for j in range(bb):
        rs = slice(j * N_HEADS, (j + 1) * N_HEADS)
        vcat = shifted(vt_ref, vnt_ref, vo_ref, j)
        o2 = lax.dot_general(p[rs], jnp.concatenate([vcat, vcat], axis=0), (((1,), (1,)), ((), ())),
                             preferred_element_type=F32) / den[rs]
        obds.append(jnp.where(hcol // HEAD_DIM == hrow,
                              jnp.concatenate([o2] * (D_MODEL // LANES), axis=1), 0.0).astype(BF16))
    yield
    pr = lax.broadcasted_iota(jnp.int32, (8, bb * N_HEADS), 0)
    pc = lax.broadcasted_iota(jnp.int32, (8, bb * N_HEADS), 1)
    pick = jnp.where(pc // N_HEADS == pr, 1.0, 0.0).astype(BF16)
    rows = jnp.dot(pick, jnp.concatenate(obds, axis=0), preferred_element_type=F32)
    for j in range(bb):
        a_ref[pl.ds(first + j, 1), :] = rows[j:j + 1]
    yield


def kernel(x_prompt, x_sample, state_pool, cache_k_win, cache_v_win, norm_g, ffn_w_gate, ffn_w_up,
           ffn_w_down, pool_w, pool_scale, kv_norm_g, w_kv, w_q, w_o, attn_sinks, final_norm_g):
    bp, seq, _ = x_prompt.shape
    bs = x_sample.shape[0]
    tm = TILE_ROWS
    assert seq % tm == 0 and tm % WINDOW == 0 and bs == LANES and bs % SAMPLE_ATTN_SEQS == 0
    assert x_prompt.shape[2] == D_MODEL and cache_k_win.shape[1:] == (WINDOW, N_KV, HEAD_DIM)
    g = lambda l, i: norm_g[l, i].reshape(1, D_MODEL)
    ffn_w = (ffn_w_gate.reshape(N_FFN, D_MODEL, D_FF), ffn_w_up.reshape(N_FFN, D_MODEL, D_FF),
             ffn_w_down.reshape(N_FFN, D_FF, D_MODEL))
    psc = pool_scale[0].reshape(1, D_MODEL)
    kvg = kv_norm_g.reshape(1, D_MODEL)
    sinks_gk = attn_sinks[0].reshape(N_KV, GROUP).T
    gfin = final_norm_g.reshape(1, D_MODEL)
    tabs_p = _rope_tables(np.arange(seq))
    tabs_s = _rope_tables(np.full((1,), PAST_LEN))

    x = x_prompt.reshape(bp * seq, D_MODEL)
    xs = x_sample.reshape(bs, D_MODEL)

    x, xs, *w_bf16 = _ffn(x, xs, g(0, 0), (*ffn_w, 0), tm=tm, seq=seq, cast_layers=tuple(range(1, N_FFN)))
    w1, w2, w3 = w_bf16[0:2], w_bf16[2:4], w_bf16[4:6]
    buf_t = jnp.transpose(state_pool[0], (1, 0, 2))
    x, xs, pool_p, nbuf_t = _ffn(x, xs, g(0, 2), w1, tm=tm, seq=seq, pool=(g(0, 1), pool_w[0], psc, buf_t))
    x, xs, k_p, v_p, k_s, v_s, wq, wo, q_s = _ffn(x, xs, g(1, 0), w2, tm=tm, seq=seq,
                                                  kv_g=kvg, wkv=w_kv, tabs_p=tabs_p, tabs_s=tabs_s,
                                                  qprep=(w_q[0], w_o[0], g(1, 1)))
    kt = jnp.transpose(cache_k_win, (0, 2, 3, 1))
    vt = jnp.transpose(cache_v_win, (0, 2, 3, 1))
    y_p, y_s, kt_new, vt_new = _ffn(
        x, xs, g(1, 2), w3, tm=tm, seq=seq, tabs_p=tabs_p, final_g=gfin,
        attn=(g(1, 1), wq, k_p, v_p, sinks_gk, wo, q_s.reshape(bs, N_HEADS, HEAD_DIM), kt, vt, k_s.T, v_s.T))

    y_prompt = y_p.reshape(bp, seq, D_MODEL)
    y_sample = y_s.reshape(bs, 1, D_MODEL)
    pool_prompt = pool_p[None]
    pool_sample = jnp.transpose(nbuf_t, (1, 0, 2))[None]
    k_win_prompt = k_p.reshape(bp, seq, KV_DIM)[:, seq - WINDOW:].reshape(bp, WINDOW, N_KV, HEAD_DIM)
    v_win_prompt = v_p.reshape(bp, seq, KV_DIM)[:, seq - WINDOW:].reshape(bp, WINDOW, N_KV, HEAD_DIM)
    k_win_sample = jnp.transpose(kt_new, (0, 3, 1, 2))
    v_win_sample = jnp.transpose(vt_new, (0, 3, 1, 2))
    return (y_prompt, y_sample, pool_prompt, pool_sample, k_win_prompt, v_win_prompt,
            k_win_sample, v_win_sample)
```
